```python
import math
import jax, jax.numpy as jnp
from jax import lax
import numpy as np

D_MODEL = 1024
BATCH = 2
SEQ = 8192
DEPTH = 1

MEM_LEN = 256
RET_HEADS = 8
RET_DK = 64
RET_DV = 64
RET_CHUNK = 128
RET_WIDTH = RET_HEADS * RET_DV
RET_THETA_BASE = 10000.0
DIFF_HEADS = 4
DIFF_DQK = 64
DIFF_DV = 2 * DIFF_DQK
DIFF_WIDTH = DIFF_HEADS * DIFF_DV
Q_BLOCK = 128
MIX_WIDTH = RET_WIDTH + DIFF_WIDTH
RET_QK_COLS = RET_HEADS * RET_DK
DIFF_QK_COLS = DIFF_HEADS * 2 * DIFF_DQK
IN_SIZES = (RET_QK_COLS, RET_QK_COLS, RET_WIDTH, RET_WIDTH, DIFF_QK_COLS, DIFF_QK_COLS, DIFF_WIDTH)
IN_COLS = 2 * RET_QK_COLS + 2 * RET_WIDTH + 2 * DIFF_QK_COLS + DIFF_WIDTH
XATTN_HEADS = 4
XATTN_DH = D_MODEL // XATTN_HEADS
FFN_HIDDEN = -(-(8 * D_MODEL) // (3 * 256)) * 256
EPS = 1e-6

kernel_name = "hymba_retention_diffattn_memxattn_swiglu"


def rms_norm(x, gain=None):
    xf = x.astype(jnp.float32)
    y = xf * lax.rsqrt(jnp.mean(xf * xf, axis=-1, keepdims=True) + EPS)
    if gain is not None:
        y = y * gain.astype(jnp.float32)
    return y.astype(x.dtype)


def rotary(x, pos):
    half = x.shape[-1] // 2
    inv = 1.0 / (RET_THETA_BASE ** jnp.linspace(0.0, 1.0, half, dtype=jnp.float32))
    ang = pos.astype(jnp.float32)[:, None] * inv[None, :]
    cos, sin = jnp.cos(ang), jnp.sin(ang)
    x1 = x[..., :half].astype(jnp.float32)
    x2 = x[..., half:].astype(jnp.float32)
    out = jnp.concatenate([x1 * cos - x2 * sin, x2 * cos + x1 * sin], axis=-1)
    return out.astype(x.dtype)


def retention_chunkwise(q, k, v):
    B, H, S, dk = q.shape
    dv = v.shape[-1]
    C = RET_CHUNK
    N = S // C
    log_g = jnp.log1p(-(2.0 ** (-5.0 - jnp.arange(H, dtype=jnp.float32))))
    idx = jnp.arange(C, dtype=jnp.float32)
    rel = idx[:, None] - idx[None, :]
    dmask = jnp.where(rel >= 0, jnp.exp(log_g[:, None, None] * jnp.maximum(rel, 0.0)), 0.0)
    zeta = jnp.exp(log_g[:, None] * (C - 1.0 - idx))
    xi = jnp.exp(log_g[:, None] * (idx + 1.0))
    g_chunk = jnp.exp(log_g * C)

    qc = q.reshape(B, H, N, C, dk)
    kc = k.reshape(B, H, N, C, dk)
    vc = v.reshape(B, H, N, C, dv)

    scores = jnp.einsum('bhncd,bhnjd->bhncj', qc, kc) * dmask[None, :, None]
    o_intra = jnp.einsum('bhncj,bhnjv->bhncv', scores, vc)

    kv = jnp.einsum('bhncd,bhncv->nbhdv', kc * zeta[None, :, None, :, None], vc).astype(jnp.float32)

    def step(R, kv_n):
        return R * g_chunk[None, :, None, None] + kv_n, R

    _, r_prev = lax.scan(step, jnp.zeros((B, H, dk, dv), jnp.float32), kv)
    o_inter = jnp.einsum('bhncd,nbhdv->bhncv', qc, r_prev) * xi[None, :, None, :, None]
    return (o_intra + o_inter).reshape(B, H, S, dv)


def diff_attention_causal(q, k, v, lam):
    B, H, _, S, dq = q.shape
    dv = v.shape[-1]
    nb = S // Q_BLOCK
    qb = q.reshape(B, H, 2, nb, Q_BLOCK, dq).transpose(3, 0, 1, 2, 4, 5)
    kpos = jnp.arange(S)
    scale = dq ** -0.5

    def block(args):
        q_blk, i = args
        qpos = i * Q_BLOCK + jnp.arange(Q_BLOCK)
        s = jnp.einsum('bhmqd,bhmkd->bhmqk', q_blk, k).astype(jnp.float32) * scale
        s = jnp.where(kpos[None, :] <= qpos[:, None], s, -jnp.inf)
        p = jax.nn.softmax(s, axis=-1)
        a = p[:, :, 0] - lam * p[:, :, 1]
        return jnp.einsum('bhqk,bhkd->bhqd', a.astype(v.dtype), v)

    o = lax.map(block, (qb, jnp.arange(nb)))
    return o.transpose(1, 2, 0, 3, 4).reshape(B, H, S, dv)


def setup_inputs(seed: int = 0) -> dict:
    key = jax.random.key(seed)
    ks = jax.random.split(key, 24)
    f32 = jnp.float32

    def w(k, shape, fan_in):
        return jax.random.normal(k, shape, f32) * (fan_in ** -0.5)

    def gain(k, shape):
        return 1.0 + 0.02 * jax.random.normal(k, shape, f32)

    L = DEPTH
    return {
        "x": jax.random.normal(ks[0], (BATCH, SEQ, D_MODEL), f32),
        "mem": jax.random.normal(ks[1], (BATCH, MEM_LEN, D_MODEL), f32),
        "norm_mix": gain(ks[2], (L, D_MODEL)),
        "w_in": w(ks[3], (L, D_MODEL, IN_COLS), D_MODEL),
        "diff_q_gain": gain(ks[4], (L, DIFF_DQK)),
        "diff_k_gain": gain(ks[5], (L, DIFF_DQK)),
        "diff_lambda": 0.1 * jax.random.normal(ks[6], (L, 4, DIFF_DQK), f32),
        "diff_subln": gain(ks[7], (L, DIFF_DV)),
        "group_scale": gain(ks[8], (L, MIX_WIDTH)),
        "w_out": w(ks[9], (L, MIX_WIDTH, D_MODEL), MIX_WIDTH),
        "norm_x": gain(ks[10], (L, D_MODEL)),
        "norm_mem": gain(ks[11], (L, D_MODEL)),
        "xq": w(ks[12], (L, D_MODEL, D_MODEL), D_MODEL),
        "xkv": w(ks[13], (L, D_MODEL, 2 * D_MODEL), D_MODEL),
        "xq_gain": gain(ks[14], (L, XATTN_DH)),
        "xk_gain": gain(ks[15], (L, XATTN_DH)),
        "xo": w(ks[16], (L, D_MODEL, D_MODEL), D_MODEL),
        "norm_ffn": gain(ks[17], (L, D_MODEL)),
        "w_gate": w(ks[18], (L, D_MODEL, FFN_HIDDEN), D_MODEL),
        "w_up": w(ks[19], (L, D_MODEL, FFN_HIDDEN), D_MODEL),
        "w_down": w(ks[20], (L, FFN_HIDDEN, D_MODEL), FFN_HIDDEN),
    }


def reference(x, mem, norm_mix, w_in, diff_q_gain, diff_k_gain, diff_lambda, diff_subln,
              group_scale, w_out, norm_x, norm_mem, xq, xkv, xq_gain, xk_gain, xo,
              norm_ffn, w_gate, w_up, w_down):
    B, S, D = x.shape
    M = mem.shape[1]
    pos = jnp.arange(S)
    split_at = list(np.cumsum(IN_SIZES)[:-1])

    for l in range(DEPTH):
        lam_init = 0.8 - 0.6 * math.exp(-0.3 * l)

        h = rms_norm(x, norm_mix[l])
        proj = h @ w_in[l]
        rq, rk, rv, rg, dq, dk, dv = jnp.split(proj, split_at, axis=-1)

        rq = rq.reshape(B, S, RET_HEADS, RET_DK).transpose(0, 2, 1, 3)
        rk = rk.reshape(B, S, RET_HEADS, RET_DK).transpose(0, 2, 1, 3)
        rv = rv.reshape(B, S, RET_HEADS, RET_DV).transpose(0, 2, 1, 3)
        rq = rotary(rq, pos)
        rk = rotary(rk, pos) * (RET_DK ** -0.5)
        o_ret = rms_norm(retention_chunkwise(rq, rk, rv).astype(x.dtype))
        o_ret = o_ret.transpose(0, 2, 1, 3).reshape(B, S, RET_WIDTH) * jax.nn.silu(rg)

        dq = dq.reshape(B, S, DIFF_HEADS, 2, DIFF_DQK).transpose(0, 2, 3, 1, 4)
        dk = dk.reshape(B, S, DIFF_HEADS, 2, DIFF_DQK).transpose(0, 2, 3, 1, 4)
        dv = dv.reshape(B, S, DIFF_HEADS, DIFF_DV).transpose(0, 2, 1, 3)
        dq = rms_norm(dq, diff_q_gain[l])
        dk = rms_norm(dk, diff_k_gain[l])
        lv = diff_lambda[l].astype(jnp.float32)
        lam = jnp.exp(jnp.sum(lv[0] * lv[1])) - jnp.exp(jnp.sum(lv[2] * lv[3])) + lam_init
        o_diff = diff_attention_causal(dq, dk, dv, lam)
        o_diff = rms_norm(o_diff, diff_subln[l]) * (1.0 - lam_init)
        o_diff = o_diff.transpose(0, 2, 1, 3).reshape(B, S, DIFF_WIDTH)

        mix = jnp.concatenate([o_ret, o_diff], axis=-1) * group_scale[l]
        x = x + mix @ w_out[l]

        h = rms_norm(x, norm_x[l])
        m = rms_norm(mem, norm_mem[l])
        q = (h @ xq[l]).reshape(B, S, XATTN_HEADS, XATTN_DH)
        k, v = jnp.split(m @ xkv[l], 2, axis=-1)
        k = k.reshape(B, M, XATTN_HEADS, XATTN_DH)
        v = v.reshape(B, M, XATTN_HEADS, XATTN_DH)
        q = rms_norm(q, xq_gain[l])
        k = rms_norm(k, xk_gain[l])
        s = jnp.einsum('bshd,bmhd->bhsm', q, k).astype(jnp.float32) * (XATTN_DH ** -0.5)
        p = jax.nn.softmax(s, axis=-1).astype(v.dtype)
        o = jnp.einsum('bhsm,bmhd->bshd', p, v).reshape(B, S, D)
        x = x + o @ xo[l]

        h = rms_norm(x, norm_ffn[l])
        x = x + (jax.nn.silu(h @ w_gate[l]) * (h @ w_up[l])) @ w_down[l]

    return x
```

```python
import functools
import math

import jax
import jax.numpy as jnp
from jax import lax
from jax.experimental import pallas as pl
from jax.experimental.pallas import tpu as pltpu

F32 = jnp.float32
BF16 = jnp.bfloat16

EPS = 1e-6
LANES = 128
VMEM_LIMIT = 56 * 1024 * 1024

RET_HEADS = 8
RET_DK = 64
RET_WIDTH = 512
RET_THETA_BASE = 10000.0
DIFF_HEADS = 4
DIFF_DQK = 64
DIFF_DV = 128
DIFF_WIDTH = 512
XATTN_HEADS = 4
SEG = 512
N_SEG = 7


def _rms(x, axis=-1):
    return lax.rsqrt(jnp.mean(x * x, axis=axis, keepdims=True) + EPS)


def _low_half_mask(shape):
    lane = lax.broadcasted_iota(jnp.int32, shape, len(shape) - 1)
    return (lane % LANES) < (LANES // 2)


def _in_proj_kernel(x_ref, gain_ref, w_ref, cos_ref, sin_ref, qg_ref, kg_ref,
                    rq_ref, rk_ref, rv_ref, rg_ref, dq_ref, dk_ref, dv_ref):
    x = x_ref[...]
    h = (x * _rms(x) * gain_ref[...]).astype(BF16)

    def proj(seg):
        return jnp.dot(h, w_ref[:, seg * SEG:(seg + 1) * SEG], preferred_element_type=F32)

    cos = cos_ref[...]
    sin = sin_ref[...]
    low = _low_half_mask(cos.shape)
    first32 = (lax.broadcasted_iota(jnp.int32, cos.shape, 1) % 64) < 32

    def rotary_store(y, out_ref, scale):
        for j in range(SEG // LANES):
            ys = y[:, j * LANES:(j + 1) * LANES]
            partner = jnp.where(first32, pltpu.roll(ys, LANES - 32, 1), pltpu.roll(ys, 32, 1))
            r = ys * cos + partner * sin
            if scale != 1.0:
                r = r * scale
            out_ref[:, j * LANES:(j + 1) * LANES] = r.astype(out_ref.dtype)

    def qknorm_store(y, g_ref, out_ref, scale):
        g = g_ref[...]
        for j in range(SEG // LANES):
            ys = y[:, j * LANES:(j + 1) * LANES]
            sq = ys * ys
            s_lo = jnp.sum(jnp.where(low, sq, 0.0), axis=-1, keepdims=True)
            s_hi = jnp.sum(jnp.where(low, 0.0, sq), axis=-1, keepdims=True)
            inv = jnp.where(low, lax.rsqrt(s_lo * (1.0 / DIFF_DQK) + EPS),
                            lax.rsqrt(s_hi * (1.0 / DIFF_DQK) + EPS))
            r = ys * inv * g
            if scale != 1.0:
                r = r * scale
            out_ref[:, j * LANES:(j + 1) * LANES] = r.astype(out_ref.dtype)

    rotary_store(proj(0), rq_ref, 1.0)
    rotary_store(proj(1), rk_ref, RET_DK ** -0.5)
    rv_ref[...] = proj(2).astype(BF16)
    g = proj(3)
    rg_ref[...] = (g * jax.nn.sigmoid(g)).astype(BF16)
    qknorm_store(proj(4), qg_ref, dq_ref, DIFF_DQK ** -0.5)
    qknorm_store(proj(5), kg_ref, dk_ref, 1.0)
    dv_ref[...] = proj(6).astype(BF16)


def _in_proj(x2, gain, w_in, cos_t, sin_t, qg, kg, seq, tm):
    T, D = x2.shape
    n_pos_blocks = seq // tm
    tok = lambda i: (i, 0)
    const = lambda i: (0, 0)
    pos = lambda i: (i % n_pos_blocks, 0)
    out = jax.ShapeDtypeStruct((T, SEG), BF16)
    return pl.pallas_call(
        _in_proj_kernel,
        grid=(T // tm,),
        in_specs=[
            pl.BlockSpec((tm, D), tok),
            pl.BlockSpec((1, D), const),
            pl.BlockSpec((D, N_SEG * SEG), const),
            pl.BlockSpec((tm, LANES), pos),
            pl.BlockSpec((tm, LANES), pos),
            pl.BlockSpec((1, LANES), const),
            pl.BlockSpec((1, LANES), const),
        ],
        out_specs=[pl.BlockSpec((tm, SEG), tok)] * N_SEG,
        out_shape=[out] * N_SEG,
        compiler_params=pltpu.CompilerParams(
            dimension_semantics=("parallel",), vmem_limit_bytes=VMEM_LIMIT),
        name="in_proj",
    )(x2, gain, w_in, cos_t, sin_t, qg, kg)


def _retention_kernel(q_ref, k_ref, v_ref, g_ref, dmask_ref, zeta_ref, xi_ref, gchunk_ref,
                      gs_ref, o_ref, state_ref, *, chunk, n_chunks):
    @pl.when(pl.program_id(1) == 0)
    def _():
        state_ref[...] = jnp.zeros_like(state_ref)

    n_slabs = RET_WIDTH // LANES
    low = _low_half_mask((chunk, LANES))
    row = lax.broadcasted_iota(jnp.int32, (LANES, LANES), 0)
    col = lax.broadcasted_iota(jnp.int32, (LANES, LANES), 1)
    blockdiag = (row < 64) == (col < 64)
    nt = (((1,), (1,)), ((), ()))
    tn = (((0,), (0,)), ((), ()))

    for s in range(n_slabs):
        sl = slice(s * LANES, (s + 1) * LANES)
        zeta = zeta_ref[:, sl]
        xi = xi_ref[:, sl]
        gch = gchunk_ref[:, sl]
        for c in range(n_chunks):
            rows = slice(c * chunk, (c + 1) * chunk)
            q = q_ref[rows, sl]
            k = k_ref[rows, sl]
            v = v_ref[rows, sl]
            zero = jnp.zeros_like(q)
            q_lo = jnp.where(low, q, zero)
            q_hi = jnp.where(low, zero, q)
            s_lo = lax.dot_general(q_lo, k, nt, preferred_element_type=F32) * dmask_ref[2 * s]
            s_hi = lax.dot_general(q_hi, k, nt, preferred_element_type=F32) * dmask_ref[2 * s + 1]
            o_lo = jnp.dot(s_lo.astype(BF16), v, preferred_element_type=F32)
            o_hi = jnp.dot(s_hi.astype(BF16), v, preferred_element_type=F32)
            r_prev = state_ref[s]
            o_inter = jnp.dot(q, r_prev.astype(BF16), preferred_element_type=F32) * xi
            o = jnp.where(low, o_lo, o_hi) + o_inter
            kz = (k.astype(F32) * zeta).astype(BF16)
            kv = lax.dot_general(kz, v, tn, preferred_element_type=F32)
            state_ref[s] = r_prev * gch + jnp.where(blockdiag, kv, 0.0)
            sq = o * o
            m_lo = jnp.sum(jnp.where(low, sq, 0.0), axis=-1, keepdims=True)
            m_hi = jnp.sum(jnp.where(low, 0.0, sq), axis=-1, keepdims=True)
            inv = jnp.where(low, lax.rsqrt(m_lo * (1.0 / 64) + EPS),
                            lax.rsqrt(m_hi * (1.0 / 64) + EPS))
            out = o * inv * g_ref[rows, sl].astype(F32) * gs_ref[:, sl]
            o_ref[rows, sl] = out.astype(o_ref.dtype)


def _retention(rq, rk, rv, rg, dmask, zeta, xi, gchunk, gs_ret, batch, seq, ts, chunk):
    T = rq.shape[0]
    nblk = seq // ts
    tok = lambda b, i: (b * nblk + i, 0)
    c2 = lambda b, i: (0, 0)
    c3 = lambda b, i: (0, 0, 0)
    kern = functools.partial(_retention_kernel, chunk=chunk, n_chunks=ts // chunk)
    return pl.pallas_call(
        kern,
        grid=(batch, nblk),
        in_specs=[pl.BlockSpec((ts, RET_WIDTH), tok)] * 4 + [
            pl.BlockSpec((RET_HEADS, chunk, chunk), c3),
            pl.BlockSpec((chunk, RET_WIDTH), c2),
            pl.BlockSpec((chunk, RET_WIDTH), c2),
            pl.BlockSpec((1, RET_WIDTH), c2),
            pl.BlockSpec((1, RET_WIDTH), c2),
        ],
        out_specs=pl.BlockSpec((ts, RET_WIDTH), tok),
        out_shape=jax.ShapeDtypeStruct((T, RET_WIDTH), BF16),
        scratch_shapes=[pltpu.VMEM((RET_WIDTH // LANES, LANES, LANES), F32)],
        compiler_params=pltpu.CompilerParams(
            dimension_semantics=("parallel", "arbitrary"), vmem_limit_bytes=VMEM_LIMIT),
        name="retention",
    )(rq, rk, rv, rg, dmask, zeta, xi, gchunk, gs_ret)


def _diff_attn_kernel(q_ref, k_ref, v_ref, lam_ref, subln_ref, gs_ref, o_ref,
                      m_ref, l_ref, acc_ref, *, tq, lam_init):
    i = pl.program_id(2)
    q = q_ref[...]
    low = _low_half_mask(q.shape)
    zero = jnp.zeros_like(q)
    qs = jnp.concatenate([jnp.where(low, q, zero), jnp.where(low, zero, q)], axis=0)
    nt = (((1,), (1,)), ((), ()))

    m_ref[...] = jnp.full_like(m_ref, -jnp.inf)
    l_ref[...] = jnp.zeros_like(l_ref)
    acc_ref[...] = jnp.zeros_like(acc_ref)

    def step(j, masked):
        start = pl.multiple_of(j * tq, tq)
        k = k_ref[pl.ds(start, tq), :]
        v = v_ref[pl.ds(start, tq), :]
        s = lax.dot_general(qs, k, nt, preferred_element_type=F32)
        if masked:
            r = lax.broadcasted_iota(jnp.int32, s.shape, 0) % tq
            c = lax.broadcasted_iota(jnp.int32, s.shape, 1)
            s = jnp.where(c <= r, s, -jnp.inf)
        m_prev = m_ref[...]
        m_new = jnp.maximum(m_prev, jnp.max(s, axis=-1, keepdims=True))
        alpha = jnp.exp(m_prev - m_new)
        p = jnp.exp(s - m_new)
        l_ref[...] = alpha * l_ref[...] + jnp.sum(p, axis=-1, keepdims=True)
        acc_ref[...] = alpha * acc_ref[...] + jnp.dot(p.astype(BF16), v,
                                                      preferred_element_type=F32)
        m_ref[...] = m_new

    def body(j, carry):
        step(j, False)
        return carry

    lax.fori_loop(0, i, body, 0)
    step(i, True)

    lv = lam_ref[...]
    lam = (jnp.exp(jnp.sum(lv[0:1] * lv[1:2], axis=-1, keepdims=True))
           - jnp.exp(jnp.sum(lv[2:3] * lv[3:4], axis=-1, keepdims=True)) + lam_init)
    o_all = acc_ref[...] * (1.0 / l_ref[...])
    o = o_all[:tq] - lam * o_all[tq:]
    y = o * _rms(o) * subln_ref[...] * (1.0 - lam_init)
    o_ref[...] = (y * gs_ref[...]).astype(o_ref.dtype)


def _diff_attn(dq, dk, dv, lam_p, subln, gs_diff, batch, seq, tq, lam_init):
    T = dq.shape[0]
    nq = seq // tq
    kern = functools.partial(_diff_attn_kernel, tq=tq, lam_init=lam_init)
    return pl.pallas_call(
        kern,
        grid=(batch, DIFF_HEADS, nq),
        in_specs=[
            pl.BlockSpec((tq, LANES), lambda b, h, i: (b * nq + i, h)),
            pl.BlockSpec((seq, LANES), lambda b, h, i: (b, h)),
            pl.BlockSpec((seq, LANES), lambda b, h, i: (b, h)),
            pl.BlockSpec((4, DIFF_DQK), lambda b, h, i: (0, 0)),
            pl.BlockSpec((1, LANES), lambda b, h, i: (0, 0)),
            pl.BlockSpec((1, LANES), lambda b, h, i: (0, h)),
        ],
        out_specs=pl.BlockSpec((tq, LANES), lambda b, h, i: (b * nq + i, h)),
        out_shape=jax.ShapeDtypeStruct((T, DIFF_WIDTH), BF16),
        scratch_shapes=[pltpu.VMEM((2 * tq, 1), F32), pltpu.VMEM((2 * tq, 1), F32),
                        pltpu.VMEM((2 * tq, LANES), F32)],
        compiler_params=pltpu.CompilerParams(
            dimension_semantics=("parallel", "parallel", "arbitrary"),
            vmem_limit_bytes=VMEM_LIMIT),
        name="diff_attn",
    )(dq, dk, dv, lam_p, subln, gs_diff)


def _mem_kv_kernel(mem_ref, gain_ref, w_ref, kg_ref, k_ref, v_ref):
    m = mem_ref[...]
    h = (m * _rms(m) * gain_ref[...]).astype(BF16)
    D = m.shape[1]
    dh = D // XATTN_HEADS
    k = jnp.dot(h, w_ref[:, :D], preferred_element_type=F32)
    v_ref[...] = jnp.dot(h, w_ref[:, D:], preferred_element_type=F32).astype(v_ref.dtype)
    for hh in range(XATTN_HEADS):
        kh = k[:, hh * dh:(hh + 1) * dh]
        k_ref[:, hh * dh:(hh + 1) * dh] = (kh * _rms(kh) * kg_ref[...]).astype(k_ref.dtype)


def _mem_kv(mem2, gain, xkv, kg, batch, mem_len):
    BM, D = mem2.shape
    const = lambda b: (0, 0)
    out = jax.ShapeDtypeStruct((BM, D), BF16)
    return pl.pallas_call(
        _mem_kv_kernel,
        grid=(batch,),
        in_specs=[
            pl.BlockSpec((mem_len, D), lambda b: (b, 0)),
            pl.BlockSpec((1, D), const),
            pl.BlockSpec((D, 2 * D), const),
            pl.BlockSpec((1, D // XATTN_HEADS), const),
        ],
        out_specs=[pl.BlockSpec((mem_len, D), lambda b: (b, 0))] * 2,
        out_shape=[out, out],
        compiler_params=pltpu.CompilerParams(
            dimension_semantics=("parallel",), vmem_limit_bytes=VMEM_LIMIT),
        name="mem_kv",
    )(mem2, gain, xkv, kg)


def _mix_xattn_kernel(x_ref, oret_ref, odiff_ref, wout_ref, nx_ref, xq_ref, qg_ref,
                      kmem_ref, vmem_ref, xo_ref, out_ref):
    D = x_ref.shape[1]
    dh = D // XATTN_HEADS
    x1 = (x_ref[...]
          + jnp.dot(oret_ref[...], wout_ref[:RET_WIDTH, :], preferred_element_type=F32)
          + jnp.dot(odiff_ref[...], wout_ref[RET_WIDTH:, :], preferred_element_type=F32))
    h = (x1 * _rms(x1) * nx_ref[...]).astype(BF16)
    q = jnp.dot(h, xq_ref[...], preferred_element_type=F32)
    nt = (((1,), (1,)), ((), ()))
    outs = []
    for hh in range(XATTN_HEADS):
        sl = slice(hh * dh, (hh + 1) * dh)
        qh = q[:, sl]
        qn = (qh * _rms(qh) * qg_ref[...] * (dh ** -0.5)).astype(BF16)
        s = lax.dot_general(qn, kmem_ref[:, sl], nt, preferred_element_type=F32)
        e = jnp.exp(s - jnp.max(s, axis=-1, keepdims=True))
        p = e * (1.0 / jnp.sum(e, axis=-1, keepdims=True))
        outs.append(jnp.dot(p.astype(BF16), vmem_ref[:, sl],
                            preferred_element_type=F32).astype(BF16))
    o = jnp.concatenate(outs, axis=1)
    out_ref[...] = x1 + jnp.dot(o, xo_ref[...], preferred_element_type=F32)


def _mix_xattn(x2, o_ret, o_diff, w_out, nx, xq, qg, kmem, vmem, xo, seq, mem_len, tm):
    T, D = x2.shape
    nblk = seq // tm
    tok = lambda i: (i, 0)
    const = lambda i: (0, 0)
    memb = lambda i: (i // nblk, 0)
    return pl.pallas_call(
        _mix_xattn_kernel,
        grid=(T // tm,),
        in_specs=[
            pl.BlockSpec((tm, D), tok),
            pl.BlockSpec((tm, RET_WIDTH), tok),
            pl.BlockSpec((tm, DIFF_WIDTH), tok),
            pl.BlockSpec((D, D), const),
            pl.BlockSpec((1, D), const),
            pl.BlockSpec((D, D), const),
            pl.BlockSpec((1, D // XATTN_HEADS), const),
            pl.BlockSpec((mem_len, D), memb),
            pl.BlockSpec((mem_len, D), memb),
            pl.BlockSpec((D, D), const),
        ],
        out_specs=pl.BlockSpec((tm, D), tok),
        out_shape=jax.ShapeDtypeStruct((T, D), F32),
        compiler_params=pltpu.CompilerParams(
            dimension_semantics=("parallel",), vmem_limit_bytes=VMEM_LIMIT),
        name="mix_xattn",
    )(x2, o_ret, o_diff, w_out, nx, xq, qg, kmem, vmem, xo)


def _ffn_kernel(x_ref, gain_ref, wg_ref, wu_ref, wd_ref, out_ref, *, hidden_chunk):
    x = x_ref[...]
    h = (x * _rms(x) * gain_ref[...]).astype(BF16)
    hidden = wg_ref.shape[1]
    acc = x
    for c in range(hidden // hidden_chunk):
        sl = slice(c * hidden_chunk, (c + 1) * hidden_chunk)
        g = jnp.dot(h, wg_ref[:, sl], preferred_element_type=F32)
        u = jnp.dot(h, wu_ref[:, sl], preferred_element_type=F32)
        a = (g * jax.nn.sigmoid(g) * u).astype(BF16)
        acc = acc + jnp.dot(a, wd_ref[sl, :], preferred_element_type=F32)
    out_ref[...] = acc


def _ffn(x2, gain, wg, wu, wd, tm, hidden_chunk):
    T, D = x2.shape
    hidden = wg.shape[1]
    tok = lambda i: (i, 0)
    const = lambda i: (0, 0)
    kern = functools.partial(_ffn_kernel, hidden_chunk=hidden_chunk)
    return pl.pallas_call(
        kern,
        grid=(T // tm,),
        in_specs=[
            pl.BlockSpec((tm, D), tok),
            pl.BlockSpec((1, D), const),
            pl.BlockSpec((D, hidden), const),
            pl.BlockSpec((D, hidden), const),
            pl.BlockSpec((hidden, D), const),
        ],
        out_specs=pl.BlockSpec((tm, D), tok),
        out_shape=jax.ShapeDtypeStruct((T, D), F32),
        compiler_params=pltpu.CompilerParams(
            dimension_semantics=("parallel",), vmem_limit_bytes=VMEM_LIMIT),
        name="ffn",
    )(x2, gain, wg, wu, wd)


def _rotary_tables(seq):
    half = RET_DK // 2
    inv = 1.0 / (RET_THETA_BASE ** jnp.linspace(0.0, 1.0, half, dtype=F32))
    ang = jnp.arange(seq).astype(F32)[:, None] * inv[None, :]
    cos, sin = jnp.cos(ang), jnp.sin(ang)
    cos_t = jnp.tile(cos, (1, LANES // half))
    sin_t = jnp.tile(jnp.concatenate([-sin, sin], axis=1), (1, LANES // RET_DK))
    return cos_t, sin_t


def _decay_tables(chunk):
    H = RET_HEADS
    log_g = jnp.log1p(-(2.0 ** (-5.0 - jnp.arange(H, dtype=F32))))
    idx = jnp.arange(chunk, dtype=F32)
    rel = idx[:, None] - idx[None, :]
    dmask = jnp.where(rel >= 0, jnp.exp(log_g[:, None, None] * jnp.maximum(rel, 0.0)), 0.0)
    zeta = jnp.exp(log_g[:, None] * (chunk - 1.0 - idx))
    xi = jnp.exp(log_g[:, None] * (idx + 1.0))
    g_chunk = jnp.exp(log_g * chunk)
    widen = lambda t: jnp.repeat(t.T, RET_DK, axis=1)
    return dmask, widen(zeta), widen(xi), jnp.repeat(g_chunk, RET_DK)[None, :]


def kernel(x, mem, norm_mix, w_in, diff_q_gain, diff_k_gain, diff_lambda, diff_subln,
           group_scale, w_out, norm_x, norm_mem, xq, xkv, xq_gain, xk_gain, xo,
           norm_ffn, w_gate, w_up, w_down):
    B, S, D = x.shape
    M = mem.shape[1]
    depth = norm_mix.shape[0]
    x2 = x.reshape(B * S, D)
    mem2 = mem.reshape(B * M, D)

    ret_chunk = 256
    cos_t, sin_t = _rotary_tables(S)
    dmask, zeta, xi, gchunk = _decay_tables(ret_chunk)
    row = lambda a: a.reshape(1, -1)
    twice = lambda a: jnp.tile(a.reshape(1, -1), (1, 2))

    for l in range(depth):
        lam_init = 0.8 - 0.6 * math.exp(-0.3 * l)
        rq, rk, rv, rg, dq, dk, dv = _in_proj(
            x2, row(norm_mix[l]), w_in[l].astype(BF16), cos_t, sin_t,
            twice(diff_q_gain[l]), twice(diff_k_gain[l]), S, tm=512)
        o_ret = _retention(rq, rk, rv, rg, dmask, zeta, xi, gchunk,
                           row(group_scale[l, :RET_WIDTH]), B, S, ts=512, chunk=ret_chunk)
        o_diff = _diff_attn(dq, dk, dv, diff_lambda[l], row(diff_subln[l]),
                            row(group_scale[l, RET_WIDTH:]), B, S, tq=512, lam_init=lam_init)
        kmem, vmem = _mem_kv(mem2, row(norm_mem[l]), xkv[l].astype(BF16), row(xk_gain[l]), B, M)
        x2 = _mix_xattn(x2, o_ret, o_diff, w_out[l].astype(BF16), row(norm_x[l]),
                        xq[l].astype(BF16), row(xq_gain[l]), kmem, vmem, xo[l].astype(BF16),
                        S, M, tm=512)
        x2 = _ffn(x2, row(norm_ffn[l]), w_gate[l].astype(BF16), w_up[l].astype(BF16),
                  w_down[l].astype(BF16), tm=256, hidden_chunk=256)
    return x2.reshape(B, S, D)
```

```python
import functools
import math

import jax
import jax.numpy as jnp
from jax import lax
from jax.experimental import pallas as pl
from jax.experimental.pallas import tpu as pltpu

F32 = jnp.float32
BF16 = jnp.bfloat16

EPS = 1e-6
LANES = 128
VMEM_LIMIT = 56 * 1024 * 1024

RET_HEADS = 8
RET_DK = 64
RET_WIDTH = 512
RET_THETA_BASE = 10000.0
DIFF_HEADS = 4
DIFF_DQK = 64
DIFF_DV = 128
DIFF_WIDTH = 512
XATTN_HEADS = 4
SEG = 512
N_SEG = 7
LOG2E = math.log2(math.e)


def _rms(x, axis=-1):
    return lax.rsqrt(jnp.mean(x * x, axis=axis, keepdims=True) + EPS)


def _low_half_mask(shape):
    lane = lax.broadcasted_iota(jnp.int32, shape, len(shape) - 1)
    return (lane % LANES) < (LANES // 2)


def _in_proj_kernel(x_ref, gain_ref, w_ref, cos_ref, sin_ref, qg_ref, kg_ref,
                    rq_ref, rk_ref, rv_ref, rg_ref, dq_ref, dk_ref, dv_ref):
    x = x_ref[...]
    h = (x * _rms(x) * gain_ref[...]).astype(BF16)

    def proj(seg):
        return jnp.dot(h, w_ref[:, seg * SEG:(seg + 1) * SEG], preferred_element_type=F32)

    cos = cos_ref[...]
    sin = sin_ref[...]
    low = _low_half_mask(cos.shape)
    first32 = (lax.broadcasted_iota(jnp.int32, cos.shape, 1) % 64) < 32

    def rotary_store(y, out_ref, scale):
        for j in range(SEG // LANES):
            ys = y[:, j * LANES:(j + 1) * LANES]
            partner = jnp.where(first32, pltpu.roll(ys, LANES - 32, 1), pltpu.roll(ys, 32, 1))
            r = ys * cos + partner * sin
            if scale != 1.0:
                r = r * scale
            out_ref[:, j * LANES:(j + 1) * LANES] = r.astype(out_ref.dtype)

    def qknorm_store(y, g_ref, out_ref, scale):
        g = g_ref[...]
        for j in range(SEG // LANES):
            ys = y[:, j * LANES:(j + 1) * LANES]
            sq = ys * ys
            s_lo = jnp.sum(jnp.where(low, sq, 0.0), axis=-1, keepdims=True)
            s_hi = jnp.sum(jnp.where(low, 0.0, sq), axis=-1, keepdims=True)
            inv = jnp.where(low, lax.rsqrt(s_lo * (1.0 / DIFF_DQK) + EPS),
                            lax.rsqrt(s_hi * (1.0 / DIFF_DQK) + EPS))
            r = ys * inv * g
            if scale != 1.0:
                r = r * scale
            out_ref[:, j * LANES:(j + 1) * LANES] = r.astype(out_ref.dtype)

    rotary_store(proj(0), rq_ref, 1.0)
    rotary_store(proj(1), rk_ref, RET_DK ** -0.5)
    rv_ref[...] = proj(2).astype(BF16)
    g = proj(3)
    rg_ref[...] = (g * jax.nn.sigmoid(g)).astype(BF16)
    qknorm_store(proj(4), qg_ref, dq_ref, DIFF_DQK ** -0.5 * LOG2E)
    qknorm_store(proj(5), kg_ref, dk_ref, 1.0)
    dv = proj(6).astype(BF16)
    ones = jnp.ones((dv.shape[0], DIFF_DV), BF16)
    for hh in range(DIFF_HEADS):
        dv_ref[:, 2 * hh * DIFF_DV:(2 * hh + 1) * DIFF_DV] = dv[:, hh * DIFF_DV:(hh + 1) * DIFF_DV]
        dv_ref[:, (2 * hh + 1) * DIFF_DV:(2 * hh + 2) * DIFF_DV] = ones


def _in_proj(x2, gain, w_in, cos_t, sin_t, qg, kg, seq, tm):
    T, D = x2.shape
    n_pos_blocks = seq // tm
    tok = lambda i: (i, 0)
    const = lambda i: (0, 0)
    pos = lambda i: (i % n_pos_blocks, 0)
    out = jax.ShapeDtypeStruct((T, SEG), BF16)
    return pl.pallas_call(
        _in_proj_kernel,
        grid=(T // tm,),
        in_specs=[
            pl.BlockSpec((tm, D), tok),
            pl.BlockSpec((1, D), const),
            pl.BlockSpec((D, N_SEG * SEG), const),
            pl.BlockSpec((tm, LANES), pos),
            pl.BlockSpec((tm, LANES), pos),
            pl.BlockSpec((1, LANES), const),
            pl.BlockSpec((1, LANES), const),
        ],
        out_specs=[pl.BlockSpec((tm, SEG), tok)] * (N_SEG - 1)
        + [pl.BlockSpec((tm, 2 * SEG), tok)],
        out_shape=[out] * (N_SEG - 1) + [jax.ShapeDtypeStruct((T, 2 * SEG), BF16)],
        compiler_params=pltpu.CompilerParams(
            dimension_semantics=("parallel",), vmem_limit_bytes=VMEM_LIMIT),
        name="in_proj",
    )(x2, gain, w_in, cos_t, sin_t, qg, kg)


def _retention_kernel(q_ref, k_ref, v_ref, g_ref, dmask_ref, zeta_ref, xi_ref, gchunk_ref,
                      gs_ref, o_ref, state_ref, *, chunk, n_chunks):
    @pl.when(pl.program_id(1) == 0)
    def _():
        state_ref[...] = jnp.zeros_like(state_ref)

    n_slabs = RET_WIDTH // LANES
    low = _low_half_mask((chunk, LANES))
    row = lax.broadcasted_iota(jnp.int32, (LANES, LANES), 0)
    col = lax.broadcasted_iota(jnp.int32, (LANES, LANES), 1)
    blockdiag = (row < 64) == (col < 64)
    nt = (((1,), (1,)), ((), ()))
    tn = (((0,), (0,)), ((), ()))

    for s in range(n_slabs):
        sl = slice(s * LANES, (s + 1) * LANES)
        zeta = zeta_ref[:, sl]
        xi = xi_ref[:, sl]
        gch = gchunk_ref[:, sl]
        for c in range(n_chunks):
            rows = slice(c * chunk, (c + 1) * chunk)
            q = q_ref[rows, sl]
            k = k_ref[rows, sl]
            v = v_ref[rows, sl]
            zero = jnp.zeros_like(q)
            q_lo = jnp.where(low, q, zero)
            q_hi = jnp.where(low, zero, q)
            s_lo = lax.dot_general(q_lo, k, nt, preferred_element_type=F32) * dmask_ref[2 * s]
            s_hi = lax.dot_general(q_hi, k, nt, preferred_element_type=F32) * dmask_ref[2 * s + 1]
            o_lo = jnp.dot(s_lo.astype(BF16), v, preferred_element_type=F32)
            o_hi = jnp.dot(s_hi.astype(BF16), v, preferred_element_type=F32)
            r_prev = state_ref[s]
            o_inter = jnp.dot(q, r_prev.astype(BF16), preferred_element_type=F32) * xi
            o = jnp.where(low, o_lo, o_hi) + o_inter
            kz = (k.astype(F32) * zeta).astype(BF16)
            kv = lax.dot_general(kz, v, tn, preferred_element_type=F32)
            state_ref[s] = r_prev * gch + jnp.where(blockdiag, kv, 0.0)
            sq = o * o
            m_lo = jnp.sum(jnp.where(low, sq, 0.0), axis=-1, keepdims=True)
            m_hi = jnp.sum(jnp.where(low, 0.0, sq), axis=-1, keepdims=True)
            inv = jnp.where(low, lax.rsqrt(m_lo * (1.0 / 64) + EPS),
                            lax.rsqrt(m_hi * (1.0 / 64) + EPS))
            out = o * inv * g_ref[rows, sl].astype(F32) * gs_ref[:, sl]
            o_ref[rows, sl] = out.astype(o_ref.dtype)


def _retention(rq, rk, rv, rg, dmask, zeta, xi, gchunk, gs_ret, batch, seq, ts, chunk):
    T = rq.shape[0]
    nblk = seq // ts
    tok = lambda b, i: (b * nblk + i, 0)
    c2 = lambda b, i: (0, 0)
    c3 = lambda b, i: (0, 0, 0)
    kern = functools.partial(_retention_kernel, chunk=chunk, n_chunks=ts // chunk)
    return pl.pallas_call(
        kern,
        grid=(batch, nblk),
        in_specs=[pl.BlockSpec((ts, RET_WIDTH), tok)] * 4 + [
            pl.BlockSpec((RET_HEADS, chunk, chunk), c3),
            pl.BlockSpec((chunk, RET_WIDTH), c2),
            pl.BlockSpec((chunk, RET_WIDTH), c2),
            pl.BlockSpec((1, RET_WIDTH), c2),
            pl.BlockSpec((1, RET_WIDTH), c2),
        ],
        out_specs=pl.BlockSpec((ts, RET_WIDTH), tok),
        out_shape=jax.ShapeDtypeStruct((T, RET_WIDTH), BF16),
        scratch_shapes=[pltpu.VMEM((RET_WIDTH // LANES, LANES, LANES), F32)],
        compiler_params=pltpu.CompilerParams(
            dimension_semantics=("parallel", "arbitrary"), vmem_limit_bytes=VMEM_LIMIT),
        name="retention",
    )(rq, rk, rv, rg, dmask, zeta, xi, gchunk, gs_ret)


def _diff_attn_kernel(q_ref, k_ref, v_ref, lam_ref, subln_ref, gs_ref, o_ref,
                      qs_ref, m_ref, acc_ref, *, tq, tk, rb, lam_init):
    i = pl.program_id(2)
    q = q_ref[...]
    low = _low_half_mask(q.shape)
    zero = jnp.zeros_like(q)
    qs_ref[:tq, :] = jnp.where(low, q, zero)
    qs_ref[tq:, :] = jnp.where(low, zero, q)
    nt = (((1,), (1,)), ((), ()))

    m_ref[...] = jnp.full_like(m_ref, -jnp.inf)
    acc_ref[...] = jnp.zeros_like(acc_ref)

    def step(j, key_offset):
        start = pl.multiple_of(j * tk, tk)
        k = k_ref[pl.ds(start, tk), :]
        v = v_ref[pl.ds(start, tk), :]
        for b in range(2 * tq // rb):
            rows = slice(b * rb, (b + 1) * rb)
            s = lax.dot_general(qs_ref[rows, :], k, nt, preferred_element_type=F32)
            if key_offset is not None:
                r = lax.broadcasted_iota(jnp.int32, s.shape, 0) + (b * rb) % tq
                c = lax.broadcasted_iota(jnp.int32, s.shape, 1) + key_offset
                s = jnp.where(c <= r, s, -jnp.inf)
            m_prev = m_ref[rows, :]
            m_new = jnp.maximum(m_prev, jnp.max(s, axis=-1, keepdims=True))
            alpha = jnp.exp2(m_prev - m_new)
            p = jnp.exp2(s - jnp.concatenate([m_new] * (tk // LANES), axis=1))
            pv = jnp.dot(p.astype(BF16), v, preferred_element_type=F32)
            acc_ref[rows, :] = jnp.concatenate([alpha, alpha], axis=1) * acc_ref[rows, :] + pv
            m_ref[rows, :] = m_new

    def body(j, carry):
        step(j, None)
        return carry

    lax.fori_loop(0, i * (tq // tk), body, 0)
    for d in range(tq // tk):
        step(i * (tq // tk) + d, d * tk)

    lv = lam_ref[...]
    lam = (jnp.exp(jnp.sum(lv[0:1] * lv[1:2], axis=-1, keepdims=True))
           - jnp.exp(jnp.sum(lv[2:3] * lv[3:4], axis=-1, keepdims=True)) + lam_init)
    acc = acc_ref[...]
    o_all = acc[:, :DIFF_DV] * (1.0 / acc[:, DIFF_DV:])
    o = o_all[:tq] - lam * o_all[tq:]
    y = o * _rms(o) * subln_ref[...] * (1.0 - lam_init)
    o_ref[...] = (y * gs_ref[...]).astype(o_ref.dtype)


def _diff_attn(dq, dk, dv_aug, lam_p, subln, gs_diff, batch, seq, tq, tk, rb, lam_init):
    T = dq.shape[0]
    nq = seq // tq
    kern = functools.partial(_diff_attn_kernel, tq=tq, tk=tk, rb=rb, lam_init=lam_init)
    return pl.pallas_call(
        kern,
        grid=(batch, DIFF_HEADS, nq),
        in_specs=[
            pl.BlockSpec((tq, LANES), lambda b, h, i: (b * nq + i, h)),
            pl.BlockSpec((seq, LANES), lambda b, h, i: (b, h)),
            pl.BlockSpec((seq, 2 * DIFF_DV), lambda b, h, i: (b, h)),
            pl.BlockSpec((4, DIFF_DQK), lambda b, h, i: (0, 0)),
            pl.BlockSpec((1, LANES), lambda b, h, i: (0, 0)),
            pl.BlockSpec((1, LANES), lambda b, h, i: (0, h)),
        ],
        out_specs=pl.BlockSpec((tq, LANES), lambda b, h, i: (b * nq + i, h)),
        out_shape=jax.ShapeDtypeStruct((T, DIFF_WIDTH), BF16),
        scratch_shapes=[pltpu.VMEM((2 * tq, LANES), BF16), pltpu.VMEM((2 * tq, LANES), F32),
                        pltpu.VMEM((2 * tq, 2 * DIFF_DV), F32)],
        compiler_params=pltpu.CompilerParams(
            dimension_semantics=("parallel", "parallel", "arbitrary"),
            vmem_limit_bytes=VMEM_LIMIT),
        name="diff_attn",
    )(dq, dk, dv_aug, lam_p, subln, gs_diff)


def _mem_kv_kernel(mem_ref, gain_ref, w_ref, kg_ref, k_ref, v_ref):
    m = mem_ref[...]
    h = (m * _rms(m) * gain_ref[...]).astype(BF16)
    D = m.shape[1]
    dh = D // XATTN_HEADS
    k = jnp.dot(h, w_ref[:, :D], preferred_element_type=F32)
    v_ref[...] = jnp.dot(h, w_ref[:, D:], preferred_element_type=F32).astype(v_ref.dtype)
    for hh in range(XATTN_HEADS):
        kh = k[:, hh * dh:(hh + 1) * dh]
        k_ref[:, hh * dh:(hh + 1) * dh] = (kh * _rms(kh) * kg_ref[...]).astype(k_ref.dtype)


def _mem_kv(mem2, gain, xkv, kg, batch, mem_len):
    BM, D = mem2.shape
    const = lambda b: (0, 0)
    out = jax.ShapeDtypeStruct((BM, D), BF16)
    return pl.pallas_call(
        _mem_kv_kernel,
        grid=(batch,),
        in_specs=[
            pl.BlockSpec((mem_len, D), lambda b: (b, 0)),
            pl.BlockSpec((1, D), const),
            pl.BlockSpec((D, 2 * D), const),
            pl.BlockSpec((1, D // XATTN_HEADS), const),
        ],
        out_specs=[pl.BlockSpec((mem_len, D), lambda b: (b, 0))] * 2,
        out_shape=[out, out],
        compiler_params=pltpu.CompilerParams(
            dimension_semantics=("parallel",), vmem_limit_bytes=VMEM_LIMIT),
        name="mem_kv",
    )(mem2, gain, xkv, kg)


def _mix_xattn_kernel(x_ref, oret_ref, odiff_ref, wout_ref, nx_ref, xq_ref, qg_ref,
                      kmem_ref, vmem_ref, xo_ref, out_ref):
    D = x_ref.shape[1]
    dh = D // XATTN_HEADS
    x1 = (x_ref[...]
          + jnp.dot(oret_ref[...], wout_ref[:RET_WIDTH, :], preferred_element_type=F32)
          + jnp.dot(odiff_ref[...], wout_ref[RET_WIDTH:, :], preferred_element_type=F32))
    h = (x1 * _rms(x1) * nx_ref[...]).astype(BF16)
    q = jnp.dot(h, xq_ref[...], preferred_element_type=F32)
    nt = (((1,), (1,)), ((), ()))
    outs = []
    for hh in range(XATTN_HEADS):
        sl = slice(hh * dh, (hh + 1) * dh)
        qh = q[:, sl]
        qn = (qh * _rms(qh) * qg_ref[...] * (dh ** -0.5)).astype(BF16)
        s = lax.dot_general(qn, kmem_ref[:, sl], nt, preferred_element_type=F32)
        e = jnp.exp(s - jnp.max(s, axis=-1, keepdims=True))
        p = e * (1.0 / jnp.sum(e, axis=-1, keepdims=True))
        outs.append(jnp.dot(p.astype(BF16), vmem_ref[:, sl],
                            preferred_element_type=F32).astype(BF16))
    o = jnp.concatenate(outs, axis=1)
    out_ref[...] = x1 + jnp.dot(o, xo_ref[...], preferred_element_type=F32)


def _mix_xattn(x2, o_ret, o_diff, w_out, nx, xq, qg, kmem, vmem, xo, seq, mem_len, tm):
    T, D = x2.shape
    nblk = seq // tm
    tok = lambda i: (i, 0)
    const = lambda i: (0, 0)
    memb = lambda i: (i // nblk, 0)
    return pl.pallas_call(
        _mix_xattn_kernel,
        grid=(T // tm,),
        in_specs=[
            pl.BlockSpec((tm, D), tok),
            pl.BlockSpec((tm, RET_WIDTH), tok),
            pl.BlockSpec((tm, DIFF_WIDTH), tok),
            pl.BlockSpec((D, D), const),
            pl.BlockSpec((1, D), const),
            pl.BlockSpec((D, D), const),
            pl.BlockSpec((1, D // XATTN_HEADS), const),
            pl.BlockSpec((mem_len, D), memb),
            pl.BlockSpec((mem_len, D), memb),
            pl.BlockSpec((D, D), const),
        ],
        out_specs=pl.BlockSpec((tm, D), tok),
        out_shape=jax.ShapeDtypeStruct((T, D), F32),
        compiler_params=pltpu.CompilerParams(
            dimension_semantics=("parallel",), vmem_limit_bytes=VMEM_LIMIT),
        name="mix_xattn",
    )(x2, o_ret, o_diff, w_out, nx, xq, qg, kmem, vmem, xo)


def _ffn_kernel(x_ref, gain_ref, wg_ref, wu_ref, wd_ref, out_ref, *, hidden_chunk):
    x = x_ref[...]
    h = (x * _rms(x) * gain_ref[...]).astype(BF16)
    hidden = wg_ref.shape[1]
    acc = x
    for c in range(hidden // hidden_chunk):
        sl = slice(c * hidden_chunk, (c + 1) * hidden_chunk)
        g = jnp.dot(h, wg_ref[:, sl], preferred_element_type=F32)
        u = jnp.dot(h, wu_ref[:, sl], preferred_element_type=F32)
        a = (g * jax.nn.sigmoid(g) * u).astype(BF16)
        acc = acc + jnp.dot(a, wd_ref[sl, :], preferred_element_type=F32)
    out_ref[...] = acc


def _ffn(x2, gain, wg, wu, wd, tm, hidden_chunk):
    T, D = x2.shape
    hidden = wg.shape[1]
    tok = lambda i: (i, 0)
    const = lambda i: (0, 0)
    kern = functools.partial(_ffn_kernel, hidden_chunk=hidden_chunk)
    return pl.pallas_call(
        kern,
        grid=(T // tm,),
        in_specs=[
            pl.BlockSpec((tm, D), tok),
            pl.BlockSpec((1, D), const),
            pl.BlockSpec((D, hidden), const),
            pl.BlockSpec((D, hidden), const),
            pl.BlockSpec((hidden, D), const),
        ],
        out_specs=pl.BlockSpec((tm, D), tok),
        out_shape=jax.ShapeDtypeStruct((T, D), F32),
        compiler_params=pltpu.CompilerParams(
            dimension_semantics=("parallel",), vmem_limit_bytes=VMEM_LIMIT),
        name="ffn",
    )(x2, gain, wg, wu, wd)


def _rotary_tables(seq):
    half = RET_DK // 2
    inv = 1.0 / (RET_THETA_BASE ** jnp.linspace(0.0, 1.0, half, dtype=F32))
    ang = jnp.arange(seq).astype(F32)[:, None] * inv[None, :]
    cos, sin = jnp.cos(ang), jnp.sin(ang)
    cos_t = jnp.tile(cos, (1, LANES // half))
    sin_t = jnp.tile(jnp.concatenate([-sin, sin], axis=1), (1, LANES // RET_DK))
    return cos_t, sin_t


def _decay_tables(chunk):
    H = RET_HEADS
    log_g = jnp.log1p(-(2.0 ** (-5.0 - jnp.arange(H, dtype=F32))))
    idx = jnp.arange(chunk, dtype=F32)
    rel = idx[:, None] - idx[None, :]
    dmask = jnp.where(rel >= 0, jnp.exp(log_g[:, None, None] * jnp.maximum(rel, 0.0)), 0.0)
    zeta = jnp.exp(log_g[:, None] * (chunk - 1.0 - idx))
    xi = jnp.exp(log_g[:, None] * (idx + 1.0))
    g_chunk = jnp.exp(log_g * chunk)
    widen = lambda t: jnp.repeat(t.T, RET_DK, axis=1)
    return dmask, widen(zeta), widen(xi), jnp.repeat(g_chunk, RET_DK)[None, :]


def kernel(x, mem, norm_mix, w_in, diff_q_gain, diff_k_gain, diff_lambda, diff_subln,
           group_scale, w_out, norm_x, norm_mem, xq, xkv, xq_gain, xk_gain, xo,
           norm_ffn, w_gate, w_up, w_down):
    B, S, D = x.shape
    M = mem.shape[1]
    depth = norm_mix.shape[0]
    x2 = x.reshape(B * S, D)
    mem2 = mem.reshape(B * M, D)

    ret_chunk = 256
    cos_t, sin_t = _rotary_tables(S)
    dmask, zeta, xi, gchunk = _decay_tables(ret_chunk)
    row = lambda a: a.reshape(1, -1)
    twice = lambda a: jnp.tile(a.reshape(1, -1), (1, 2))

    for l in range(depth):
        lam_init = 0.8 - 0.6 * math.exp(-0.3 * l)
        rq, rk, rv, rg, dq, dk, dv = _in_proj(
            x2, row(norm_mix[l]), w_in[l].astype(BF16), cos_t, sin_t,
            twice(diff_q_gain[l]), twice(diff_k_gain[l]), S, tm=512)
        o_ret = _retention(rq, rk, rv, rg, dmask, zeta, xi, gchunk,
                           row(group_scale[l, :RET_WIDTH]), B, S, ts=512, chunk=ret_chunk)
        o_diff = _diff_attn(dq, dk, dv, diff_lambda[l], row(diff_subln[l]),
                            row(group_scale[l, RET_WIDTH:]), B, S, tq=512, tk=512, rb=256,
                            lam_init=lam_init)
        kmem, vmem = _mem_kv(mem2, row(norm_mem[l]), xkv[l].astype(BF16), row(xk_gain[l]), B, M)
        x2 = _mix_xattn(x2, o_ret, o_diff, w_out[l].astype(BF16), row(norm_x[l]),
                        xq[l].astype(BF16), row(xq_gain[l]), kmem, vmem, xo[l].astype(BF16),
                        S, M, tm=512)
        x2 = _ffn(x2, row(norm_ffn[l]), w_gate[l].astype(BF16), w_up[l].astype(BF16),
                  w_down[l].astype(BF16), tm=256, hidden_chunk=256)
    return x2.reshape(B, S, D)
```

```python
import functools
import math

import jax
import jax.numpy as jnp
from jax import lax
from jax.experimental import pallas as pl
from jax.experimental.pallas import tpu as pltpu

F32 = jnp.float32
BF16 = jnp.bfloat16

EPS = 1e-6
LANES = 128
VMEM_LIMIT = 56 * 1024 * 1024

RET_HEADS = 8
RET_DK = 64
RET_WIDTH = 512
RET_THETA_BASE = 10000.0
DIFF_HEADS = 4
DIFF_DQK = 64
DIFF_DV = 128
DIFF_WIDTH = 512
XATTN_HEADS = 4
SEG = 512
N_SEG = 7
LOG2E = math.log2(math.e)


def _rms(x, axis=-1):
    return lax.rsqrt(jnp.mean(x * x, axis=axis, keepdims=True) + EPS)


def _low_half_mask(shape):
    lane = lax.broadcasted_iota(jnp.int32, shape, len(shape) - 1)
    return (lane % LANES) < (LANES // 2)


def _in_proj_kernel(x_ref, gain_ref, w_ref, cos_ref, sin_ref, qg_ref, kg_ref,
                    rq_ref, rk_ref, rv_ref, rg_ref, dq_ref, dk_ref, dv_ref):
    x = x_ref[...]
    h = (x * _rms(x) * gain_ref[...]).astype(BF16)

    def proj(seg):
        return jnp.dot(h, w_ref[:, seg * SEG:(seg + 1) * SEG], preferred_element_type=F32)

    cos = cos_ref[...]
    sin = sin_ref[...]
    low = _low_half_mask(cos.shape)
    first32 = (lax.broadcasted_iota(jnp.int32, cos.shape, 1) % 64) < 32

    def rotary_store(y, out_ref, scale):
        for j in range(SEG // LANES):
            ys = y[:, j * LANES:(j + 1) * LANES]
            partner = jnp.where(first32, pltpu.roll(ys, LANES - 32, 1), pltpu.roll(ys, 32, 1))
            r = ys * cos + partner * sin
            if scale != 1.0:
                r = r * scale
            out_ref[:, j * LANES:(j + 1) * LANES] = r.astype(out_ref.dtype)

    def qknorm_store(y, g_ref, out_ref, scale):
        g = g_ref[...]
        for j in range(SEG // LANES):
            ys = y[:, j * LANES:(j + 1) * LANES]
            sq = ys * ys
            s_lo = jnp.sum(jnp.where(low, sq, 0.0), axis=-1, keepdims=True)
            s_hi = jnp.sum(jnp.where(low, 0.0, sq), axis=-1, keepdims=True)
            inv = jnp.where(low, lax.rsqrt(s_lo * (1.0 / DIFF_DQK) + EPS),
                            lax.rsqrt(s_hi * (1.0 / DIFF_DQK) + EPS))
            r = ys * inv * g
            if scale != 1.0:
                r = r * scale
            out_ref[:, j * LANES:(j + 1) * LANES] = r.astype(out_ref.dtype)

    rotary_store(proj(0), rq_ref, 1.0)
    rotary_store(proj(1), rk_ref, RET_DK ** -0.5)
    rv_ref[...] = proj(2).astype(BF16)
    g = proj(3)
    rg_ref[...] = (g * jax.nn.sigmoid(g)).astype(BF16)
    qknorm_store(proj(4), qg_ref, dq_ref, DIFF_DQK ** -0.5 * LOG2E)
    qknorm_store(proj(5), kg_ref, dk_ref, 1.0)
    dv = proj(6).astype(BF16)
    ones = jnp.ones((dv.shape[0], DIFF_DV), BF16)
    for hh in range(DIFF_HEADS):
        dv_ref[:, 2 * hh * DIFF_DV:(2 * hh + 1) * DIFF_DV] = dv[:, hh * DIFF_DV:(hh + 1) * DIFF_DV]
        dv_ref[:, (2 * hh + 1) * DIFF_DV:(2 * hh + 2) * DIFF_DV] = ones


def _in_proj(x2, gain, w_in, cos_t, sin_t, qg, kg, seq, tm):
    T, D = x2.shape
    n_pos_blocks = seq // tm
    tok = lambda i: (i, 0)
    const = lambda i: (0, 0)
    pos = lambda i: (i % n_pos_blocks, 0)
    out = jax.ShapeDtypeStruct((T, SEG), BF16)
    return pl.pallas_call(
        _in_proj_kernel,
        grid=(T // tm,),
        in_specs=[
            pl.BlockSpec((tm, D), tok),
            pl.BlockSpec((1, D), const),
            pl.BlockSpec((D, N_SEG * SEG), const),
            pl.BlockSpec((tm, LANES), pos),
            pl.BlockSpec((tm, LANES), pos),
            pl.BlockSpec((1, LANES), const),
            pl.BlockSpec((1, LANES), const),
        ],
        out_specs=[pl.BlockSpec((tm, SEG), tok)] * (N_SEG - 1)
        + [pl.BlockSpec((tm, 2 * SEG), tok)],
        out_shape=[out] * (N_SEG - 1) + [jax.ShapeDtypeStruct((T, 2 * SEG), BF16)],
        compiler_params=pltpu.CompilerParams(
            dimension_semantics=("parallel",), vmem_limit_bytes=VMEM_LIMIT),
        name="in_proj",
    )(x2, gain, w_in, cos_t, sin_t, qg, kg)


def _retention_kernel(q_ref, k_ref, v_ref, g_ref, dmask_ref, zeta_ref, xi_ref, gchunk_ref,
                      gs_ref, o_ref, state_ref, *, chunk, n_chunks):
    @pl.when(pl.program_id(1) == 0)
    def _():
        state_ref[...] = jnp.zeros_like(state_ref)

    n_slabs = RET_WIDTH // LANES
    low = _low_half_mask((chunk, LANES))
    row = lax.broadcasted_iota(jnp.int32, (LANES, LANES), 0)
    col = lax.broadcasted_iota(jnp.int32, (LANES, LANES), 1)
    blockdiag = (row < 64) == (col < 64)
    nt = (((1,), (1,)), ((), ()))
    tn = (((0,), (0,)), ((), ()))

    for s in range(n_slabs):
        sl = slice(s * LANES, (s + 1) * LANES)
        zeta = zeta_ref[:, sl]
        xi = xi_ref[:, sl]
        gch = gchunk_ref[:, sl]
        for c in range(n_chunks):
            rows = slice(c * chunk, (c + 1) * chunk)
            q = q_ref[rows, sl]
            k = k_ref[rows, sl]
            v = v_ref[rows, sl]
            zero = jnp.zeros_like(q)
            q_lo = jnp.where(low, q, zero)
            q_hi = jnp.where(low, zero, q)
            s_lo = lax.dot_general(q_lo, k, nt, preferred_element_type=F32) * dmask_ref[2 * s]
            s_hi = lax.dot_general(q_hi, k, nt, preferred_element_type=F32) * dmask_ref[2 * s + 1]
            o_lo = jnp.dot(s_lo.astype(BF16), v, preferred_element_type=F32)
            o_hi = jnp.dot(s_hi.astype(BF16), v, preferred_element_type=F32)
            r_prev = state_ref[s]
            o_inter = jnp.dot(q, r_prev.astype(BF16), preferred_element_type=F32) * xi
            o = jnp.where(low, o_lo, o_hi) + o_inter
            kz = (k.astype(F32) * zeta).astype(BF16)
            kv = lax.dot_general(kz, v, tn, preferred_element_type=F32)
            state_ref[s] = r_prev * gch + jnp.where(blockdiag, kv, 0.0)
            sq = o * o
            m_lo = jnp.sum(jnp.where(low, sq, 0.0), axis=-1, keepdims=True)
            m_hi = jnp.sum(jnp.where(low, 0.0, sq), axis=-1, keepdims=True)
            inv = jnp.where(low, lax.rsqrt(m_lo * (1.0 / 64) + EPS),
                            lax.rsqrt(m_hi * (1.0 / 64) + EPS))
            out = o * inv * g_ref[rows, sl].astype(F32) * gs_ref[:, sl]
            o_ref[rows, sl] = out.astype(o_ref.dtype)


def _retention(rq, rk, rv, rg, dmask, zeta, xi, gchunk, gs_ret, batch, seq, ts, chunk):
    T = rq.shape[0]
    nblk = seq // ts
    tok = lambda b, i: (b * nblk + i, 0)
    c2 = lambda b, i: (0, 0)
    c3 = lambda b, i: (0, 0, 0)
    kern = functools.partial(_retention_kernel, chunk=chunk, n_chunks=ts // chunk)
    return pl.pallas_call(
        kern,
        grid=(batch, nblk),
        in_specs=[pl.BlockSpec((ts, RET_WIDTH), tok)] * 4 + [
            pl.BlockSpec((RET_HEADS, chunk, chunk), c3),
            pl.BlockSpec((chunk, RET_WIDTH), c2),
            pl.BlockSpec((chunk, RET_WIDTH), c2),
            pl.BlockSpec((1, RET_WIDTH), c2),
            pl.BlockSpec((1, RET_WIDTH), c2),
        ],
        out_specs=pl.BlockSpec((ts, RET_WIDTH), tok),
        out_shape=jax.ShapeDtypeStruct((T, RET_WIDTH), BF16),
        scratch_shapes=[pltpu.VMEM((RET_WIDTH // LANES, LANES, LANES), F32)],
        compiler_params=pltpu.CompilerParams(
            dimension_semantics=("parallel", "arbitrary"), vmem_limit_bytes=VMEM_LIMIT),
        name="retention",
    )(rq, rk, rv, rg, dmask, zeta, xi, gchunk, gs_ret)


def _diff_attn_kernel(q_ref, k_ref, v_ref, lam_ref, subln_ref, gs_ref, o_ref,
                      qs_ref, s_ref, m_ref, acc_ref, *, tq, tk, rb, lam_init):
    i = pl.program_id(2)
    q = q_ref[...]
    low = _low_half_mask(q.shape)
    zero = jnp.zeros_like(q)
    qs_ref[:tq, :] = jnp.where(low, q, zero)
    qs_ref[tq:, :] = jnp.where(low, zero, q)
    nt = (((1,), (1,)), ((), ()))

    m_ref[...] = jnp.full_like(m_ref, -jnp.inf)
    acc_ref[...] = jnp.zeros_like(acc_ref)

    n_blocks = 2 * tq // rb
    n_diag = tq // tk
    n_full = i * n_diag

    def scores(j, slot, b):
        start = pl.multiple_of(j * tk, tk)
        rows = slice(b * rb, (b + 1) * rb)
        s_ref[slot, rows, :] = lax.dot_general(
            qs_ref[rows, :], k_ref[pl.ds(start, tk), :], nt, preferred_element_type=F32)

    def consume(j, slot, b, key_offset):
        start = pl.multiple_of(j * tk, tk)
        rows = slice(b * rb, (b + 1) * rb)
        s = s_ref[slot, rows, :]
        if key_offset is not None:
            r = lax.broadcasted_iota(jnp.int32, s.shape, 0) + (b * rb) % tq
            c = lax.broadcasted_iota(jnp.int32, s.shape, 1) + key_offset
            s = jnp.where(c <= r, s, -jnp.inf)
        m_prev = m_ref[rows, :]
        m_new = jnp.maximum(m_prev, jnp.max(s, axis=-1, keepdims=True))
        alpha = jnp.exp2(m_prev - m_new)
        p = jnp.exp2(s - jnp.concatenate([m_new] * (tk // LANES), axis=1))
        pv = jnp.dot(p.astype(BF16), v_ref[pl.ds(start, tk), :],
                     preferred_element_type=F32)
        acc_ref[rows, :] = jnp.concatenate([alpha, alpha], axis=1) * acc_ref[rows, :] + pv
        m_ref[rows, :] = m_new

    def diag_kind(d, b):
        q_first = (b * rb) % tq
        if d * tk > q_first + rb - 1:
            return "skip"
        return "full" if (d + 1) * tk - 1 <= q_first else "mask"

    for b in range(n_blocks):
        scores(0, 0, b)

    def body(jj, carry):
        for u in range(2):
            for b in range(n_blocks):
                scores(2 * jj + u + 1, 1 - u, b)
                consume(2 * jj + u, u, b, None)
        return carry

    lax.fori_loop(0, n_full // 2, body, 0)
    for d in range(n_diag):
        for b in range(n_blocks):
            if d + 1 < n_diag and diag_kind(d + 1, b) != "skip":
                scores(n_full + d + 1, (d + 1) % 2, b)
            kind = diag_kind(d, b)
            if kind != "skip":
                consume(n_full + d, d % 2, b, d * tk if kind == "mask" else None)

    lv = lam_ref[...]
    lam = (jnp.exp(jnp.sum(lv[0:1] * lv[1:2], axis=-1, keepdims=True))
           - jnp.exp(jnp.sum(lv[2:3] * lv[3:4], axis=-1, keepdims=True)) + lam_init)
    acc = acc_ref[...]
    o_all = acc[:, :DIFF_DV] * (1.0 / acc[:, DIFF_DV:])
    o = o_all[:tq] - lam * o_all[tq:]
    y = o * _rms(o) * subln_ref[...] * (1.0 - lam_init)
    o_ref[...] = (y * gs_ref[...]).astype(o_ref.dtype)


def _diff_attn(dq, dk, dv_aug, lam_p, subln, gs_diff, batch, seq, tq, tk, rb, lam_init):
    T = dq.shape[0]
    nq = seq // tq
    assert seq % tq == 0 and tq % (2 * tk) == 0 and (2 * tq) % rb == 0 and tq % rb == 0
    kern = functools.partial(_diff_attn_kernel, tq=tq, tk=tk, rb=rb, lam_init=lam_init)
    return pl.pallas_call(
        kern,
        grid=(batch, DIFF_HEADS, nq),
        in_specs=[
            pl.BlockSpec((tq, LANES), lambda b, h, i: (b * nq + i, h)),
            pl.BlockSpec((seq, LANES), lambda b, h, i: (b, h)),
            pl.BlockSpec((seq, 2 * DIFF_DV), lambda b, h, i: (b, h)),
            pl.BlockSpec((4, DIFF_DQK), lambda b, h, i: (0, 0)),
            pl.BlockSpec((1, LANES), lambda b, h, i: (0, 0)),
            pl.BlockSpec((1, LANES), lambda b, h, i: (0, h)),
        ],
        out_specs=pl.BlockSpec((tq, LANES), lambda b, h, i: (b * nq + i, h)),
        out_shape=jax.ShapeDtypeStruct((T, DIFF_WIDTH), BF16),
        scratch_shapes=[pltpu.VMEM((2 * tq, LANES), BF16), pltpu.VMEM((2, 2 * tq, tk), F32),
                        pltpu.VMEM((2 * tq, LANES), F32), pltpu.VMEM((2 * tq, 2 * DIFF_DV), F32)],
        compiler_params=pltpu.CompilerParams(
            dimension_semantics=("parallel", "parallel", "arbitrary"),
            vmem_limit_bytes=VMEM_LIMIT),
        name="diff_attn",
    )(dq, dk, dv_aug, lam_p, subln, gs_diff)


def _mem_kv_kernel(mem_ref, gain_ref, w_ref, kg_ref, k_ref, v_ref):
    m = mem_ref[...]
    h = (m * _rms(m) * gain_ref[...]).astype(BF16)
    D = m.shape[1]
    dh = D // XATTN_HEADS
    k = jnp.dot(h, w_ref[:, :D], preferred_element_type=F32)
    v_ref[...] = jnp.dot(h, w_ref[:, D:], preferred_element_type=F32).astype(v_ref.dtype)
    for hh in range(XATTN_HEADS):
        kh = k[:, hh * dh:(hh + 1) * dh]
        k_ref[:, hh * dh:(hh + 1) * dh] = (kh * _rms(kh) * kg_ref[...]).astype(k_ref.dtype)


def _mem_kv(mem2, gain, xkv, kg, batch, mem_len):
    BM, D = mem2.shape
    const = lambda b: (0, 0)
    out = jax.ShapeDtypeStruct((BM, D), BF16)
    return pl.pallas_call(
        _mem_kv_kernel,
        grid=(batch,),
        in_specs=[
            pl.BlockSpec((mem_len, D), lambda b: (b, 0)),
            pl.BlockSpec((1, D), const),
            pl.BlockSpec((D, 2 * D), const),
            pl.BlockSpec((1, D // XATTN_HEADS), const),
        ],
        out_specs=[pl.BlockSpec((mem_len, D), lambda b: (b, 0))] * 2,
        out_shape=[out, out],
        compiler_params=pltpu.CompilerParams(
            dimension_semantics=("parallel",), vmem_limit_bytes=VMEM_LIMIT),
        name="mem_kv",
    )(mem2, gain, xkv, kg)


def _mix_xattn_kernel(x_ref, oret_ref, odiff_ref, wout_ref, nx_ref, xq_ref, qg_ref,
                      kmem_ref, vmem_ref, xo_ref, out_ref):
    D = x_ref.shape[1]
    dh = D // XATTN_HEADS
    x1 = (x_ref[...]
          + jnp.dot(oret_ref[...], wout_ref[:RET_WIDTH, :], preferred_element_type=F32)
          + jnp.dot(odiff_ref[...], wout_ref[RET_WIDTH:, :], preferred_element_type=F32))
    h = (x1 * _rms(x1) * nx_ref[...]).astype(BF16)
    q = jnp.dot(h, xq_ref[...], preferred_element_type=F32)
    nt = (((1,), (1,)), ((), ()))
    outs = []
    for hh in range(XATTN_HEADS):
        sl = slice(hh * dh, (hh + 1) * dh)
        qh = q[:, sl]
        qn = (qh * _rms(qh) * qg_ref[...] * (dh ** -0.5)).astype(BF16)
        s = lax.dot_general(qn, kmem_ref[:, sl], nt, preferred_element_type=F32)
        e = jnp.exp(s - jnp.max(s, axis=-1, keepdims=True))
        p = e * (1.0 / jnp.sum(e, axis=-1, keepdims=True))
        outs.append(jnp.dot(p.astype(BF16), vmem_ref[:, sl],
                            preferred_element_type=F32).astype(BF16))
    o = jnp.concatenate(outs, axis=1)
    out_ref[...] = x1 + jnp.dot(o, xo_ref[...], preferred_element_type=F32)


def _mix_xattn(x2, o_ret, o_diff, w_out, nx, xq, qg, kmem, vmem, xo, seq, mem_len, tm):
    T, D = x2.shape
    nblk = seq // tm
    tok = lambda i: (i, 0)
    const = lambda i: (0, 0)
    memb = lambda i: (i // nblk, 0)
    return pl.pallas_call(
        _mix_xattn_kernel,
        grid=(T // tm,),
        in_specs=[
            pl.BlockSpec((tm, D), tok),
            pl.BlockSpec((tm, RET_WIDTH), tok),
            pl.BlockSpec((tm, DIFF_WIDTH), tok),
            pl.BlockSpec((D, D), const),
            pl.BlockSpec((1, D), const),
            pl.BlockSpec((D, D), const),
            pl.BlockSpec((1, D // XATTN_HEADS), const),
            pl.BlockSpec((mem_len, D), memb),
            pl.BlockSpec((mem_len, D), memb),
            pl.BlockSpec((D, D), const),
        ],
        out_specs=pl.BlockSpec((tm, D), tok),
        out_shape=jax.ShapeDtypeStruct((T, D), F32),
        compiler_params=pltpu.CompilerParams(
            dimension_semantics=("parallel",), vmem_limit_bytes=VMEM_LIMIT),
        name="mix_xattn",
    )(x2, o_ret, o_diff, w_out, nx, xq, qg, kmem, vmem, xo)


def _ffn_kernel(x_ref, gain_ref, wg_ref, wu_ref, wd_ref, out_ref, *, hidden_chunk):
    x = x_ref[...]
    h = (x * _rms(x) * gain_ref[...]).astype(BF16)
    hidden = wg_ref.shape[1]
    acc = x
    for c in range(hidden // hidden_chunk):
        sl = slice(c * hidden_chunk, (c + 1) * hidden_chunk)
        g = jnp.dot(h, wg_ref[:, sl], preferred_element_type=F32)
        u = jnp.dot(h, wu_ref[:, sl], preferred_element_type=F32)
        a = (g * jax.nn.sigmoid(g) * u).astype(BF16)
        acc = acc + jnp.dot(a, wd_ref[sl, :], preferred_element_type=F32)
    out_ref[...] = acc


def _ffn(x2, gain, wg, wu, wd, tm, hidden_chunk):
    T, D = x2.shape
    hidden = wg.shape[1]
    tok = lambda i: (i, 0)
    const = lambda i: (0, 0)
    kern = functools.partial(_ffn_kernel, hidden_chunk=hidden_chunk)
    return pl.pallas_call(
        kern,
        grid=(T // tm,),
        in_specs=[
            pl.BlockSpec((tm, D), tok),
            pl.BlockSpec((1, D), const),
            pl.BlockSpec((D, hidden), const),
            pl.BlockSpec((D, hidden), const),
            pl.BlockSpec((hidden, D), const),
        ],
        out_specs=pl.BlockSpec((tm, D), tok),
        out_shape=jax.ShapeDtypeStruct((T, D), F32),
        compiler_params=pltpu.CompilerParams(
            dimension_semantics=("parallel",), vmem_limit_bytes=VMEM_LIMIT),
        name="ffn",
    )(x2, gain, wg, wu, wd)


def _rotary_tables(seq):
    half = RET_DK // 2
    inv = 1.0 / (RET_THETA_BASE ** jnp.linspace(0.0, 1.0, half, dtype=F32))
    ang = jnp.arange(seq).astype(F32)[:, None] * inv[None, :]
    cos, sin = jnp.cos(ang), jnp.sin(ang)
    cos_t = jnp.tile(cos, (1, LANES // half))
    sin_t = jnp.tile(jnp.concatenate([-sin, sin], axis=1), (1, LANES // RET_DK))
    return cos_t, sin_t


def _decay_tables(chunk):
    H = RET_HEADS
    log_g = jnp.log1p(-(2.0 ** (-5.0 - jnp.arange(H, dtype=F32))))
    idx = jnp.arange(chunk, dtype=F32)
    rel = idx[:, None] - idx[None, :]
    dmask = jnp.where(rel >= 0, jnp.exp(log_g[:, None, None] * jnp.maximum(rel, 0.0)), 0.0)
    zeta = jnp.exp(log_g[:, None] * (chunk - 1.0 - idx))
    xi = jnp.exp(log_g[:, None] * (idx + 1.0))
    g_chunk = jnp.exp(log_g * chunk)
    widen = lambda t: jnp.repeat(t.T, RET_DK, axis=1)
    return dmask, widen(zeta), widen(xi), jnp.repeat(g_chunk, RET_DK)[None, :]


def kernel(x, mem, norm_mix, w_in, diff_q_gain, diff_k_gain, diff_lambda, diff_subln,
           group_scale, w_out, norm_x, norm_mem, xq, xkv, xq_gain, xk_gain, xo,
           norm_ffn, w_gate, w_up, w_down):
    B, S, D = x.shape
    M = mem.shape[1]
    depth = norm_mix.shape[0]
    x2 = x.reshape(B * S, D)
    mem2 = mem.reshape(B * M, D)

    ret_chunk = 256
    cos_t, sin_t = _rotary_tables(S)
    dmask, zeta, xi, gchunk = _decay_tables(ret_chunk)
    row = lambda a: a.reshape(1, -1)
    twice = lambda a: jnp.tile(a.reshape(1, -1), (1, 2))

    for l in range(depth):
        lam_init = 0.8 - 0.6 * math.exp(-0.3 * l)
        rq, rk, rv, rg, dq, dk, dv = _in_proj(
            x2, row(norm_mix[l]), w_in[l].astype(BF16), cos_t, sin_t,
            twice(diff_q_gain[l]), twice(diff_k_gain[l]), S, tm=512)
        o_ret = _retention(rq, rk, rv, rg, dmask, zeta, xi, gchunk,
                           row(group_scale[l, :RET_WIDTH]), B, S, ts=512, chunk=ret_chunk)
        o_diff = _diff_attn(dq, dk, dv, diff_lambda[l], row(diff_subln[l]),
                            row(group_scale[l, RET_WIDTH:]), B, S, tq=1024, tk=512, rb=256,
                            lam_init=lam_init)
        kmem, vmem = _mem_kv(mem2, row(norm_mem[l]), xkv[l].astype(BF16), row(xk_gain[l]), B, M)
        x2 = _mix_xattn(x2, o_ret, o_diff, w_out[l].astype(BF16), row(norm_x[l]),
                        xq[l].astype(BF16), row(xq_gain[l]), kmem, vmem, xo[l].astype(BF16),
                        S, M, tm=512)
        x2 = _ffn(x2, row(norm_ffn[l]), w_gate[l].astype(BF16), w_up[l].astype(BF16),
                  w_down[l].astype(BF16), tm=256, hidden_chunk=256)
    return x2.reshape(B, S, D)
```

```python
import functools
import math

import jax
import jax.numpy as jnp
from jax import lax
from jax.experimental import pallas as pl
from jax.experimental.pallas import tpu as pltpu

F32 = jnp.float32
BF16 = jnp.bfloat16

EPS = 1e-6
LANES = 128
VMEM_LIMIT = 56 * 1024 * 1024

RET_HEADS = 8
RET_DK = 64
RET_WIDTH = 512
RET_THETA_BASE = 10000.0
DIFF_HEADS = 4
DIFF_DQK = 64
DIFF_DV = 128
DIFF_WIDTH = 512
XATTN_HEADS = 4
SEG = 512
N_SEG = 7
LOG2E = math.log2(math.e)


def _rms(x, axis=-1):
    return lax.rsqrt(jnp.mean(x * x, axis=axis, keepdims=True) + EPS)


def _low_half_mask(shape):
    lane = lax.broadcasted_iota(jnp.int32, shape, len(shape) - 1)
    return (lane % LANES) < (LANES // 2)


def _in_proj_kernel(x_ref, gain_ref, w_ref, cos_ref, sin_ref, qg_ref, kg_ref,
                    rq_ref, rk_ref, rv_ref, rg_ref, dq_ref, dk_ref, dv_ref):
    x = x_ref[...]
    h = (x * _rms(x) * gain_ref[...]).astype(BF16)

    def proj(seg):
        return jnp.dot(h, w_ref[:, seg * SEG:(seg + 1) * SEG], preferred_element_type=F32)

    cos = cos_ref[...]
    sin = sin_ref[...]
    low = _low_half_mask(cos.shape)
    first32 = (lax.broadcasted_iota(jnp.int32, cos.shape, 1) % 64) < 32

    def rotary_store(y, out_ref, scale):
        for j in range(SEG // LANES):
            ys = y[:, j * LANES:(j + 1) * LANES]
            partner = jnp.where(first32, pltpu.roll(ys, LANES - 32, 1), pltpu.roll(ys, 32, 1))
            r = ys * cos + partner * sin
            if scale != 1.0:
                r = r * scale
            out_ref[:, j * LANES:(j + 1) * LANES] = r.astype(out_ref.dtype)

    def qknorm_store(y, g_ref, out_ref, scale):
        g = g_ref[...]
        for j in range(SEG // LANES):
            ys = y[:, j * LANES:(j + 1) * LANES]
            sq = ys * ys
            s_lo = jnp.sum(jnp.where(low, sq, 0.0), axis=-1, keepdims=True)
            s_hi = jnp.sum(jnp.where(low, 0.0, sq), axis=-1, keepdims=True)
            inv = jnp.where(low, lax.rsqrt(s_lo * (1.0 / DIFF_DQK) + EPS),
                            lax.rsqrt(s_hi * (1.0 / DIFF_DQK) + EPS))
            r = ys * inv * g
            if scale != 1.0:
                r = r * scale
            out_ref[:, j * LANES:(j + 1) * LANES] = r.astype(out_ref.dtype)

    rotary_store(proj(0), rq_ref, 1.0)
    rotary_store(proj(1), rk_ref, RET_DK ** -0.5)
    rv_ref[...] = proj(2).astype(BF16)
    g = proj(3)
    rg_ref[...] = (g * jax.nn.sigmoid(g)).astype(BF16)
    qknorm_store(proj(4), qg_ref, dq_ref, DIFF_DQK ** -0.5 * LOG2E)
    qknorm_store(proj(5), kg_ref, dk_ref, 1.0)
    dv = proj(6).astype(BF16)
    ones = jnp.ones((dv.shape[0], DIFF_DV), BF16)
    for hh in range(DIFF_HEADS):
        dv_ref[:, 2 * hh * DIFF_DV:(2 * hh + 1) * DIFF_DV] = dv[:, hh * DIFF_DV:(hh + 1) * DIFF_DV]
        dv_ref[:, (2 * hh + 1) * DIFF_DV:(2 * hh + 2) * DIFF_DV] = ones


def _in_proj(x2, gain, w_in, cos_t, sin_t, qg, kg, seq, tm):
    T, D = x2.shape
    n_pos_blocks = seq // tm
    tok = lambda i: (i, 0)
    const = lambda i: (0, 0)
    pos = lambda i: (i % n_pos_blocks, 0)
    out = jax.ShapeDtypeStruct((T, SEG), BF16)
    return pl.pallas_call(
        _in_proj_kernel,
        grid=(T // tm,),
        in_specs=[
            pl.BlockSpec((tm, D), tok),
            pl.BlockSpec((1, D), const),
            pl.BlockSpec((D, N_SEG * SEG), const),
            pl.BlockSpec((tm, LANES), pos),
            pl.BlockSpec((tm, LANES), pos),
            pl.BlockSpec((1, LANES), const),
            pl.BlockSpec((1, LANES), const),
        ],
        out_specs=[pl.BlockSpec((tm, SEG), tok)] * (N_SEG - 1)
        + [pl.BlockSpec((tm, 2 * SEG), tok)],
        out_shape=[out] * (N_SEG - 1) + [jax.ShapeDtypeStruct((T, 2 * SEG), BF16)],
        compiler_params=pltpu.CompilerParams(
            dimension_semantics=("parallel",), vmem_limit_bytes=VMEM_LIMIT),
        name="in_proj",
    )(x2, gain, w_in, cos_t, sin_t, qg, kg)


def _retention_kernel(q_ref, k_ref, v_ref, g_ref, dmask_ref, zeta_ref, xi_ref, gchunk_ref,
                      gs_ref, o_ref, state_ref, *, chunk, n_chunks):
    @pl.when(pl.program_id(1) == 0)
    def _():
        state_ref[...] = jnp.zeros_like(state_ref)

    n_slabs = RET_WIDTH // LANES
    low = _low_half_mask((chunk, LANES))
    row = lax.broadcasted_iota(jnp.int32, (LANES, LANES), 0)
    col = lax.broadcasted_iota(jnp.int32, (LANES, LANES), 1)
    blockdiag = (row < 64) == (col < 64)
    nt = (((1,), (1,)), ((), ()))
    tn = (((0,), (0,)), ((), ()))

    for s in range(n_slabs):
        sl = slice(s * LANES, (s + 1) * LANES)
        zeta = zeta_ref[:, sl]
        xi = xi_ref[:, sl]
        gch = gchunk_ref[:, sl]
        for c in range(n_chunks):
            rows = slice(c * chunk, (c + 1) * chunk)
            q = q_ref[rows, sl]
            k = k_ref[rows, sl]
            v = v_ref[rows, sl]
            zero = jnp.zeros_like(q)
            q_lo = jnp.where(low, q, zero)
            q_hi = jnp.where(low, zero, q)
            s_lo = lax.dot_general(q_lo, k, nt, preferred_element_type=F32) * dmask_ref[2 * s]
            s_hi = lax.dot_general(q_hi, k, nt, preferred_element_type=F32) * dmask_ref[2 * s + 1]
            o_lo = jnp.dot(s_lo.astype(BF16), v, preferred_element_type=F32)
            o_hi = jnp.dot(s_hi.astype(BF16), v, preferred_element_type=F32)
            r_prev = state_ref[s]
            o_inter = jnp.dot(q, r_prev.astype(BF16), preferred_element_type=F32) * xi
            o = jnp.where(low, o_lo, o_hi) + o_inter
            kz = (k.astype(F32) * zeta).astype(BF16)
            kv = lax.dot_general(kz, v, tn, preferred_element_type=F32)
            state_ref[s] = r_prev * gch + jnp.where(blockdiag, kv, 0.0)
            sq = o * o
            m_lo = jnp.sum(jnp.where(low, sq, 0.0), axis=-1, keepdims=True)
            m_hi = jnp.sum(jnp.where(low, 0.0, sq), axis=-1, keepdims=True)
            inv = jnp.where(low, lax.rsqrt(m_lo * (1.0 / 64) + EPS),
                            lax.rsqrt(m_hi * (1.0 / 64) + EPS))
            out = o * inv * g_ref[rows, sl].astype(F32) * gs_ref[:, sl]
            o_ref[rows, sl] = out.astype(o_ref.dtype)


def _retention(rq, rk, rv, rg, dmask, zeta, xi, gchunk, gs_ret, batch, seq, ts, chunk):
    T = rq.shape[0]
    nblk = seq // ts
    tok = lambda b, i: (b * nblk + i, 0)
    c2 = lambda b, i: (0, 0)
    c3 = lambda b, i: (0, 0, 0)
    kern = functools.partial(_retention_kernel, chunk=chunk, n_chunks=ts // chunk)
    return pl.pallas_call(
        kern,
        grid=(batch, nblk),
        in_specs=[pl.BlockSpec((ts, RET_WIDTH), tok)] * 4 + [
            pl.BlockSpec((RET_HEADS, chunk, chunk), c3),
            pl.BlockSpec((chunk, RET_WIDTH), c2),
            pl.BlockSpec((chunk, RET_WIDTH), c2),
            pl.BlockSpec((1, RET_WIDTH), c2),
            pl.BlockSpec((1, RET_WIDTH), c2),
        ],
        out_specs=pl.BlockSpec((ts, RET_WIDTH), tok),
        out_shape=jax.ShapeDtypeStruct((T, RET_WIDTH), BF16),
        scratch_shapes=[pltpu.VMEM((RET_WIDTH // LANES, LANES, LANES), F32)],
        compiler_params=pltpu.CompilerParams(
            dimension_semantics=("parallel", "arbitrary"), vmem_limit_bytes=VMEM_LIMIT),
        name="retention",
    )(rq, rk, rv, rg, dmask, zeta, xi, gchunk, gs_ret)


def _diff_attn_kernel(q_ref, k_ref, v_ref, lam_ref, subln_ref, gs_ref, o_ref,
                      qs_ref, s_ref, m_ref, acc_ref, *, tq, tk, rb, lam_init):
    i = pl.program_id(2)
    q = q_ref[...]
    low = _low_half_mask(q.shape)
    zero = jnp.zeros_like(q)
    qs_ref[:tq, :] = jnp.where(low, q, zero)
    qs_ref[tq:, :] = jnp.where(low, zero, q)
    nt = (((1,), (1,)), ((), ()))

    m_ref[...] = jnp.full_like(m_ref, -jnp.inf)
    acc_ref[...] = jnp.zeros_like(acc_ref)

    n_blocks = 2 * tq // rb
    n_diag = tq // tk
    n_full = i * n_diag

    def scores(j, b):
        start = pl.multiple_of(j * tk, tk)
        rows = slice(b * rb, (b + 1) * rb)
        s_ref[rows, :] = lax.dot_general(
            qs_ref[rows, :], k_ref[pl.ds(start, tk), :], nt, preferred_element_type=F32)

    def consume(j, b, key_offset):
        start = pl.multiple_of(j * tk, tk)
        rows = slice(b * rb, (b + 1) * rb)
        s = s_ref[rows, :]
        if key_offset is not None:
            r = lax.broadcasted_iota(jnp.int32, s.shape, 0) + (b * rb) % tq
            c = lax.broadcasted_iota(jnp.int32, s.shape, 1) + key_offset
            s = jnp.where(c <= r, s, -jnp.inf)
        m_prev = m_ref[rows, :]
        m_new = jnp.maximum(m_prev, jnp.max(s, axis=-1, keepdims=True))
        alpha = jnp.exp2(m_prev - m_new)
        p = jnp.exp2(s - jnp.concatenate([m_new] * (tk // LANES), axis=1))
        pv = jnp.dot(p.astype(BF16), v_ref[pl.ds(start, tk), :],
                     preferred_element_type=F32)
        acc_ref[rows, :] = jnp.concatenate([alpha, alpha], axis=1) * acc_ref[rows, :] + pv
        m_ref[rows, :] = m_new

    def diag_kind(d, b):
        q_first = (b * rb) % tq
        if d * tk > q_first + rb - 1:
            return "skip"
        return "full" if (d + 1) * tk - 1 <= q_first else "mask"

    ahead = n_blocks // 2
    for b in range(ahead):
        scores(0, b)

    def body(jj, carry):
        for u in range(n_diag):
            j = jj * n_diag + u
            for b in range(n_blocks):
                if b + ahead < n_blocks:
                    scores(j, b + ahead)
                else:
                    scores(j + 1, b + ahead - n_blocks)
                consume(j, b, None)
        return carry

    lax.fori_loop(0, i, body, 0)
    for d in range(n_diag):
        for b in range(n_blocks):
            nd, nb = (d, b + ahead) if b + ahead < n_blocks else (d + 1, b + ahead - n_blocks)
            if nd < n_diag and diag_kind(nd, nb) != "skip":
                scores(n_full + nd, nb)
            kind = diag_kind(d, b)
            if kind != "skip":
                consume(n_full + d, b, d * tk if kind == "mask" else None)

    lv = lam_ref[...]
    lam = (jnp.exp(jnp.sum(lv[0:1] * lv[1:2], axis=-1, keepdims=True))
           - jnp.exp(jnp.sum(lv[2:3] * lv[3:4], axis=-1, keepdims=True)) + lam_init)
    acc = acc_ref[...]
    o_all = acc[:, :DIFF_DV] * (1.0 / acc[:, DIFF_DV:])
    o = o_all[:tq] - lam * o_all[tq:]
    y = o * _rms(o) * subln_ref[...] * (1.0 - lam_init)
    o_ref[...] = (y * gs_ref[...]).astype(o_ref.dtype)


def _diff_attn(dq, dk, dv_aug, lam_p, subln, gs_diff, batch, seq, tq, tk, rb, lam_init):
    T = dq.shape[0]
    nq = seq // tq
    assert seq % tq == 0 and tq % tk == 0 and tq % rb == 0
    kern = functools.partial(_diff_attn_kernel, tq=tq, tk=tk, rb=rb, lam_init=lam_init)
    return pl.pallas_call(
        kern,
        grid=(batch, DIFF_HEADS, nq),
        in_specs=[
            pl.BlockSpec((tq, LANES), lambda b, h, i: (b * nq + i, h)),
            pl.BlockSpec((seq, LANES), lambda b, h, i: (b, h)),
            pl.BlockSpec((seq, 2 * DIFF_DV), lambda b, h, i: (b, h)),
            pl.BlockSpec((4, DIFF_DQK), lambda b, h, i: (0, 0)),
            pl.BlockSpec((1, LANES), lambda b, h, i: (0, 0)),
            pl.BlockSpec((1, LANES), lambda b, h, i: (0, h)),
        ],
        out_specs=pl.BlockSpec((tq, LANES), lambda b, h, i: (b * nq + i, h)),
        out_shape=jax.ShapeDtypeStruct((T, DIFF_WIDTH), BF16),
        scratch_shapes=[pltpu.VMEM((2 * tq, LANES), BF16), pltpu.VMEM((2 * tq, tk), F32),
                        pltpu.VMEM((2 * tq, LANES), F32), pltpu.VMEM((2 * tq, 2 * DIFF_DV), F32)],
        compiler_params=pltpu.CompilerParams(
            dimension_semantics=("parallel", "parallel", "arbitrary"),
            vmem_limit_bytes=VMEM_LIMIT),
        name="diff_attn",
    )(dq, dk, dv_aug, lam_p, subln, gs_diff)


def _mem_kv_kernel(mem_ref, gain_ref, w_ref, kg_ref, k_ref, v_ref):
    m = mem_ref[...]
    h = (m * _rms(m) * gain_ref[...]).astype(BF16)
    D = m.shape[1]
    dh = D // XATTN_HEADS
    k = jnp.dot(h, w_ref[:, :D], preferred_element_type=F32)
    v_ref[...] = jnp.dot(h, w_ref[:, D:], preferred_element_type=F32).astype(v_ref.dtype)
    for hh in range(XATTN_HEADS):
        kh = k[:, hh * dh:(hh + 1) * dh]
        k_ref[:, hh * dh:(hh + 1) * dh] = (kh * _rms(kh) * kg_ref[...]).astype(k_ref.dtype)


def _mem_kv(mem2, gain, xkv, kg, batch, mem_len):
    BM, D = mem2.shape
    const = lambda b: (0, 0)
    out = jax.ShapeDtypeStruct((BM, D), BF16)
    return pl.pallas_call(
        _mem_kv_kernel,
        grid=(batch,),
        in_specs=[
            pl.BlockSpec((mem_len, D), lambda b: (b, 0)),
            pl.BlockSpec((1, D), const),
            pl.BlockSpec((D, 2 * D), const),
            pl.BlockSpec((1, D // XATTN_HEADS), const),
        ],
        out_specs=[pl.BlockSpec((mem_len, D), lambda b: (b, 0))] * 2,
        out_shape=[out, out],
        compiler_params=pltpu.CompilerParams(
            dimension_semantics=("parallel",), vmem_limit_bytes=VMEM_LIMIT),
        name="mem_kv",
    )(mem2, gain, xkv, kg)


def _mix_xattn_kernel(x_ref, oret_ref, odiff_ref, wout_ref, nx_ref, xq_ref, qg_ref,
                      kmem_ref, vmem_ref, xo_ref, out_ref):
    D = x_ref.shape[1]
    dh = D // XATTN_HEADS
    x1 = (x_ref[...]
          + jnp.dot(oret_ref[...], wout_ref[:RET_WIDTH, :], preferred_element_type=F32)
          + jnp.dot(odiff_ref[...], wout_ref[RET_WIDTH:, :], preferred_element_type=F32))
    h = (x1 * _rms(x1) * nx_ref[...]).astype(BF16)
    q = jnp.dot(h, xq_ref[...], preferred_element_type=F32)
    nt = (((1,), (1,)), ((), ()))
    outs = []
    for hh in range(XATTN_HEADS):
        sl = slice(hh * dh, (hh + 1) * dh)
        qh = q[:, sl]
        qn = (qh * _rms(qh) * qg_ref[...] * (dh ** -0.5)).astype(BF16)
        s = lax.dot_general(qn, kmem_ref[:, sl], nt, preferred_element_type=F32)
        e = jnp.exp(s - jnp.max(s, axis=-1, keepdims=True))
        p = e * (1.0 / jnp.sum(e, axis=-1, keepdims=True))
        outs.append(jnp.dot(p.astype(BF16), vmem_ref[:, sl],
                            preferred_element_type=F32).astype(BF16))
    o = jnp.concatenate(outs, axis=1)
    out_ref[...] = x1 + jnp.dot(o, xo_ref[...], preferred_element_type=F32)


def _mix_xattn(x2, o_ret, o_diff, w_out, nx, xq, qg, kmem, vmem, xo, seq, mem_len, tm):
    T, D = x2.shape
    nblk = seq // tm
    tok = lambda i: (i, 0)
    const = lambda i: (0, 0)
    memb = lambda i: (i // nblk, 0)
    return pl.pallas_call(
        _mix_xattn_kernel,
        grid=(T // tm,),
        in_specs=[
            pl.BlockSpec((tm, D), tok),
            pl.BlockSpec((tm, RET_WIDTH), tok),
            pl.BlockSpec((tm, DIFF_WIDTH), tok),
            pl.BlockSpec((D, D), const),
            pl.BlockSpec((1, D), const),
            pl.BlockSpec((D, D), const),
            pl.BlockSpec((1, D // XATTN_HEADS), const),
            pl.BlockSpec((mem_len, D), memb),
            pl.BlockSpec((mem_len, D), memb),
            pl.BlockSpec((D, D), const),
        ],
        out_specs=pl.BlockSpec((tm, D), tok),
        out_shape=jax.ShapeDtypeStruct((T, D), F32),
        compiler_params=pltpu.CompilerParams(
            dimension_semantics=("parallel",), vmem_limit_bytes=VMEM_LIMIT),
        name="mix_xattn",
    )(x2, o_ret, o_diff, w_out, nx, xq, qg, kmem, vmem, xo)


def _ffn_kernel(x_ref, gain_ref, wg_ref, wu_ref, wd_ref, out_ref, *, hidden_chunk):
    x = x_ref[...]
    h = (x * _rms(x) * gain_ref[...]).astype(BF16)
    hidden = wg_ref.shape[1]
    acc = x
    for c in range(hidden // hidden_chunk):
        sl = slice(c * hidden_chunk, (c + 1) * hidden_chunk)
        g = jnp.dot(h, wg_ref[:, sl], preferred_element_type=F32)
        u = jnp.dot(h, wu_ref[:, sl], preferred_element_type=F32)
        a = (g * jax.nn.sigmoid(g) * u).astype(BF16)
        acc = acc + jnp.dot(a, wd_ref[sl, :], preferred_element_type=F32)
    out_ref[...] = acc


def _ffn(x2, gain, wg, wu, wd, tm, hidden_chunk):
    T, D = x2.shape
    hidden = wg.shape[1]
    tok = lambda i: (i, 0)
    const = lambda i: (0, 0)
    kern = functools.partial(_ffn_kernel, hidden_chunk=hidden_chunk)
    return pl.pallas_call(
        kern,
        grid=(T // tm,),
        in_specs=[
            pl.BlockSpec((tm, D), tok),
            pl.BlockSpec((1, D), const),
            pl.BlockSpec((D, hidden), const),
            pl.BlockSpec((D, hidden), const),
            pl.BlockSpec((hidden, D), const),
        ],
        out_specs=pl.BlockSpec((tm, D), tok),
        out_shape=jax.ShapeDtypeStruct((T, D), F32),
        compiler_params=pltpu.CompilerParams(
            dimension_semantics=("parallel",), vmem_limit_bytes=VMEM_LIMIT),
        name="ffn",
    )(x2, gain, wg, wu, wd)


def _rotary_tables(seq):
    half = RET_DK // 2
    inv = 1.0 / (RET_THETA_BASE ** jnp.linspace(0.0, 1.0, half, dtype=F32))
    ang = jnp.arange(seq).astype(F32)[:, None] * inv[None, :]
    cos, sin = jnp.cos(ang), jnp.sin(ang)
    cos_t = jnp.tile(cos, (1, LANES // half))
    sin_t = jnp.tile(jnp.concatenate([-sin, sin], axis=1), (1, LANES // RET_DK))
    return cos_t, sin_t


def _decay_tables(chunk):
    H = RET_HEADS
    log_g = jnp.log1p(-(2.0 ** (-5.0 - jnp.arange(H, dtype=F32))))
    idx = jnp.arange(chunk, dtype=F32)
    rel = idx[:, None] - idx[None, :]
    dmask = jnp.where(rel >= 0, jnp.exp(log_g[:, None, None] * jnp.maximum(rel, 0.0)), 0.0)
    zeta = jnp.exp(log_g[:, None] * (chunk - 1.0 - idx))
    xi = jnp.exp(log_g[:, None] * (idx + 1.0))
    g_chunk = jnp.exp(log_g * chunk)
    widen = lambda t: jnp.repeat(t.T, RET_DK, axis=1)
    return dmask, widen(zeta), widen(xi), jnp.repeat(g_chunk, RET_DK)[None, :]


def kernel(x, mem, norm_mix, w_in, diff_q_gain, diff_k_gain, diff_lambda, diff_subln,
           group_scale, w_out, norm_x, norm_mem, xq, xkv, xq_gain, xk_gain, xo,
           norm_ffn, w_gate, w_up, w_down):
    B, S, D = x.shape
    M = mem.shape[1]
    depth = norm_mix.shape[0]
    x2 = x.reshape(B * S, D)
    mem2 = mem.reshape(B * M, D)

    ret_chunk = 256
    cos_t, sin_t = _rotary_tables(S)
    dmask, zeta, xi, gchunk = _decay_tables(ret_chunk)
    row = lambda a: a.reshape(1, -1)
    twice = lambda a: jnp.tile(a.reshape(1, -1), (1, 2))

    for l in range(depth):
        lam_init = 0.8 - 0.6 * math.exp(-0.3 * l)
        rq, rk, rv, rg, dq, dk, dv = _in_proj(
            x2, row(norm_mix[l]), w_in[l].astype(BF16), cos_t, sin_t,
            twice(diff_q_gain[l]), twice(diff_k_gain[l]), S, tm=512)
        o_ret = _retention(rq, rk, rv, rg, dmask, zeta, xi, gchunk,
                           row(group_scale[l, :RET_WIDTH]), B, S, ts=512, chunk=ret_chunk)
        o_diff = _diff_attn(dq, dk, dv, diff_lambda[l], row(diff_subln[l]),
                            row(group_scale[l, RET_WIDTH:]), B, S, tq=1024, tk=512, rb=256,
                            lam_init=lam_init)
        kmem, vmem = _mem_kv(mem2, row(norm_mem[l]), xkv[l].astype(BF16), row(xk_gain[l]), B, M)
        x2 = _mix_xattn(x2, o_ret, o_diff, w_out[l].astype(BF16), row(norm_x[l]),
                        xq[l].astype(BF16), row(xq_gain[l]), kmem, vmem, xo[l].astype(BF16),
                        S, M, tm=512)
        x2 = _ffn(x2, row(norm_ffn[l]), w_gate[l].astype(BF16), w_up[l].astype(BF16),
                  w_down[l].astype(BF16), tm=512, hidden_chunk=256)
    return x2.reshape(B, S, D)
```

```python
import functools
import math

import jax
import jax.numpy as jnp
from jax import lax
from jax.experimental import pallas as pl
from jax.experimental.pallas import tpu as pltpu

F32 = jnp.float32
BF16 = jnp.bfloat16

EPS = 1e-6
LANES = 128
VMEM_LIMIT = 56 * 1024 * 1024

RET_HEADS = 8
RET_DK = 64
RET_WIDTH = 512
RET_THETA_BASE = 10000.0
DIFF_HEADS = 4
DIFF_DQK = 64
DIFF_DV = 128
DIFF_WIDTH = 512
XATTN_HEADS = 4
SEG = 512
N_SEG = 7
LOG2E = math.log2(math.e)
F32_SUBLANES = 8
BF16_SUBLANES = 16
VT_ROWS = DIFF_DV + BF16_SUBLANES


def _rms(x, axis=-1):
    return lax.rsqrt(jnp.mean(x * x, axis=axis, keepdims=True) + EPS)


def _low_half_mask(shape):
    lane = lax.broadcasted_iota(jnp.int32, shape, len(shape) - 1)
    return (lane % LANES) < (LANES // 2)


def _in_proj_kernel(x_ref, gain_ref, w_ref, cos_ref, sin_ref, qg_ref, kg_ref,
                    rq_ref, rk_ref, rv_ref, rg_ref, dq_ref, dk_ref, dvt_ref):
    x = x_ref[...]
    h = (x * _rms(x) * gain_ref[...]).astype(BF16)

    def proj(seg):
        return jnp.dot(h, w_ref[:, seg * SEG:(seg + 1) * SEG], preferred_element_type=F32)

    cos = cos_ref[...]
    sin = sin_ref[...]
    low = _low_half_mask(cos.shape)
    first32 = (lax.broadcasted_iota(jnp.int32, cos.shape, 1) % 64) < 32

    def rotary_store(y, out_ref, scale):
        for j in range(SEG // LANES):
            ys = y[:, j * LANES:(j + 1) * LANES]
            partner = jnp.where(first32, pltpu.roll(ys, LANES - 32, 1), pltpu.roll(ys, 32, 1))
            r = ys * cos + partner * sin
            if scale != 1.0:
                r = r * scale
            out_ref[:, j * LANES:(j + 1) * LANES] = r.astype(out_ref.dtype)

    def qknorm_store(y, g_ref, out_ref, scale):
        g = g_ref[...]
        for j in range(SEG // LANES):
            ys = y[:, j * LANES:(j + 1) * LANES]
            sq = ys * ys
            s_lo = jnp.sum(jnp.where(low, sq, 0.0), axis=-1, keepdims=True)
            s_hi = jnp.sum(jnp.where(low, 0.0, sq), axis=-1, keepdims=True)
            inv = jnp.where(low, lax.rsqrt(s_lo * (1.0 / DIFF_DQK) + EPS),
                            lax.rsqrt(s_hi * (1.0 / DIFF_DQK) + EPS))
            r = ys * inv * g
            if scale != 1.0:
                r = r * scale
            out_ref[:, j * LANES:(j + 1) * LANES] = r.astype(out_ref.dtype)

    rotary_store(proj(0), rq_ref, 1.0)
    rotary_store(proj(1), rk_ref, RET_DK ** -0.5)
    rv_ref[...] = proj(2).astype(BF16)
    g = proj(3)
    rg_ref[...] = (g * jax.nn.sigmoid(g)).astype(BF16)
    qknorm_store(proj(4), qg_ref, dq_ref, DIFF_DQK ** -0.5 * LOG2E)
    qknorm_store(proj(5), kg_ref, dk_ref, 1.0)
    dv = proj(6)
    ones = jnp.ones((VT_ROWS - DIFF_DV, dv.shape[0]), BF16)
    for hh in range(DIFF_HEADS):
        dvt_ref[hh * VT_ROWS:hh * VT_ROWS + DIFF_DV, :] = (
            dv[:, hh * DIFF_DV:(hh + 1) * DIFF_DV].T.astype(BF16))
        dvt_ref[hh * VT_ROWS + DIFF_DV:(hh + 1) * VT_ROWS, :] = ones


def _in_proj(x2, gain, w_in, cos_t, sin_t, qg, kg, seq, tm):
    T, D = x2.shape
    n_pos_blocks = seq // tm
    tok = lambda i: (i, 0)
    const = lambda i: (0, 0)
    pos = lambda i: (i % n_pos_blocks, 0)
    out = jax.ShapeDtypeStruct((T, SEG), BF16)
    return pl.pallas_call(
        _in_proj_kernel,
        grid=(T // tm,),
        in_specs=[
            pl.BlockSpec((tm, D), tok),
            pl.BlockSpec((1, D), const),
            pl.BlockSpec((D, N_SEG * SEG), const),
            pl.BlockSpec((tm, LANES), pos),
            pl.BlockSpec((tm, LANES), pos),
            pl.BlockSpec((1, LANES), const),
            pl.BlockSpec((1, LANES), const),
        ],
        out_specs=[pl.BlockSpec((tm, SEG), tok)] * (N_SEG - 1)
        + [pl.BlockSpec((DIFF_HEADS * VT_ROWS, tm), lambda i: (0, i))],
        out_shape=[out] * (N_SEG - 1)
        + [jax.ShapeDtypeStruct((DIFF_HEADS * VT_ROWS, T), BF16)],
        compiler_params=pltpu.CompilerParams(
            dimension_semantics=("parallel",), vmem_limit_bytes=VMEM_LIMIT),
        name="in_proj",
    )(x2, gain, w_in, cos_t, sin_t, qg, kg)


def _retention_kernel(q_ref, k_ref, v_ref, g_ref, dmask_ref, zeta_ref, xi_ref, gchunk_ref,
                      gs_ref, o_ref, state_ref, *, chunk, n_chunks):
    @pl.when(pl.program_id(1) == 0)
    def _():
        state_ref[...] = jnp.zeros_like(state_ref)

    n_slabs = RET_WIDTH // LANES
    low = _low_half_mask((chunk, LANES))
    row = lax.broadcasted_iota(jnp.int32, (LANES, LANES), 0)
    col = lax.broadcasted_iota(jnp.int32, (LANES, LANES), 1)
    blockdiag = (row < 64) == (col < 64)
    nt = (((1,), (1,)), ((), ()))
    tn = (((0,), (0,)), ((), ()))

    for s in range(n_slabs):
        sl = slice(s * LANES, (s + 1) * LANES)
        zeta = zeta_ref[:, sl]
        xi = xi_ref[:, sl]
        gch = gchunk_ref[:, sl]
        for c in range(n_chunks):
            rows = slice(c * chunk, (c + 1) * chunk)
            q = q_ref[rows, sl]
            k = k_ref[rows, sl]
            v = v_ref[rows, sl]
            zero = jnp.zeros_like(q)
            q_lo = jnp.where(low, q, zero)
            q_hi = jnp.where(low, zero, q)
            s_lo = lax.dot_general(q_lo, k, nt, preferred_element_type=F32) * dmask_ref[2 * s]
            s_hi = lax.dot_general(q_hi, k, nt, preferred_element_type=F32) * dmask_ref[2 * s + 1]
            o_lo = jnp.dot(s_lo.astype(BF16), v, preferred_element_type=F32)
            o_hi = jnp.dot(s_hi.astype(BF16), v, preferred_element_type=F32)
            r_prev = state_ref[s]
            o_inter = jnp.dot(q, r_prev.astype(BF16), preferred_element_type=F32) * xi
            o = jnp.where(low, o_lo, o_hi) + o_inter
            kz = (k.astype(F32) * zeta).astype(BF16)
            kv = lax.dot_general(kz, v, tn, preferred_element_type=F32)
            state_ref[s] = r_prev * gch + jnp.where(blockdiag, kv, 0.0)
            sq = o * o
            m_lo = jnp.sum(jnp.where(low, sq, 0.0), axis=-1, keepdims=True)
            m_hi = jnp.sum(jnp.where(low, 0.0, sq), axis=-1, keepdims=True)
            inv = jnp.where(low, lax.rsqrt(m_lo * (1.0 / 64) + EPS),
                            lax.rsqrt(m_hi * (1.0 / 64) + EPS))
            out = o * inv * g_ref[rows, sl].astype(F32) * gs_ref[:, sl]
            o_ref[rows, sl] = out.astype(o_ref.dtype)


def _retention(rq, rk, rv, rg, dmask, zeta, xi, gchunk, gs_ret, batch, seq, ts, chunk):
    T = rq.shape[0]
    nblk = seq // ts
    tok = lambda b, i: (b * nblk + i, 0)
    c2 = lambda b, i: (0, 0)
    c3 = lambda b, i: (0, 0, 0)
    kern = functools.partial(_retention_kernel, chunk=chunk, n_chunks=ts // chunk)
    return pl.pallas_call(
        kern,
        grid=(batch, nblk),
        in_specs=[pl.BlockSpec((ts, RET_WIDTH), tok)] * 4 + [
            pl.BlockSpec((RET_HEADS, chunk, chunk), c3),
            pl.BlockSpec((chunk, RET_WIDTH), c2),
            pl.BlockSpec((chunk, RET_WIDTH), c2),
            pl.BlockSpec((1, RET_WIDTH), c2),
            pl.BlockSpec((1, RET_WIDTH), c2),
        ],
        out_specs=pl.BlockSpec((ts, RET_WIDTH), tok),
        out_shape=jax.ShapeDtypeStruct((T, RET_WIDTH), BF16),
        scratch_shapes=[pltpu.VMEM((RET_WIDTH // LANES, LANES, LANES), F32)],
        compiler_params=pltpu.CompilerParams(
            dimension_semantics=("parallel", "arbitrary"), vmem_limit_bytes=VMEM_LIMIT),
        name="retention",
    )(rq, rk, rv, rg, dmask, zeta, xi, gchunk, gs_ret)


def _diff_attn_kernel(q_ref, k_ref, vt_ref, lam_ref, subln_ref, gs_ref, o_ref,
                      qs_ref, s_ref, m_ref, acc_ref, *, tq, tk, rb, lam_init):
    i = pl.program_id(2)
    q = q_ref[...]
    low = _low_half_mask(q.shape)
    zero = jnp.zeros_like(q)
    qs_ref[:tq, :] = jnp.where(low, q, zero)
    qs_ref[tq:, :] = jnp.where(low, zero, q)
    nt = (((1,), (1,)), ((), ()))

    m_ref[...] = jnp.full_like(m_ref, -jnp.inf)
    acc_ref[...] = jnp.zeros_like(acc_ref)

    n_blocks = 2 * tq // rb
    n_diag = tq // tk
    n_full = i * n_diag

    def scores(j, b):
        start = pl.multiple_of(j * tk, tk)
        cols = slice(b * rb, (b + 1) * rb)
        s_ref[:, cols] = lax.dot_general(
            k_ref[pl.ds(start, tk), :], qs_ref[cols, :], nt, preferred_element_type=F32)

    def consume(j, b, key_offset):
        start = pl.multiple_of(j * tk, tk)
        cols = slice(b * rb, (b + 1) * rb)
        s = s_ref[:, cols]
        if key_offset is not None:
            kpos = lax.broadcasted_iota(jnp.int32, s.shape, 0) + key_offset
            qpos = lax.broadcasted_iota(jnp.int32, s.shape, 1) + (b * rb) % tq
            s = jnp.where(kpos <= qpos, s, -jnp.inf)
        m_prev = m_ref[:, cols]
        m_new = jnp.maximum(m_prev, jnp.max(s, axis=0, keepdims=True))
        alpha = jnp.exp2(m_prev - m_new)
        p = jnp.exp2(s - m_new[0:1, :])
        pv = jnp.dot(vt_ref[:, pl.ds(start, tk)], p.astype(BF16),
                     preferred_element_type=F32)
        acc_ref[:, cols] = alpha[0:1, :] * acc_ref[:, cols] + pv
        m_ref[:, cols] = m_new

    def diag_kind(d, b):
        q_first = (b * rb) % tq
        if d * tk > q_first + rb - 1:
            return "skip"
        return "full" if (d + 1) * tk - 1 <= q_first else "mask"

    ahead = n_blocks // 2
    for b in range(ahead):
        scores(0, b)

    def body(jj, carry):
        for u in range(n_diag):
            j = jj * n_diag + u
            for b in range(n_blocks):
                if b + ahead < n_blocks:
                    scores(j, b + ahead)
                else:
                    scores(j + 1, b + ahead - n_blocks)
                consume(j, b, None)
        return carry

    lax.fori_loop(0, i, body, 0)
    for d in range(n_diag):
        for b in range(n_blocks):
            nd, nb = (d, b + ahead) if b + ahead < n_blocks else (d + 1, b + ahead - n_blocks)
            if nd < n_diag and diag_kind(nd, nb) != "skip":
                scores(n_full + nd, nb)
            kind = diag_kind(d, b)
            if kind != "skip":
                consume(n_full + d, b, d * tk if kind == "mask" else None)

    lv = lam_ref[...]
    lam = (jnp.exp(jnp.sum(lv[0:1] * lv[1:2], axis=-1, keepdims=True))
           - jnp.exp(jnp.sum(lv[2:3] * lv[3:4], axis=-1, keepdims=True)) + lam_init)
    acc = acc_ref[...]
    o_all = acc[:DIFF_DV, :] * (1.0 / acc[DIFF_DV:DIFF_DV + 1, :])
    o = (o_all[:, :tq] - lam * o_all[:, tq:]).T
    y = o * _rms(o) * subln_ref[...] * (1.0 - lam_init)
    o_ref[...] = (y * gs_ref[...]).astype(o_ref.dtype)


def _diff_attn(dq, dk, dvt, lam_p, subln, gs_diff, batch, seq, tq, tk, rb, lam_init):
    T = dq.shape[0]
    nq = seq // tq
    assert seq % tq == 0 and tq % tk == 0 and tq % rb == 0
    kern = functools.partial(_diff_attn_kernel, tq=tq, tk=tk, rb=rb, lam_init=lam_init)
    return pl.pallas_call(
        kern,
        grid=(batch, DIFF_HEADS, nq),
        in_specs=[
            pl.BlockSpec((tq, LANES), lambda b, h, i: (b * nq + i, h)),
            pl.BlockSpec((seq, LANES), lambda b, h, i: (b, h)),
            pl.BlockSpec((VT_ROWS, seq), lambda b, h, i: (h, b)),
            pl.BlockSpec((4, DIFF_DQK), lambda b, h, i: (0, 0)),
            pl.BlockSpec((1, LANES), lambda b, h, i: (0, 0)),
            pl.BlockSpec((1, LANES), lambda b, h, i: (0, h)),
        ],
        out_specs=pl.BlockSpec((tq, LANES), lambda b, h, i: (b * nq + i, h)),
        out_shape=jax.ShapeDtypeStruct((T, DIFF_WIDTH), BF16),
        scratch_shapes=[pltpu.VMEM((2 * tq, LANES), BF16), pltpu.VMEM((tk, 2 * tq), F32),
                        pltpu.VMEM((F32_SUBLANES, 2 * tq), F32),
                        pltpu.VMEM((VT_ROWS, 2 * tq), F32)],
        compiler_params=pltpu.CompilerParams(
            dimension_semantics=("parallel", "parallel", "arbitrary"),
            vmem_limit_bytes=VMEM_LIMIT),
        name="diff_attn",
    )(dq, dk, dvt, lam_p, subln, gs_diff)


def _mem_kv_kernel(mem_ref, gain_ref, w_ref, kg_ref, k_ref, v_ref):
    m = mem_ref[...]
    h = (m * _rms(m) * gain_ref[...]).astype(BF16)
    D = m.shape[1]
    dh = D // XATTN_HEADS
    k = jnp.dot(h, w_ref[:, :D], preferred_element_type=F32)
    v_ref[...] = jnp.dot(h, w_ref[:, D:], preferred_element_type=F32).astype(v_ref.dtype)
    for hh in range(XATTN_HEADS):
        kh = k[:, hh * dh:(hh + 1) * dh]
        k_ref[:, hh * dh:(hh + 1) * dh] = (kh * _rms(kh) * kg_ref[...]).astype(k_ref.dtype)


def _mem_kv(mem2, gain, xkv, kg, batch, mem_len):
    BM, D = mem2.shape
    const = lambda b: (0, 0)
    out = jax.ShapeDtypeStruct((BM, D), BF16)
    return pl.pallas_call(
        _mem_kv_kernel,
        grid=(batch,),
        in_specs=[
            pl.BlockSpec((mem_len, D), lambda b: (b, 0)),
            pl.BlockSpec((1, D), const),
            pl.BlockSpec((D, 2 * D), const),
            pl.BlockSpec((1, D // XATTN_HEADS), const),
        ],
        out_specs=[pl.BlockSpec((mem_len, D), lambda b: (b, 0))] * 2,
        out_shape=[out, out],
        compiler_params=pltpu.CompilerParams(
            dimension_semantics=("parallel",), vmem_limit_bytes=VMEM_LIMIT),
        name="mem_kv",
    )(mem2, gain, xkv, kg)


def _mix_xattn_kernel(x_ref, oret_ref, odiff_ref, wout_ref, nx_ref, xq_ref, qg_ref,
                      kmem_ref, vmem_ref, xo_ref, out_ref):
    D = x_ref.shape[1]
    dh = D // XATTN_HEADS
    x1 = (x_ref[...]
          + jnp.dot(oret_ref[...], wout_ref[:RET_WIDTH, :], preferred_element_type=F32)
          + jnp.dot(odiff_ref[...], wout_ref[RET_WIDTH:, :], preferred_element_type=F32))
    h = (x1 * _rms(x1) * nx_ref[...]).astype(BF16)
    q = jnp.dot(h, xq_ref[...], preferred_element_type=F32)
    nt = (((1,), (1,)), ((), ()))
    outs = []
    for hh in range(XATTN_HEADS):
        sl = slice(hh * dh, (hh + 1) * dh)
        qh = q[:, sl]
        qn = (qh * _rms(qh) * qg_ref[...] * (dh ** -0.5)).astype(BF16)
        s = lax.dot_general(qn, kmem_ref[:, sl], nt, preferred_element_type=F32)
        e = jnp.exp(s - jnp.max(s, axis=-1, keepdims=True))
        p = e * (1.0 / jnp.sum(e, axis=-1, keepdims=True))
        outs.append(jnp.dot(p.astype(BF16), vmem_ref[:, sl],
                            preferred_element_type=F32).astype(BF16))
    o = jnp.concatenate(outs, axis=1)
    out_ref[...] = x1 + jnp.dot(o, xo_ref[...], preferred_element_type=F32)


def _mix_xattn(x2, o_ret, o_diff, w_out, nx, xq, qg, kmem, vmem, xo, seq, mem_len, tm):
    T, D = x2.shape
    nblk = seq // tm
    tok = lambda i: (i, 0)
    const = lambda i: (0, 0)
    memb = lambda i: (i // nblk, 0)
    return pl.pallas_call(
        _mix_xattn_kernel,
        grid=(T // tm,),
        in_specs=[
            pl.BlockSpec((tm, D), tok),
            pl.BlockSpec((tm, RET_WIDTH), tok),
            pl.BlockSpec((tm, DIFF_WIDTH), tok),
            pl.BlockSpec((D, D), const),
            pl.BlockSpec((1, D), const),
            pl.BlockSpec((D, D), const),
            pl.BlockSpec((1, D // XATTN_HEADS), const),
            pl.BlockSpec((mem_len, D), memb),
            pl.BlockSpec((mem_len, D), memb),
            pl.BlockSpec((D, D), const),
        ],
        out_specs=pl.BlockSpec((tm, D), tok),
        out_shape=jax.ShapeDtypeStruct((T, D), F32),
        compiler_params=pltpu.CompilerParams(
            dimension_semantics=("parallel",), vmem_limit_bytes=VMEM_LIMIT),
        name="mix_xattn",
    )(x2, o_ret, o_diff, w_out, nx, xq, qg, kmem, vmem, xo)


def _ffn_kernel(x_ref, gain_ref, wg_ref, wu_ref, wd_ref, out_ref, *, hidden_chunk):
    x = x_ref[...]
    h = (x * _rms(x) * gain_ref[...]).astype(BF16)
    hidden = wg_ref.shape[1]
    acc = x
    for c in range(hidden // hidden_chunk):
        sl = slice(c * hidden_chunk, (c + 1) * hidden_chunk)
        g = jnp.dot(h, wg_ref[:, sl], preferred_element_type=F32)
        u = jnp.dot(h, wu_ref[:, sl], preferred_element_type=F32)
        a = (g * jax.nn.sigmoid(g) * u).astype(BF16)
        acc = acc + jnp.dot(a, wd_ref[sl, :], preferred_element_type=F32)
    out_ref[...] = acc


def _ffn(x2, gain, wg, wu, wd, tm, hidden_chunk):
    T, D = x2.shape
    hidden = wg.shape[1]
    tok = lambda i: (i, 0)
    const = lambda i: (0, 0)
    kern = functools.partial(_ffn_kernel, hidden_chunk=hidden_chunk)
    return pl.pallas_call(
        kern,
        grid=(T // tm,),
        in_specs=[
            pl.BlockSpec((tm, D), tok),
            pl.BlockSpec((1, D), const),
            pl.BlockSpec((D, hidden), const),
            pl.BlockSpec((D, hidden), const),
            pl.BlockSpec((hidden, D), const),
        ],
        out_specs=pl.BlockSpec((tm, D), tok),
        out_shape=jax.ShapeDtypeStruct((T, D), F32),
        compiler_params=pltpu.CompilerParams(
            dimension_semantics=("parallel",), vmem_limit_bytes=VMEM_LIMIT),
        name="ffn",
    )(x2, gain, wg, wu, wd)


def _rotary_tables(seq):
    half = RET_DK // 2
    inv = 1.0 / (RET_THETA_BASE ** jnp.linspace(0.0, 1.0, half, dtype=F32))
    ang = jnp.arange(seq).astype(F32)[:, None] * inv[None, :]
    cos, sin = jnp.cos(ang), jnp.sin(ang)
    cos_t = jnp.tile(cos, (1, LANES // half))
    sin_t = jnp.tile(jnp.concatenate([-sin, sin], axis=1), (1, LANES // RET_DK))
    return cos_t, sin_t


def _decay_tables(chunk):
    H = RET_HEADS
    log_g = jnp.log1p(-(2.0 ** (-5.0 - jnp.arange(H, dtype=F32))))
    idx = jnp.arange(chunk, dtype=F32)
    rel = idx[:, None] - idx[None, :]
    dmask = jnp.where(rel >= 0, jnp.exp(log_g[:, None, None] * jnp.maximum(rel, 0.0)), 0.0)
    zeta = jnp.exp(log_g[:, None] * (chunk - 1.0 - idx))
    xi = jnp.exp(log_g[:, None] * (idx + 1.0))
    g_chunk = jnp.exp(log_g * chunk)
    widen = lambda t: jnp.repeat(t.T, RET_DK, axis=1)
    return dmask, widen(zeta), widen(xi), jnp.repeat(g_chunk, RET_DK)[None, :]


def kernel(x, mem, norm_mix, w_in, diff_q_gain, diff_k_gain, diff_lambda, diff_subln,
           group_scale, w_out, norm_x, norm_mem, xq, xkv, xq_gain, xk_gain, xo,
           norm_ffn, w_gate, w_up, w_down):
    B, S, D = x.shape
    M = mem.shape[1]
    depth = norm_mix.shape[0]
    x2 = x.reshape(B * S, D)
    mem2 = mem.reshape(B * M, D)

    ret_chunk = 256
    cos_t, sin_t = _rotary_tables(S)
    dmask, zeta, xi, gchunk = _decay_tables(ret_chunk)
    row = lambda a: a.reshape(1, -1)
    twice = lambda a: jnp.tile(a.reshape(1, -1), (1, 2))

    for l in range(depth):
        lam_init = 0.8 - 0.6 * math.exp(-0.3 * l)
        rq, rk, rv, rg, dq, dk, dv = _in_proj(
            x2, row(norm_mix[l]), w_in[l].astype(BF16), cos_t, sin_t,
            twice(diff_q_gain[l]), twice(diff_k_gain[l]), S, tm=512)
        o_ret = _retention(rq, rk, rv, rg, dmask, zeta, xi, gchunk,
                           row(group_scale[l, :RET_WIDTH]), B, S, ts=512, chunk=ret_chunk)
        o_diff = _diff_attn(dq, dk, dv, diff_lambda[l], row(diff_subln[l]),
                            row(group_scale[l, RET_WIDTH:]), B, S, tq=1024, tk=512, rb=256,
                            lam_init=lam_init)
        kmem, vmem = _mem_kv(mem2, row(norm_mem[l]), xkv[l].astype(BF16), row(xk_gain[l]), B, M)
        x2 = _mix_xattn(x2, o_ret, o_diff, w_out[l].astype(BF16), row(norm_x[l]),
                        xq[l].astype(BF16), row(xq_gain[l]), kmem, vmem, xo[l].astype(BF16),
                        S, M, tm=512)
        x2 = _ffn(x2, row(norm_ffn[l]), w_gate[l].astype(BF16), w_up[l].astype(BF16),
                  w_down[l].astype(BF16), tm=512, hidden_chunk=256)
    return x2.reshape(B, S, D)
```

```python
import functools
import math

import jax
import jax.numpy as jnp
import numpy as np
from jax import lax
from jax.experimental import pallas as pl
from jax.experimental.pallas import tpu as pltpu

F32 = jnp.float32
BF16 = jnp.bfloat16

EPS = 1e-6
LANES = 128
VMEM_LIMIT = 56 * 1024 * 1024

RET_HEADS = 8
RET_DK = 64
RET_WIDTH = 512
RET_THETA_BASE = 10000.0
DIFF_HEADS = 4
DIFF_DQK = 64
DIFF_DV = 128
DIFF_WIDTH = 512
XATTN_HEADS = 4
SEG = 512
N_SEG = 7
LOG2E = math.log2(math.e)
F32_SUBLANES = 8
BF16_SUBLANES = 16
VT_ROWS = DIFF_DV + BF16_SUBLANES


def _rms(x, axis=-1):
    return lax.rsqrt(jnp.mean(x * x, axis=axis, keepdims=True) + EPS)


def _low_half_mask(shape):
    lane = lax.broadcasted_iota(jnp.int32, shape, len(shape) - 1)
    return (lane % LANES) < (LANES // 2)


def _in_proj_kernel(x_ref, gain_ref, w_ref, cos_ref, sin_ref, qg_ref, kg_ref,
                    rq_ref, rk_ref, rv_ref, rg_ref, dq_ref, dk_ref, dvt_ref):
    x = x_ref[...]
    h = (x * _rms(x) * gain_ref[...]).astype(BF16)

    def proj(seg):
        return jnp.dot(h, w_ref[:, seg * SEG:(seg + 1) * SEG], preferred_element_type=F32)

    cos = cos_ref[...]
    sin = sin_ref[...]
    low = _low_half_mask(cos.shape)
    first32 = (lax.broadcasted_iota(jnp.int32, cos.shape, 1) % 64) < 32

    def rotary_store(y, out_ref, scale):
        for j in range(SEG // LANES):
            ys = y[:, j * LANES:(j + 1) * LANES]
            partner = jnp.where(first32, pltpu.roll(ys, LANES - 32, 1), pltpu.roll(ys, 32, 1))
            r = ys * cos + partner * sin
            if scale != 1.0:
                r = r * scale
            out_ref[:, j * LANES:(j + 1) * LANES] = r.astype(out_ref.dtype)

    def qknorm_store(y, g_ref, out_ref, scale):
        g = g_ref[...]
        for j in range(SEG // LANES):
            ys = y[:, j * LANES:(j + 1) * LANES]
            sq = ys * ys
            s_lo = jnp.sum(jnp.where(low, sq, 0.0), axis=-1, keepdims=True)
            s_hi = jnp.sum(jnp.where(low, 0.0, sq), axis=-1, keepdims=True)
            inv = jnp.where(low, lax.rsqrt(s_lo * (1.0 / DIFF_DQK) + EPS),
                            lax.rsqrt(s_hi * (1.0 / DIFF_DQK) + EPS))
            r = ys * inv * g
            if scale != 1.0:
                r = r * scale
            out_ref[:, j * LANES:(j + 1) * LANES] = r.astype(out_ref.dtype)

    rotary_store(proj(0), rq_ref, 1.0)
    rotary_store(proj(1), rk_ref, RET_DK ** -0.5)
    rv_ref[...] = proj(2).astype(BF16)
    g = proj(3)
    rg_ref[...] = (g * jax.nn.sigmoid(g)).astype(BF16)
    qknorm_store(proj(4), qg_ref, dq_ref, DIFF_DQK ** -0.5 * LOG2E)
    qknorm_store(proj(5), kg_ref, dk_ref, 1.0)
    dv = proj(6)
    ones = jnp.ones((VT_ROWS - DIFF_DV, dv.shape[0]), BF16)
    for hh in range(DIFF_HEADS):
        dvt_ref[hh * VT_ROWS:hh * VT_ROWS + DIFF_DV, :] = (
            dv[:, hh * DIFF_DV:(hh + 1) * DIFF_DV].T.astype(BF16))
        dvt_ref[hh * VT_ROWS + DIFF_DV:(hh + 1) * VT_ROWS, :] = ones


def _in_proj(x2, gain, w_in, cos_t, sin_t, qg, kg, seq, tm):
    T, D = x2.shape
    n_pos_blocks = seq // tm
    tok = lambda i: (i, 0)
    const = lambda i: (0, 0)
    pos = lambda i: (i % n_pos_blocks, 0)
    out = jax.ShapeDtypeStruct((T, SEG), BF16)
    return pl.pallas_call(
        _in_proj_kernel,
        grid=(T // tm,),
        in_specs=[
            pl.BlockSpec((tm, D), tok),
            pl.BlockSpec((1, D), const),
            pl.BlockSpec((D, N_SEG * SEG), const),
            pl.BlockSpec((tm, LANES), pos),
            pl.BlockSpec((tm, LANES), pos),
            pl.BlockSpec((1, LANES), const),
            pl.BlockSpec((1, LANES), const),
        ],
        out_specs=[pl.BlockSpec((tm, SEG), tok)] * (N_SEG - 1)
        + [pl.BlockSpec((DIFF_HEADS * VT_ROWS, tm), lambda i: (0, i))],
        out_shape=[out] * (N_SEG - 1)
        + [jax.ShapeDtypeStruct((DIFF_HEADS * VT_ROWS, T), BF16)],
        compiler_params=pltpu.CompilerParams(
            dimension_semantics=("parallel",), vmem_limit_bytes=VMEM_LIMIT),
        name="in_proj",
    )(x2, gain, w_in, cos_t, sin_t, qg, kg)


def _retention_kernel(q_ref, k_ref, v_ref, g_ref, dmask_ref, zeta_ref, xi_ref, gchunk_ref,
                      gs_ref, o_ref, state_ref, *, chunk, n_chunks):
    @pl.when(pl.program_id(1) == 0)
    def _():
        state_ref[...] = jnp.zeros_like(state_ref)

    n_slabs = RET_WIDTH // LANES
    low = _low_half_mask((chunk, LANES))
    row = lax.broadcasted_iota(jnp.int32, (LANES, LANES), 0)
    col = lax.broadcasted_iota(jnp.int32, (LANES, LANES), 1)
    blockdiag = (row < 64) == (col < 64)
    nt = (((1,), (1,)), ((), ()))
    tn = (((0,), (0,)), ((), ()))

    for s in range(n_slabs):
        sl = slice(s * LANES, (s + 1) * LANES)
        zeta = zeta_ref[:, sl]
        xi = xi_ref[:, sl]
        gch = gchunk_ref[:, sl]
        for c in range(n_chunks):
            rows = slice(c * chunk, (c + 1) * chunk)
            q = q_ref[rows, sl]
            k = k_ref[rows, sl]
            v = v_ref[rows, sl]
            zero = jnp.zeros_like(q)
            q_lo = jnp.where(low, q, zero)
            q_hi = jnp.where(low, zero, q)
            s_lo = lax.dot_general(q_lo, k, nt, preferred_element_type=F32) * dmask_ref[2 * s]
            s_hi = lax.dot_general(q_hi, k, nt, preferred_element_type=F32) * dmask_ref[2 * s + 1]
            o_lo = jnp.dot(s_lo.astype(BF16), v, preferred_element_type=F32)
            o_hi = jnp.dot(s_hi.astype(BF16), v, preferred_element_type=F32)
            r_prev = state_ref[s]
            o_inter = jnp.dot(q, r_prev.astype(BF16), preferred_element_type=F32) * xi
            o = jnp.where(low, o_lo, o_hi) + o_inter
            kz = (k.astype(F32) * zeta).astype(BF16)
            kv = lax.dot_general(kz, v, tn, preferred_element_type=F32)
            state_ref[s] = r_prev * gch + jnp.where(blockdiag, kv, 0.0)
            sq = o * o
            m_lo = jnp.sum(jnp.where(low, sq, 0.0), axis=-1, keepdims=True)
            m_hi = jnp.sum(jnp.where(low, 0.0, sq), axis=-1, keepdims=True)
            inv = jnp.where(low, lax.rsqrt(m_lo * (1.0 / 64) + EPS),
                            lax.rsqrt(m_hi * (1.0 / 64) + EPS))
            out = o * inv * g_ref[rows, sl].astype(F32) * gs_ref[:, sl]
            o_ref[rows, sl] = out.astype(o_ref.dtype)


def _retention(rq, rk, rv, rg, dmask, zeta, xi, gchunk, gs_ret, batch, seq, ts, chunk):
    T = rq.shape[0]
    nblk = seq // ts
    tok = lambda b, i: (b * nblk + i, 0)
    c2 = lambda b, i: (0, 0)
    c3 = lambda b, i: (0, 0, 0)
    kern = functools.partial(_retention_kernel, chunk=chunk, n_chunks=ts // chunk)
    return pl.pallas_call(
        kern,
        grid=(batch, nblk),
        in_specs=[pl.BlockSpec((ts, RET_WIDTH), tok)] * 4 + [
            pl.BlockSpec((RET_HEADS, chunk, chunk), c3),
            pl.BlockSpec((chunk, RET_WIDTH), c2),
            pl.BlockSpec((chunk, RET_WIDTH), c2),
            pl.BlockSpec((1, RET_WIDTH), c2),
            pl.BlockSpec((1, RET_WIDTH), c2),
        ],
        out_specs=pl.BlockSpec((ts, RET_WIDTH), tok),
        out_shape=jax.ShapeDtypeStruct((T, RET_WIDTH), BF16),
        scratch_shapes=[pltpu.VMEM((RET_WIDTH // LANES, LANES, LANES), F32)],
        compiler_params=pltpu.CompilerParams(
            dimension_semantics=("parallel", "arbitrary"), vmem_limit_bytes=VMEM_LIMIT),
        name="retention",
    )(rq, rk, rv, rg, dmask, zeta, xi, gchunk, gs_ret)


def _diff_attn_kernel(q_ref, k_ref, vt_ref, lam_ref, subln_ref, gs_ref, o_ref,
                      qs_ref, s_ref, m_ref, acc_ref, *, tq, tk, rb, lam_init):
    i = pl.program_id(2)
    q = q_ref[...]
    low = _low_half_mask(q.shape)
    zero = jnp.zeros_like(q)
    qs_ref[:tq, :] = jnp.where(low, q, zero)
    qs_ref[tq:, :] = jnp.where(low, zero, q)
    nt = (((1,), (1,)), ((), ()))

    m_ref[...] = jnp.full_like(m_ref, -jnp.inf)
    acc_ref[...] = jnp.zeros_like(acc_ref)

    n_blocks = 2 * tq // rb
    n_diag = tq // tk
    n_full = i * n_diag

    def scores(j, b):
        start = pl.multiple_of(j * tk, tk)
        cols = slice(b * rb, (b + 1) * rb)
        s_ref[:, cols] = lax.dot_general(
            k_ref[pl.ds(start, tk), :], qs_ref[cols, :], nt, preferred_element_type=F32)

    def consume(j, b, key_offset):
        start = pl.multiple_of(j * tk, tk)
        cols = slice(b * rb, (b + 1) * rb)
        s = s_ref[:, cols]
        if key_offset is not None:
            kpos = lax.broadcasted_iota(jnp.int32, s.shape, 0) + key_offset
            qpos = lax.broadcasted_iota(jnp.int32, s.shape, 1) + (b * rb) % tq
            s = jnp.where(kpos <= qpos, s, -jnp.inf)
        m_prev = m_ref[:, cols]
        m_new = jnp.maximum(m_prev, jnp.max(s, axis=0, keepdims=True))
        alpha = jnp.exp2(m_prev - m_new)
        p = jnp.exp2(s - m_new[0:1, :])
        pv = jnp.dot(vt_ref[:, pl.ds(start, tk)], p.astype(BF16),
                     preferred_element_type=F32)
        acc_ref[:, cols] = alpha[0:1, :] * acc_ref[:, cols] + pv
        m_ref[:, cols] = m_new

    def diag_kind(d, b):
        q_first = (b * rb) % tq
        if d * tk > q_first + rb - 1:
            return "skip"
        return "full" if (d + 1) * tk - 1 <= q_first else "mask"

    ahead = n_blocks // 2
    for b in range(ahead):
        scores(0, b)

    def full_chunks(j0, count):
        for u in range(count):
            j = j0 + u
            for b in range(n_blocks):
                if b + ahead < n_blocks:
                    scores(j, b + ahead)
                else:
                    scores(j + 1, b + ahead - n_blocks)
                consume(j, b, None)

    def body(t, carry):
        full_chunks(t * (2 * n_diag), 2 * n_diag)
        return carry

    lax.fori_loop(0, i // 2, body, 0)

    @pl.when(i % 2 == 1)
    def _():
        full_chunks((i - 1) * n_diag, n_diag)

    for d in range(n_diag):
        for b in range(n_blocks):
            nd, nb = (d, b + ahead) if b + ahead < n_blocks else (d + 1, b + ahead - n_blocks)
            if nd < n_diag and diag_kind(nd, nb) != "skip":
                scores(n_full + nd, nb)
            kind = diag_kind(d, b)
            if kind != "skip":
                consume(n_full + d, b, d * tk if kind == "mask" else None)

    lv = lam_ref[...]
    lam = (jnp.exp(jnp.sum(lv[0:1] * lv[1:2], axis=-1, keepdims=True))
           - jnp.exp(jnp.sum(lv[2:3] * lv[3:4], axis=-1, keepdims=True)) + lam_init)
    acc = acc_ref[...]
    o_all = acc[:DIFF_DV, :] * (1.0 / acc[DIFF_DV:DIFF_DV + 1, :])
    o = (o_all[:, :tq] - lam * o_all[:, tq:]).T
    y = o * _rms(o) * subln_ref[...] * (1.0 - lam_init)
    o_ref[...] = (y * gs_ref[...]).astype(o_ref.dtype)


def _diff_attn(dq, dk, dvt, lam_p, subln, gs_diff, batch, seq, tq, tk, rb, lam_init):
    T = dq.shape[0]
    nq = seq // tq
    assert seq % tq == 0 and tq % tk == 0 and tq % rb == 0
    kern = functools.partial(_diff_attn_kernel, tq=tq, tk=tk, rb=rb, lam_init=lam_init)
    return pl.pallas_call(
        kern,
        grid=(batch, DIFF_HEADS, nq),
        in_specs=[
            pl.BlockSpec((tq, LANES), lambda b, h, i: (b * nq + i, h)),
            pl.BlockSpec((seq, LANES), lambda b, h, i: (b, h)),
            pl.BlockSpec((VT_ROWS, seq), lambda b, h, i: (h, b)),
            pl.BlockSpec((4, DIFF_DQK), lambda b, h, i: (0, 0)),
            pl.BlockSpec((1, LANES), lambda b, h, i: (0, 0)),
            pl.BlockSpec((1, LANES), lambda b, h, i: (0, h)),
        ],
        out_specs=pl.BlockSpec((tq, LANES), lambda b, h, i: (b * nq + i, h)),
        out_shape=jax.ShapeDtypeStruct((T, DIFF_WIDTH), BF16),
        scratch_shapes=[pltpu.VMEM((2 * tq, LANES), BF16), pltpu.VMEM((tk, 2 * tq), F32),
                        pltpu.VMEM((F32_SUBLANES, 2 * tq), F32),
                        pltpu.VMEM((VT_ROWS, 2 * tq), F32)],
        compiler_params=pltpu.CompilerParams(
            dimension_semantics=("parallel", "parallel", "arbitrary"),
            vmem_limit_bytes=VMEM_LIMIT),
        name="diff_attn",
    )(dq, dk, dvt, lam_p, subln, gs_diff)


def _mem_kv_kernel(mem_ref, gain_ref, w_ref, kg_ref, k_ref, v_ref):
    m = mem_ref[...]
    h = (m * _rms(m) * gain_ref[...]).astype(BF16)
    D = m.shape[1]
    dh = D // XATTN_HEADS
    k = jnp.dot(h, w_ref[:, :D], preferred_element_type=F32)
    v_ref[...] = jnp.dot(h, w_ref[:, D:], preferred_element_type=F32).astype(v_ref.dtype)
    for hh in range(XATTN_HEADS):
        kh = k[:, hh * dh:(hh + 1) * dh]
        k_ref[:, hh * dh:(hh + 1) * dh] = (kh * _rms(kh) * kg_ref[...]).astype(k_ref.dtype)


def _mem_kv(mem2, gain, xkv, kg, batch, mem_len):
    BM, D = mem2.shape
    const = lambda b: (0, 0)
    out = jax.ShapeDtypeStruct((BM, D), BF16)
    return pl.pallas_call(
        _mem_kv_kernel,
        grid=(batch,),
        in_specs=[
            pl.BlockSpec((mem_len, D), lambda b: (b, 0)),
            pl.BlockSpec((1, D), const),
            pl.BlockSpec((D, 2 * D), const),
            pl.BlockSpec((1, D // XATTN_HEADS), const),
        ],
        out_specs=[pl.BlockSpec((mem_len, D), lambda b: (b, 0))] * 2,
        out_shape=[out, out],
        compiler_params=pltpu.CompilerParams(
            dimension_semantics=("parallel",), vmem_limit_bytes=VMEM_LIMIT),
        name="mem_kv",
    )(mem2, gain, xkv, kg)


def _mix_xattn_kernel(x_ref, oret_ref, odiff_ref, wout_ref, nx_ref, xq_ref, qg_ref,
                      kmem_ref, vmem_ref, xo_ref, out_ref):
    D = x_ref.shape[1]
    dh = D // XATTN_HEADS
    x1 = (x_ref[...]
          + jnp.dot(oret_ref[...], wout_ref[:RET_WIDTH, :], preferred_element_type=F32)
          + jnp.dot(odiff_ref[...], wout_ref[RET_WIDTH:, :], preferred_element_type=F32))
    h = (x1 * _rms(x1) * nx_ref[...]).astype(BF16)
    q = jnp.dot(h, xq_ref[...], preferred_element_type=F32)
    nt = (((1,), (1,)), ((), ()))
    outs = []
    for hh in range(XATTN_HEADS):
        sl = slice(hh * dh, (hh + 1) * dh)
        qh = q[:, sl]
        qn = (qh * _rms(qh) * qg_ref[...] * (dh ** -0.5)).astype(BF16)
        s = lax.dot_general(qn, kmem_ref[:, sl], nt, preferred_element_type=F32)
        e = jnp.exp(s - jnp.max(s, axis=-1, keepdims=True))
        p = e * (1.0 / jnp.sum(e, axis=-1, keepdims=True))
        outs.append(jnp.dot(p.astype(BF16), vmem_ref[:, sl],
                            preferred_element_type=F32).astype(BF16))
    o = jnp.concatenate(outs, axis=1)
    out_ref[...] = x1 + jnp.dot(o, xo_ref[...], preferred_element_type=F32)


def _mix_xattn(x2, o_ret, o_diff, w_out, nx, xq, qg, kmem, vmem, xo, seq, mem_len, tm):
    T, D = x2.shape
    nblk = seq // tm
    tok = lambda i: (i, 0)
    const = lambda i: (0, 0)
    memb = lambda i: (i // nblk, 0)
    return pl.pallas_call(
        _mix_xattn_kernel,
        grid=(T // tm,),
        in_specs=[
            pl.BlockSpec((tm, D), tok),
            pl.BlockSpec((tm, RET_WIDTH), tok),
            pl.BlockSpec((tm, DIFF_WIDTH), tok),
            pl.BlockSpec((D, D), const),
            pl.BlockSpec((1, D), const),
            pl.BlockSpec((D, D), const),
            pl.BlockSpec((1, D // XATTN_HEADS), const),
            pl.BlockSpec((mem_len, D), memb),
            pl.BlockSpec((mem_len, D), memb),
            pl.BlockSpec((D, D), const),
        ],
        out_specs=pl.BlockSpec((tm, D), tok),
        out_shape=jax.ShapeDtypeStruct((T, D), F32),
        compiler_params=pltpu.CompilerParams(
            dimension_semantics=("parallel",), vmem_limit_bytes=VMEM_LIMIT),
        name="mix_xattn",
    )(x2, o_ret, o_diff, w_out, nx, xq, qg, kmem, vmem, xo)


def _ffn_kernel(x_ref, gain_ref, wg_ref, wu_ref, wd_ref, out_ref, *, hidden_chunk):
    x = x_ref[...]
    h = (x * _rms(x) * gain_ref[...]).astype(BF16)
    hidden = wg_ref.shape[1]
    acc = x
    for c in range(hidden // hidden_chunk):
        sl = slice(c * hidden_chunk, (c + 1) * hidden_chunk)
        g = jnp.dot(h, wg_ref[:, sl], preferred_element_type=F32)
        u = jnp.dot(h, wu_ref[:, sl], preferred_element_type=F32)
        a = (g * jax.nn.sigmoid(g) * u).astype(BF16)
        acc = acc + jnp.dot(a, wd_ref[sl, :], preferred_element_type=F32)
    out_ref[...] = acc


def _ffn(x2, gain, wg, wu, wd, tm, hidden_chunk):
    T, D = x2.shape
    hidden = wg.shape[1]
    tok = lambda i: (i, 0)
    const = lambda i: (0, 0)
    kern = functools.partial(_ffn_kernel, hidden_chunk=hidden_chunk)
    return pl.pallas_call(
        kern,
        grid=(T // tm,),
        in_specs=[
            pl.BlockSpec((tm, D), tok),
            pl.BlockSpec((1, D), const),
            pl.BlockSpec((D, hidden), const),
            pl.BlockSpec((D, hidden), const),
            pl.BlockSpec((hidden, D), const),
        ],
        out_specs=pl.BlockSpec((tm, D), tok),
        out_shape=jax.ShapeDtypeStruct((T, D), F32),
        compiler_params=pltpu.CompilerParams(
            dimension_semantics=("parallel",), vmem_limit_bytes=VMEM_LIMIT),
        name="ffn",
    )(x2, gain, wg, wu, wd)


def _rotary_tables(seq):
    half = RET_DK // 2
    inv = 1.0 / (RET_THETA_BASE ** np.linspace(0.0, 1.0, half))
    ang = np.arange(seq, dtype=np.float64)[:, None] * inv[None, :]
    cos, sin = np.cos(ang), np.sin(ang)
    cos_t = np.tile(cos, (1, LANES // half))
    sin_t = np.tile(np.concatenate([-sin, sin], axis=1), (1, LANES // RET_DK))
    return cos_t.astype(np.float32), sin_t.astype(np.float32)


def _decay_tables(chunk):
    H = RET_HEADS
    log_g = np.log1p(-(2.0 ** (-5.0 - np.arange(H, dtype=np.float64))))
    idx = np.arange(chunk, dtype=np.float64)
    rel = idx[:, None] - idx[None, :]
    dmask = np.where(rel >= 0, np.exp(log_g[:, None, None] * np.maximum(rel, 0.0)), 0.0)
    zeta = np.exp(log_g[:, None] * (chunk - 1.0 - idx))
    xi = np.exp(log_g[:, None] * (idx + 1.0))
    g_chunk = np.exp(log_g * chunk)
    widen = lambda t: np.repeat(t.T, RET_DK, axis=1)
    f32 = lambda t: t.astype(np.float32)
    return (f32(dmask), f32(widen(zeta)), f32(widen(xi)),
            f32(np.repeat(g_chunk, RET_DK)[None, :]))


def kernel(x, mem, norm_mix, w_in, diff_q_gain, diff_k_gain, diff_lambda, diff_subln,
           group_scale, w_out, norm_x, norm_mem, xq, xkv, xq_gain, xk_gain, xo,
           norm_ffn, w_gate, w_up, w_down):
    B, S, D = x.shape
    M = mem.shape[1]
    depth = norm_mix.shape[0]
    x2 = x.reshape(B * S, D)
    mem2 = mem.reshape(B * M, D)

    ret_chunk = 256
    cos_t, sin_t = _rotary_tables(S)
    dmask, zeta, xi, gchunk = _decay_tables(ret_chunk)
    row = lambda a: a.reshape(1, -1)
    twice = lambda a: jnp.tile(a.reshape(1, -1), (1, 2))

    for l in range(depth):
        lam_init = 0.8 - 0.6 * math.exp(-0.3 * l)
        rq, rk, rv, rg, dq, dk, dv = _in_proj(
            x2, row(norm_mix[l]), w_in[l].astype(BF16), cos_t, sin_t,
            twice(diff_q_gain[l]), twice(diff_k_gain[l]), S, tm=1024)
        o_ret = _retention(rq, rk, rv, rg, dmask, zeta, xi, gchunk,
                           row(group_scale[l, :RET_WIDTH]), B, S, ts=512, chunk=ret_chunk)
        o_diff = _diff_attn(dq, dk, dv, diff_lambda[l], row(diff_subln[l]),
                            row(group_scale[l, RET_WIDTH:]), B, S, tq=1024, tk=512, rb=256,
                            lam_init=lam_init)
        kmem, vmem = _mem_kv(mem2, row(norm_mem[l]), xkv[l].astype(BF16), row(xk_gain[l]), B, M)
        x2 = _mix_xattn(x2, o_ret, o_diff, w_out[l].astype(BF16), row(norm_x[l]),
                        xq[l].astype(BF16), row(xq_gain[l]), kmem, vmem, xo[l].astype(BF16),
                        S, M, tm=1024)
        x2 = _ffn(x2, row(norm_ffn[l]), w_gate[l].astype(BF16), w_up[l].astype(BF16),
                  w_down[l].astype(BF16), tm=512, hidden_chunk=256)
    return x2.reshape(B, S, D)
```

```python
import functools
import math

import jax
import jax.numpy as jnp
import numpy as np
from jax import lax
from jax.experimental import pallas as pl
from jax.experimental.pallas import tpu as pltpu

F32 = jnp.float32
BF16 = jnp.bfloat16

EPS = 1e-6
LANES = 128
VMEM_LIMIT = 56 * 1024 * 1024

RET_HEADS = 8
RET_DK = 64
RET_WIDTH = 512
RET_THETA_BASE = 10000.0
DIFF_HEADS = 4
DIFF_DQK = 64
DIFF_DV = 128
DIFF_WIDTH = 512
XATTN_HEADS = 4
SEG = 512
N_SEG = 7
LOG2E = math.log2(math.e)
F32_SUBLANES = 8
BF16_SUBLANES = 16
VT_ROWS = DIFF_DV + BF16_SUBLANES


def _rms(x, axis=-1):
    return lax.rsqrt(jnp.mean(x * x, axis=axis, keepdims=True) + EPS)


def _low_half_mask(shape):
    lane = lax.broadcasted_iota(jnp.int32, shape, len(shape) - 1)
    return (lane % LANES) < (LANES // 2)


def _in_proj_kernel(x_ref, gain_ref, w_ref, cos_ref, sin_ref, qg_ref, kg_ref,
                    rq_ref, rk_ref, rv_ref, rg_ref, dq_ref, dk_ref, dvt_ref):
    x = x_ref[...]
    h = (x * _rms(x) * gain_ref[...]).astype(BF16)

    def proj(seg):
        return jnp.dot(h, w_ref[:, seg * SEG:(seg + 1) * SEG], preferred_element_type=F32)

    cos = cos_ref[...]
    sin = sin_ref[...]
    low = _low_half_mask(cos.shape)
    first32 = (lax.broadcasted_iota(jnp.int32, cos.shape, 1) % 64) < 32

    def rotary_store(y, out_ref, scale):
        for j in range(SEG // LANES):
            ys = y[:, j * LANES:(j + 1) * LANES]
            partner = jnp.where(first32, pltpu.roll(ys, LANES - 32, 1), pltpu.roll(ys, 32, 1))
            r = ys * cos + partner * sin
            if scale != 1.0:
                r = r * scale
            out_ref[:, j * LANES:(j + 1) * LANES] = r.astype(out_ref.dtype)

    def qknorm_store(y, g_ref, out_ref, scale):
        g = g_ref[...]
        for j in range(SEG // LANES):
            ys = y[:, j * LANES:(j + 1) * LANES]
            sq = ys * ys
            s_lo = jnp.sum(jnp.where(low, sq, 0.0), axis=-1, keepdims=True)
            s_hi = jnp.sum(jnp.where(low, 0.0, sq), axis=-1, keepdims=True)
            inv = jnp.where(low, lax.rsqrt(s_lo * (1.0 / DIFF_DQK) + EPS),
                            lax.rsqrt(s_hi * (1.0 / DIFF_DQK) + EPS))
            r = ys * inv * g
            if scale != 1.0:
                r = r * scale
            out_ref[:, j * LANES:(j + 1) * LANES] = r.astype(out_ref.dtype)

    rotary_store(proj(0), rq_ref, 1.0)
    rotary_store(proj(1), rk_ref, RET_DK ** -0.5)
    rv_ref[...] = proj(2).astype(BF16)
    g = proj(3)
    rg_ref[...] = (g * jax.nn.sigmoid(g)).astype(BF16)
    qknorm_store(proj(4), qg_ref, dq_ref, DIFF_DQK ** -0.5 * LOG2E)
    qknorm_store(proj(5), kg_ref, dk_ref, 1.0)
    dv = proj(6)
    ones = jnp.ones((VT_ROWS - DIFF_DV, dv.shape[0]), BF16)
    for hh in range(DIFF_HEADS):
        dvt_ref[hh * VT_ROWS:hh * VT_ROWS + DIFF_DV, :] = (
            dv[:, hh * DIFF_DV:(hh + 1) * DIFF_DV].T.astype(BF16))
        dvt_ref[hh * VT_ROWS + DIFF_DV:(hh + 1) * VT_ROWS, :] = ones


def _in_proj(x2, gain, w_in, cos_t, sin_t, qg, kg, seq, tm):
    T, D = x2.shape
    n_pos_blocks = seq // tm
    tok = lambda i: (i, 0)
    const = lambda i: (0, 0)
    pos = lambda i: (i % n_pos_blocks, 0)
    out = jax.ShapeDtypeStruct((T, SEG), BF16)
    return pl.pallas_call(
        _in_proj_kernel,
        grid=(T // tm,),
        in_specs=[
            pl.BlockSpec((tm, D), tok),
            pl.BlockSpec((1, D), const),
            pl.BlockSpec((D, N_SEG * SEG), const),
            pl.BlockSpec((tm, LANES), pos),
            pl.BlockSpec((tm, LANES), pos),
            pl.BlockSpec((1, LANES), const),
            pl.BlockSpec((1, LANES), const),
        ],
        out_specs=[pl.BlockSpec((tm, SEG), tok)] * (N_SEG - 1)
        + [pl.BlockSpec((DIFF_HEADS * VT_ROWS, tm), lambda i: (0, i))],
        out_shape=[out] * (N_SEG - 1)
        + [jax.ShapeDtypeStruct((DIFF_HEADS * VT_ROWS, T), BF16)],
        compiler_params=pltpu.CompilerParams(
            dimension_semantics=("parallel",), vmem_limit_bytes=VMEM_LIMIT),
        name="in_proj",
    )(x2, gain, w_in, cos_t, sin_t, qg, kg)


def _retention_kernel(q_ref, k_ref, v_ref, g_ref, dmask_ref, zeta_ref, xi_ref, gchunk_ref,
                      gs_ref, o_ref, state_ref, *, chunk, n_chunks):
    @pl.when(pl.program_id(1) == 0)
    def _():
        state_ref[...] = jnp.zeros_like(state_ref)

    n_slabs = RET_WIDTH // LANES
    low = _low_half_mask((chunk, LANES))
    row = lax.broadcasted_iota(jnp.int32, (LANES, LANES), 0)
    col = lax.broadcasted_iota(jnp.int32, (LANES, LANES), 1)
    blockdiag = (row < 64) == (col < 64)
    nt = (((1,), (1,)), ((), ()))
    tn = (((0,), (0,)), ((), ()))

    for s in range(n_slabs):
        sl = slice(s * LANES, (s + 1) * LANES)
        zeta = zeta_ref[:, sl]
        xi = xi_ref[:, sl]
        gch = gchunk_ref[:, sl]
        for c in range(n_chunks):
            rows = slice(c * chunk, (c + 1) * chunk)
            q = q_ref[rows, sl]
            k = k_ref[rows, sl]
            v = v_ref[rows, sl]
            zero = jnp.zeros_like(q)
            q_lo = jnp.where(low, q, zero)
            q_hi = jnp.where(low, zero, q)
            s_lo = lax.dot_general(q_lo, k, nt, preferred_element_type=F32) * dmask_ref[2 * s]
            s_hi = lax.dot_general(q_hi, k, nt, preferred_element_type=F32) * dmask_ref[2 * s + 1]
            o_lo = jnp.dot(s_lo.astype(BF16), v, preferred_element_type=F32)
            o_hi = jnp.dot(s_hi.astype(BF16), v, preferred_element_type=F32)
            r_prev = state_ref[s]
            o_inter = jnp.dot(q, r_prev.astype(BF16), preferred_element_type=F32) * xi
            o = jnp.where(low, o_lo, o_hi) + o_inter
            kz = (k.astype(F32) * zeta).astype(BF16)
            kv = lax.dot_general(kz, v, tn, preferred_element_type=F32)
            state_ref[s] = r_prev * gch + jnp.where(blockdiag, kv, 0.0)
            sq = o * o
            m_lo = jnp.sum(jnp.where(low, sq, 0.0), axis=-1, keepdims=True)
            m_hi = jnp.sum(jnp.where(low, 0.0, sq), axis=-1, keepdims=True)
            inv = jnp.where(low, lax.rsqrt(m_lo * (1.0 / 64) + EPS),
                            lax.rsqrt(m_hi * (1.0 / 64) + EPS))
            out = o * inv * g_ref[rows, sl].astype(F32) * gs_ref[:, sl]
            o_ref[rows, sl] = out.astype(o_ref.dtype)


def _retention(rq, rk, rv, rg, dmask, zeta, xi, gchunk, gs_ret, batch, seq, ts, chunk):
    T = rq.shape[0]
    nblk = seq // ts
    tok = lambda b, i: (b * nblk + i, 0)
    c2 = lambda b, i: (0, 0)
    c3 = lambda b, i: (0, 0, 0)
    kern = functools.partial(_retention_kernel, chunk=chunk, n_chunks=ts // chunk)
    return pl.pallas_call(
        kern,
        grid=(batch, nblk),
        in_specs=[pl.BlockSpec((ts, RET_WIDTH), tok)] * 4 + [
            pl.BlockSpec((RET_HEADS, chunk, chunk), c3),
            pl.BlockSpec((chunk, RET_WIDTH), c2),
            pl.BlockSpec((chunk, RET_WIDTH), c2),
            pl.BlockSpec((1, RET_WIDTH), c2),
            pl.BlockSpec((1, RET_WIDTH), c2),
        ],
        out_specs=pl.BlockSpec((ts, RET_WIDTH), tok),
        out_shape=jax.ShapeDtypeStruct((T, RET_WIDTH), BF16),
        scratch_shapes=[pltpu.VMEM((RET_WIDTH // LANES, LANES, LANES), F32)],
        compiler_params=pltpu.CompilerParams(
            dimension_semantics=("parallel", "arbitrary"), vmem_limit_bytes=VMEM_LIMIT),
        name="retention",
    )(rq, rk, rv, rg, dmask, zeta, xi, gchunk, gs_ret)


def _diff_attn_kernel(q_ref, k_ref, vt_ref, lam_ref, subln_ref, gs_ref, *refs,
                      n_cast, tq, tk, rb, lam_init):
    w_refs = refs[:n_cast]
    o_ref = refs[n_cast]
    wb_refs = refs[n_cast + 1:2 * n_cast + 1]
    qs_ref, s_ref, m_ref, acc_ref = refs[2 * n_cast + 1:]
    i = pl.program_id(2)
    q = q_ref[...]
    low = _low_half_mask(q.shape)
    zero = jnp.zeros_like(q)
    qs_ref[:tq, :] = jnp.where(low, q, zero)
    qs_ref[tq:, :] = jnp.where(low, zero, q)
    nt = (((1,), (1,)), ((), ()))

    m_ref[...] = jnp.full_like(m_ref, -jnp.inf)
    acc_ref[...] = jnp.zeros_like(acc_ref)

    n_blocks = 2 * tq // rb
    n_diag = tq // tk
    n_full = i * n_diag

    def scores(j, b):
        start = pl.multiple_of(j * tk, tk)
        cols = slice(b * rb, (b + 1) * rb)
        s_ref[:, cols] = lax.dot_general(
            k_ref[pl.ds(start, tk), :], qs_ref[cols, :], nt, preferred_element_type=F32)

    def consume(j, b, key_offset):
        start = pl.multiple_of(j * tk, tk)
        cols = slice(b * rb, (b + 1) * rb)
        s = s_ref[:, cols]
        if key_offset is not None:
            kpos = lax.broadcasted_iota(jnp.int32, s.shape, 0) + key_offset
            qpos = lax.broadcasted_iota(jnp.int32, s.shape, 1) + (b * rb) % tq
            s = jnp.where(kpos <= qpos, s, -jnp.inf)
        m_prev = m_ref[:, cols]
        m_new = jnp.maximum(m_prev, jnp.max(s, axis=0, keepdims=True))
        alpha = jnp.exp2(m_prev - m_new)
        p = jnp.exp2(s - m_new[0:1, :])
        pv = jnp.dot(vt_ref[:, pl.ds(start, tk)], p.astype(BF16),
                     preferred_element_type=F32)
        acc_ref[:, cols] = alpha[0:1, :] * acc_ref[:, cols] + pv
        m_ref[:, cols] = m_new

    def diag_kind(d, b):
        q_first = (b * rb) % tq
        if d * tk > q_first + rb - 1:
            return "skip"
        return "full" if (d + 1) * tk - 1 <= q_first else "mask"

    ahead = n_blocks // 2
    for b in range(ahead):
        scores(0, b)

    def full_chunks(j0, count):
        for u in range(count):
            j = j0 + u
            for b in range(n_blocks):
                if b + ahead < n_blocks:
                    scores(j, b + ahead)
                else:
                    scores(j + 1, b + ahead - n_blocks)
                consume(j, b, None)

    def body(t, carry):
        full_chunks(t * (2 * n_diag), 2 * n_diag)
        return carry

    lax.fori_loop(0, i // 2, body, 0)

    @pl.when(i % 2 == 1)
    def _():
        full_chunks((i - 1) * n_diag, n_diag)

    for d in range(n_diag):
        for b in range(n_blocks):
            nd, nb = (d, b + ahead) if b + ahead < n_blocks else (d + 1, b + ahead - n_blocks)
            if nd < n_diag and diag_kind(nd, nb) != "skip":
                scores(n_full + nd, nb)
            kind = diag_kind(d, b)
            if kind != "skip":
                consume(n_full + d, b, d * tk if kind == "mask" else None)

    lv = lam_ref[...]
    lam = (jnp.exp(jnp.sum(lv[0:1] * lv[1:2], axis=-1, keepdims=True))
           - jnp.exp(jnp.sum(lv[2:3] * lv[3:4], axis=-1, keepdims=True)) + lam_init)
    acc = acc_ref[...]
    o_all = acc[:DIFF_DV, :] * (1.0 / acc[DIFF_DV:DIFF_DV + 1, :])
    o = (o_all[:, :tq] - lam * o_all[:, tq:]).T
    y = o * _rms(o) * subln_ref[...] * (1.0 - lam_init)
    o_ref[...] = (y * gs_ref[...]).astype(o_ref.dtype)

    for w_ref, wb_ref in zip(w_refs, wb_refs):
        wb_ref[...] = w_ref[...].astype(wb_ref.dtype)


def _diff_attn(dq, dk, dvt, lam_p, subln, gs_diff, cast_weights, batch, seq, tq, tk, rb,
               lam_init):
    T = dq.shape[0]
    nq = seq // tq
    n_steps = batch * DIFF_HEADS * nq
    assert seq % tq == 0 and tq % tk == 0 and tq % rb == 0

    def cast_spec(w):
        rows = w.shape[0]
        n_blk = max(n for n in range(1, n_steps + 1)
                    if n_steps % n == 0 and rows % (n * BF16_SUBLANES) == 0)
        hold = n_steps // n_blk
        return pl.BlockSpec(
            (rows // n_blk, w.shape[1]),
            lambda b, h, i: (((b * DIFF_HEADS + h) * nq + i) // hold, 0))

    w_specs = [cast_spec(w) for w in cast_weights]
    kern = functools.partial(_diff_attn_kernel, n_cast=len(cast_weights), tq=tq, tk=tk, rb=rb,
                             lam_init=lam_init)
    outs = pl.pallas_call(
        kern,
        grid=(batch, DIFF_HEADS, nq),
        in_specs=[
            pl.BlockSpec((tq, LANES), lambda b, h, i: (b * nq + i, h)),
            pl.BlockSpec((seq, LANES), lambda b, h, i: (b, h)),
            pl.BlockSpec((VT_ROWS, seq), lambda b, h, i: (h, b)),
            pl.BlockSpec((4, DIFF_DQK), lambda b, h, i: (0, 0)),
            pl.BlockSpec((1, LANES), lambda b, h, i: (0, 0)),
            pl.BlockSpec((1, LANES), lambda b, h, i: (0, h)),
        ] + w_specs,
        out_specs=[pl.BlockSpec((tq, LANES), lambda b, h, i: (b * nq + i, h))] + w_specs,
        out_shape=[jax.ShapeDtypeStruct((T, DIFF_WIDTH), BF16)]
        + [jax.ShapeDtypeStruct(w.shape, BF16) for w in cast_weights],
        scratch_shapes=[pltpu.VMEM((2 * tq, LANES), BF16), pltpu.VMEM((tk, 2 * tq), F32),
                        pltpu.VMEM((F32_SUBLANES, 2 * tq), F32),
                        pltpu.VMEM((VT_ROWS, 2 * tq), F32)],
        compiler_params=pltpu.CompilerParams(
            dimension_semantics=("parallel", "parallel", "arbitrary"),
            vmem_limit_bytes=VMEM_LIMIT),
        name="diff_attn",
    )(dq, dk, dvt, lam_p, subln, gs_diff, *cast_weights)
    return outs[0], outs[1:]


def _mem_kv_kernel(mem_ref, gain_ref, w_ref, kg_ref, k_ref, v_ref):
    m = mem_ref[...]
    h = (m * _rms(m) * gain_ref[...]).astype(BF16)
    D = m.shape[1]
    dh = D // XATTN_HEADS
    k = jnp.dot(h, w_ref[:, :D], preferred_element_type=F32)
    v_ref[...] = jnp.dot(h, w_ref[:, D:], preferred_element_type=F32).astype(v_ref.dtype)
    for hh in range(XATTN_HEADS):
        kh = k[:, hh * dh:(hh + 1) * dh]
        k_ref[:, hh * dh:(hh + 1) * dh] = (kh * _rms(kh) * kg_ref[...]).astype(k_ref.dtype)


def _mem_kv(mem2, gain, xkv, kg, batch, mem_len):
    BM, D = mem2.shape
    const = lambda b: (0, 0)
    out = jax.ShapeDtypeStruct((BM, D), BF16)
    return pl.pallas_call(
        _mem_kv_kernel,
        grid=(batch,),
        in_specs=[
            pl.BlockSpec((mem_len, D), lambda b: (b, 0)),
            pl.BlockSpec((1, D), const),
            pl.BlockSpec((D, 2 * D), const),
            pl.BlockSpec((1, D // XATTN_HEADS), const),
        ],
        out_specs=[pl.BlockSpec((mem_len, D), lambda b: (b, 0))] * 2,
        out_shape=[out, out],
        compiler_params=pltpu.CompilerParams(
            dimension_semantics=("parallel",), vmem_limit_bytes=VMEM_LIMIT),
        name="mem_kv",
    )(mem2, gain, xkv, kg)


def _mix_xattn_kernel(x_ref, oret_ref, odiff_ref, wout_ref, nx_ref, xq_ref, qg_ref,
                      kmem_ref, vmem_ref, xo_ref, out_ref):
    D = x_ref.shape[1]
    dh = D // XATTN_HEADS
    x1 = (x_ref[...]
          + jnp.dot(oret_ref[...], wout_ref[:RET_WIDTH, :], preferred_element_type=F32)
          + jnp.dot(odiff_ref[...], wout_ref[RET_WIDTH:, :], preferred_element_type=F32))
    h = (x1 * _rms(x1) * nx_ref[...]).astype(BF16)
    q = jnp.dot(h, xq_ref[...], preferred_element_type=F32)
    nt = (((1,), (1,)), ((), ()))
    outs = []
    for hh in range(XATTN_HEADS):
        sl = slice(hh * dh, (hh + 1) * dh)
        qh = q[:, sl]
        qn = (qh * _rms(qh) * qg_ref[...] * (dh ** -0.5)).astype(BF16)
        s = lax.dot_general(qn, kmem_ref[:, sl], nt, preferred_element_type=F32)
        e = jnp.exp(s - jnp.max(s, axis=-1, keepdims=True))
        p = e * (1.0 / jnp.sum(e, axis=-1, keepdims=True))
        outs.append(jnp.dot(p.astype(BF16), vmem_ref[:, sl],
                            preferred_element_type=F32).astype(BF16))
    o = jnp.concatenate(outs, axis=1)
    out_ref[...] = x1 + jnp.dot(o, xo_ref[...], preferred_element_type=F32)


def _mix_xattn(x2, o_ret, o_diff, w_out, nx, xq, qg, kmem, vmem, xo, seq, mem_len, tm):
    T, D = x2.shape
    nblk = seq // tm
    tok = lambda i: (i, 0)
    const = lambda i: (0, 0)
    memb = lambda i: (i // nblk, 0)
    return pl.pallas_call(
        _mix_xattn_kernel,
        grid=(T // tm,),
        in_specs=[
            pl.BlockSpec((tm, D), tok),
            pl.BlockSpec((tm, RET_WIDTH), tok),
            pl.BlockSpec((tm, DIFF_WIDTH), tok),
            pl.BlockSpec((D, D), const),
            pl.BlockSpec((1, D), const),
            pl.BlockSpec((D, D), const),
            pl.BlockSpec((1, D // XATTN_HEADS), const),
            pl.BlockSpec((mem_len, D), memb),
            pl.BlockSpec((mem_len, D), memb),
            pl.BlockSpec((D, D), const),
        ],
        out_specs=pl.BlockSpec((tm, D), tok),
        out_shape=jax.ShapeDtypeStruct((T, D), F32),
        compiler_params=pltpu.CompilerParams(
            dimension_semantics=("parallel",), vmem_limit_bytes=VMEM_LIMIT),
        name="mix_xattn",
    )(x2, o_ret, o_diff, w_out, nx, xq, qg, kmem, vmem, xo)


def _ffn_kernel(x_ref, gain_ref, wg_ref, wu_ref, wd_ref, out_ref, *, hidden_chunk):
    x = x_ref[...]
    h = (x * _rms(x) * gain_ref[...]).astype(BF16)
    hidden = wg_ref.shape[1]
    acc = x
    for c in range(hidden // hidden_chunk):
        sl = slice(c * hidden_chunk, (c + 1) * hidden_chunk)
        g = jnp.dot(h, wg_ref[:, sl], preferred_element_type=F32)
        u = jnp.dot(h, wu_ref[:, sl], preferred_element_type=F32)
        a = (g * jax.nn.sigmoid(g) * u).astype(BF16)
        acc = acc + jnp.dot(a, wd_ref[sl, :], preferred_element_type=F32)
    out_ref[...] = acc


def _ffn(x2, gain, wg, wu, wd, tm, hidden_chunk):
    T, D = x2.shape
    hidden = wg.shape[1]
    tok = lambda i: (i, 0)
    const = lambda i: (0, 0)
    kern = functools.partial(_ffn_kernel, hidden_chunk=hidden_chunk)
    return pl.pallas_call(
        kern,
        grid=(T // tm,),
        in_specs=[
            pl.BlockSpec((tm, D), tok),
            pl.BlockSpec((1, D), const),
            pl.BlockSpec((D, hidden), const),
            pl.BlockSpec((D, hidden), const),
            pl.BlockSpec((hidden, D), const),
        ],
        out_specs=pl.BlockSpec((tm, D), tok),
        out_shape=jax.ShapeDtypeStruct((T, D), F32),
        compiler_params=pltpu.CompilerParams(
            dimension_semantics=("parallel",), vmem_limit_bytes=VMEM_LIMIT),
        name="ffn",
    )(x2, gain, wg, wu, wd)


def _rotary_tables(seq):
    half = RET_DK // 2
    inv = 1.0 / (RET_THETA_BASE ** np.linspace(0.0, 1.0, half))
    ang = np.arange(seq, dtype=np.float64)[:, None] * inv[None, :]
    cos, sin = np.cos(ang), np.sin(ang)
    cos_t = np.tile(cos, (1, LANES // half))
    sin_t = np.tile(np.concatenate([-sin, sin], axis=1), (1, LANES // RET_DK))
    return cos_t.astype(np.float32), sin_t.astype(np.float32)


def _decay_tables(chunk):
    H = RET_HEADS
    log_g = np.log1p(-(2.0 ** (-5.0 - np.arange(H, dtype=np.float64))))
    idx = np.arange(chunk, dtype=np.float64)
    rel = idx[:, None] - idx[None, :]
    dmask = np.where(rel >= 0, np.exp(log_g[:, None, None] * np.maximum(rel, 0.0)), 0.0)
    zeta = np.exp(log_g[:, None] * (chunk - 1.0 - idx))
    xi = np.exp(log_g[:, None] * (idx + 1.0))
    g_chunk = np.exp(log_g * chunk)
    widen = lambda t: np.repeat(t.T, RET_DK, axis=1)
    f32 = lambda t: t.astype(np.float32)
    return (f32(dmask), f32(widen(zeta)), f32(widen(xi)),
            f32(np.repeat(g_chunk, RET_DK)[None, :]))


def kernel(x, mem, norm_mix, w_in, diff_q_gain, diff_k_gain, diff_lambda, diff_subln,
           group_scale, w_out, norm_x, norm_mem, xq, xkv, xq_gain, xk_gain, xo,
           norm_ffn, w_gate, w_up, w_down):
    B, S, D = x.shape
    M = mem.shape[1]
    depth = norm_mix.shape[0]
    x2 = x.reshape(B * S, D)
    mem2 = mem.reshape(B * M, D)

    ret_chunk = 256
    cos_t, sin_t = _rotary_tables(S)
    dmask, zeta, xi, gchunk = _decay_tables(ret_chunk)
    row = lambda a: a.reshape(1, -1)
    twice = lambda a: jnp.tile(a.reshape(1, -1), (1, 2))

    for l in range(depth):
        lam_init = 0.8 - 0.6 * math.exp(-0.3 * l)
        rq, rk, rv, rg, dq, dk, dv = _in_proj(
            x2, row(norm_mix[l]), w_in[l].astype(BF16), cos_t, sin_t,
            twice(diff_q_gain[l]), twice(diff_k_gain[l]), S, tm=1024)
        o_ret = _retention(rq, rk, rv, rg, dmask, zeta, xi, gchunk,
                           row(group_scale[l, :RET_WIDTH]), B, S, ts=512, chunk=ret_chunk)
        later_weights = [w_out[l], xq[l], xkv[l], xo[l], w_gate[l], w_up[l], w_down[l]]
        o_diff, (w_out_b, xq_b, xkv_b, xo_b, w_gate_b, w_up_b, w_down_b) = _diff_attn(
            dq, dk, dv, diff_lambda[l], row(diff_subln[l]), row(group_scale[l, RET_WIDTH:]),
            later_weights, B, S, tq=1024, tk=512, rb=256, lam_init=lam_init)
        kmem, vmem = _mem_kv(mem2, row(norm_mem[l]), xkv_b, row(xk_gain[l]), B, M)
        x2 = _mix_xattn(x2, o_ret, o_diff, w_out_b, row(norm_x[l]), xq_b, row(xq_gain[l]),
                        kmem, vmem, xo_b, S, M, tm=1024)
        x2 = _ffn(x2, row(norm_ffn[l]), w_gate_b, w_up_b, w_down_b, tm=512, hidden_chunk=256)
    return x2.reshape(B, S, D)
```

```python
import functools
import math

import jax
import jax.numpy as jnp
import numpy as np
from jax import lax
from jax.experimental import pallas as pl
from jax.experimental.pallas import tpu as pltpu

F32 = jnp.float32
BF16 = jnp.bfloat16

EPS = 1e-6
LANES = 128
VMEM_LIMIT = 56 * 1024 * 1024

RET_HEADS = 8
RET_DK = 64
RET_WIDTH = 512
RET_THETA_BASE = 10000.0
DIFF_HEADS = 4
DIFF_DQK = 64
DIFF_DV = 128
DIFF_WIDTH = 512
XATTN_HEADS = 4
SEG = 512
N_SEG = 7
LOG2E = math.log2(math.e)
F32_SUBLANES = 8
BF16_SUBLANES = 16
VT_ROWS = DIFF_DV + BF16_SUBLANES


def _rms(x, axis=-1):
    return lax.rsqrt(jnp.mean(x * x, axis=axis, keepdims=True) + EPS)


def _low_half_mask(shape):
    lane = lax.broadcasted_iota(jnp.int32, shape, len(shape) - 1)
    return (lane % LANES) < (LANES // 2)


def _in_proj_kernel(x_ref, gain_ref, w_ref, cos_ref, sin_ref, qg_ref, kg_ref,
                    rq_ref, rk_ref, rv_ref, rg_ref, dq_ref, dk_ref, dvt_ref):
    x = x_ref[...]
    h = (x * _rms(x) * gain_ref[...]).astype(BF16)

    def proj(seg):
        return jnp.dot(h, w_ref[:, seg * SEG:(seg + 1) * SEG], preferred_element_type=F32)

    cos = cos_ref[...]
    sin = sin_ref[...]
    low = _low_half_mask(cos.shape)
    first32 = (lax.broadcasted_iota(jnp.int32, cos.shape, 1) % 64) < 32

    def rotary_store(y, out_ref, scale):
        for j in range(SEG // LANES):
            ys = y[:, j * LANES:(j + 1) * LANES]
            partner = jnp.where(first32, pltpu.roll(ys, LANES - 32, 1), pltpu.roll(ys, 32, 1))
            r = ys * cos + partner * sin
            if scale != 1.0:
                r = r * scale
            out_ref[:, j * LANES:(j + 1) * LANES] = r.astype(out_ref.dtype)

    def qknorm_store(y, g_ref, out_ref, scale):
        g = g_ref[...]
        for j in range(SEG // LANES):
            ys = y[:, j * LANES:(j + 1) * LANES]
            sq = ys * ys
            s_lo = jnp.sum(jnp.where(low, sq, 0.0), axis=-1, keepdims=True)
            s_hi = jnp.sum(jnp.where(low, 0.0, sq), axis=-1, keepdims=True)
            inv = jnp.where(low, lax.rsqrt(s_lo * (1.0 / DIFF_DQK) + EPS),
                            lax.rsqrt(s_hi * (1.0 / DIFF_DQK) + EPS))
            r = ys * inv * g
            if scale != 1.0:
                r = r * scale
            out_ref[:, j * LANES:(j + 1) * LANES] = r.astype(out_ref.dtype)

    rotary_store(proj(0), rq_ref, 1.0)
    rotary_store(proj(1), rk_ref, RET_DK ** -0.5)
    rv_ref[...] = proj(2).astype(BF16)
    g = proj(3)
    rg_ref[...] = (g * jax.nn.sigmoid(g)).astype(BF16)
    qknorm_store(proj(4), qg_ref, dq_ref, DIFF_DQK ** -0.5 * LOG2E)
    qknorm_store(proj(5), kg_ref, dk_ref, 1.0)
    dv = proj(6)
    ones = jnp.ones((VT_ROWS - DIFF_DV, dv.shape[0]), BF16)
    for hh in range(DIFF_HEADS):
        dvt_ref[hh * VT_ROWS:hh * VT_ROWS + DIFF_DV, :] = (
            dv[:, hh * DIFF_DV:(hh + 1) * DIFF_DV].T.astype(BF16))
        dvt_ref[hh * VT_ROWS + DIFF_DV:(hh + 1) * VT_ROWS, :] = ones


def _in_proj(x2, gain, w_in, cos_t, sin_t, qg, kg, seq, tm):
    T, D = x2.shape
    n_pos_blocks = seq // tm
    tok = lambda i: (i, 0)
    const = lambda i: (0, 0)
    pos = lambda i: (i % n_pos_blocks, 0)
    out = jax.ShapeDtypeStruct((T, SEG), BF16)
    return pl.pallas_call(
        _in_proj_kernel,
        grid=(T // tm,),
        in_specs=[
            pl.BlockSpec((tm, D), tok),
            pl.BlockSpec((1, D), const),
            pl.BlockSpec((D, N_SEG * SEG), const),
            pl.BlockSpec((tm, LANES), pos),
            pl.BlockSpec((tm, LANES), pos),
            pl.BlockSpec((1, LANES), const),
            pl.BlockSpec((1, LANES), const),
        ],
        out_specs=[pl.BlockSpec((tm, SEG), tok)] * (N_SEG - 1)
        + [pl.BlockSpec((DIFF_HEADS * VT_ROWS, tm), lambda i: (0, i))],
        out_shape=[out] * (N_SEG - 1)
        + [jax.ShapeDtypeStruct((DIFF_HEADS * VT_ROWS, T), BF16)],
        compiler_params=pltpu.CompilerParams(
            dimension_semantics=("parallel",), vmem_limit_bytes=VMEM_LIMIT),
        name="in_proj",
    )(x2, gain, w_in, cos_t, sin_t, qg, kg)


def _retention_kernel(q_ref, k_ref, v_ref, g_ref, dmask_ref, zeta_ref, xi_ref, gchunk_ref,
                      gs_ref, o_ref, state_ref, *, chunk, n_chunks):
    @pl.when(pl.program_id(1) == 0)
    def _():
        state_ref[...] = jnp.zeros_like(state_ref)

    n_slabs = RET_WIDTH // LANES
    low = _low_half_mask((chunk, LANES))
    row = lax.broadcasted_iota(jnp.int32, (LANES, LANES), 0)
    col = lax.broadcasted_iota(jnp.int32, (LANES, LANES), 1)
    blockdiag = (row < 64) == (col < 64)
    nt = (((1,), (1,)), ((), ()))
    tn = (((0,), (0,)), ((), ()))

    for c in range(n_chunks):
        rows = slice(c * chunk, (c + 1) * chunk)
        for s in range(n_slabs):
            sl = slice(s * LANES, (s + 1) * LANES)
            zeta = zeta_ref[:, sl]
            xi = xi_ref[:, sl]
            gch = gchunk_ref[:, sl]
            q = q_ref[rows, sl]
            k = k_ref[rows, sl]
            v = v_ref[rows, sl]
            zero = jnp.zeros_like(q)
            q2 = jnp.concatenate([jnp.where(low, q, zero), jnp.where(low, zero, q)], axis=0)
            s2 = lax.dot_general(q2, k, nt, preferred_element_type=F32) * dmask_ref[s]
            o2 = jnp.dot(s2.astype(BF16), v, preferred_element_type=F32)
            r_prev = state_ref[s]
            o_inter = jnp.dot(q, r_prev.astype(BF16), preferred_element_type=F32) * xi
            o = jnp.where(low, o2[:chunk], o2[chunk:]) + o_inter
            kz = (k.astype(F32) * zeta).astype(BF16)
            kv = lax.dot_general(kz, v, tn, preferred_element_type=F32)
            state_ref[s] = r_prev * gch + jnp.where(blockdiag, kv, 0.0)
            sq = o * o
            m_lo = jnp.sum(jnp.where(low, sq, 0.0), axis=-1, keepdims=True)
            m_hi = jnp.sum(jnp.where(low, 0.0, sq), axis=-1, keepdims=True)
            inv = jnp.where(low, lax.rsqrt(m_lo * (1.0 / 64) + EPS),
                            lax.rsqrt(m_hi * (1.0 / 64) + EPS))
            out = o * inv * g_ref[rows, sl].astype(F32) * gs_ref[:, sl]
            o_ref[rows, sl] = out.astype(o_ref.dtype)


def _retention(rq, rk, rv, rg, dmask, zeta, xi, gchunk, gs_ret, batch, seq, ts, chunk):
    T = rq.shape[0]
    nblk = seq // ts
    tok = lambda b, i: (b * nblk + i, 0)
    c2 = lambda b, i: (0, 0)
    c3 = lambda b, i: (0, 0, 0)
    kern = functools.partial(_retention_kernel, chunk=chunk, n_chunks=ts // chunk)
    return pl.pallas_call(
        kern,
        grid=(batch, nblk),
        in_specs=[pl.BlockSpec((ts, RET_WIDTH), tok)] * 4 + [
            pl.BlockSpec((RET_HEADS // 2, 2 * chunk, chunk), c3),
            pl.BlockSpec((chunk, RET_WIDTH), c2),
            pl.BlockSpec((chunk, RET_WIDTH), c2),
            pl.BlockSpec((1, RET_WIDTH), c2),
            pl.BlockSpec((1, RET_WIDTH), c2),
        ],
        out_specs=pl.BlockSpec((ts, RET_WIDTH), tok),
        out_shape=jax.ShapeDtypeStruct((T, RET_WIDTH), BF16),
        scratch_shapes=[pltpu.VMEM((RET_WIDTH // LANES, LANES, LANES), F32)],
        compiler_params=pltpu.CompilerParams(
            dimension_semantics=("parallel", "arbitrary"), vmem_limit_bytes=VMEM_LIMIT),
        name="retention",
    )(rq, rk, rv, rg, dmask, zeta, xi, gchunk, gs_ret)


def _diff_attn_kernel(q_ref, k_ref, vt_ref, lam_ref, subln_ref, gs_ref, *refs,
                      n_cast, tq, tk, rb, lam_init):
    w_refs = refs[:n_cast]
    o_ref = refs[n_cast]
    wb_refs = refs[n_cast + 1:2 * n_cast + 1]
    qs_ref, s_ref, m_ref, acc_ref = refs[2 * n_cast + 1:]
    i = pl.program_id(2)
    q = q_ref[...]
    low = _low_half_mask(q.shape)
    zero = jnp.zeros_like(q)
    qs_ref[:tq, :] = jnp.where(low, q, zero)
    qs_ref[tq:, :] = jnp.where(low, zero, q)
    nt = (((1,), (1,)), ((), ()))

    m_ref[...] = jnp.full_like(m_ref, -jnp.inf)
    acc_ref[...] = jnp.zeros_like(acc_ref)

    n_blocks = 2 * tq // rb
    n_diag = tq // tk
    n_full = i * n_diag

    def scores(j, b):
        start = pl.multiple_of(j * tk, tk)
        cols = slice(b * rb, (b + 1) * rb)
        s_ref[:, cols] = lax.dot_general(
            k_ref[pl.ds(start, tk), :], qs_ref[cols, :], nt, preferred_element_type=F32)

    def consume(j, b, key_offset):
        start = pl.multiple_of(j * tk, tk)
        cols = slice(b * rb, (b + 1) * rb)
        s = s_ref[:, cols]
        if key_offset is not None:
            kpos = lax.broadcasted_iota(jnp.int32, s.shape, 0) + key_offset
            qpos = lax.broadcasted_iota(jnp.int32, s.shape, 1) + (b * rb) % tq
            s = jnp.where(kpos <= qpos, s, -jnp.inf)
        m_prev = m_ref[:, cols]
        m_new = jnp.maximum(m_prev, jnp.max(s, axis=0, keepdims=True))
        alpha = jnp.exp2(m_prev - m_new)
        p = jnp.exp2(s - m_new[0:1, :])
        pv = jnp.dot(vt_ref[:, pl.ds(start, tk)], p.astype(BF16),
                     preferred_element_type=F32)
        acc_ref[:, cols] = alpha[0:1, :] * acc_ref[:, cols] + pv
        m_ref[:, cols] = m_new

    def diag_kind(d, b):
        q_first = (b * rb) % tq
        if d * tk > q_first + rb - 1:
            return "skip"
        return "full" if (d + 1) * tk - 1 <= q_first else "mask"

    ahead = n_blocks // 2
    for b in range(ahead):
        scores(0, b)

    def full_chunks(j0, count):
        for u in range(count):
            j = j0 + u
            for b in range(n_blocks):
                if b + ahead < n_blocks:
                    scores(j, b + ahead)
                else:
                    scores(j + 1, b + ahead - n_blocks)
                consume(j, b, None)

    def body(t, carry):
        full_chunks(t * (2 * n_diag), 2 * n_diag)
        return carry

    lax.fori_loop(0, i // 2, body, 0)

    @pl.when(i % 2 == 1)
    def _():
        full_chunks((i - 1) * n_diag, n_diag)

    for d in range(n_diag):
        for b in range(n_blocks):
            nd, nb = (d, b + ahead) if b + ahead < n_blocks else (d + 1, b + ahead - n_blocks)
            if nd < n_diag and diag_kind(nd, nb) != "skip":
                scores(n_full + nd, nb)
            kind = diag_kind(d, b)
            if kind != "skip":
                consume(n_full + d, b, d * tk if kind == "mask" else None)

    lv = lam_ref[...]
    lam = (jnp.exp(jnp.sum(lv[0:1] * lv[1:2], axis=-1, keepdims=True))
           - jnp.exp(jnp.sum(lv[2:3] * lv[3:4], axis=-1, keepdims=True)) + lam_init)
    acc = acc_ref[...]
    o_all = acc[:DIFF_DV, :] * (1.0 / acc[DIFF_DV:DIFF_DV + 1, :])
    o = (o_all[:, :tq] - lam * o_all[:, tq:]).T
    y = o * _rms(o) * subln_ref[...] * (1.0 - lam_init)
    o_ref[...] = (y * gs_ref[...]).astype(o_ref.dtype)

    for w_ref, wb_ref in zip(w_refs, wb_refs):
        wb_ref[...] = w_ref[...].astype(wb_ref.dtype)


def _diff_attn(dq, dk, dvt, lam_p, subln, gs_diff, cast_weights, batch, seq, tq, tk, rb,
               lam_init):
    T = dq.shape[0]
    nq = seq // tq
    n_steps = batch * DIFF_HEADS * nq
    assert seq % tq == 0 and tq % tk == 0 and tq % rb == 0

    def cast_spec(w):
        rows = w.shape[0]
        n_blk = max(n for n in range(1, n_steps + 1)
                    if n_steps % n == 0 and rows % (n * BF16_SUBLANES) == 0)
        hold = n_steps // n_blk
        return pl.BlockSpec(
            (rows // n_blk, w.shape[1]),
            lambda b, h, i: (((b * DIFF_HEADS + h) * nq + i) // hold, 0))

    w_specs = [cast_spec(w) for w in cast_weights]
    kern = functools.partial(_diff_attn_kernel, n_cast=len(cast_weights), tq=tq, tk=tk, rb=rb,
                             lam_init=lam_init)
    outs = pl.pallas_call(
        kern,
        grid=(batch, DIFF_HEADS, nq),
        in_specs=[
            pl.BlockSpec((tq, LANES), lambda b, h, i: (b * nq + i, h)),
            pl.BlockSpec((seq, LANES), lambda b, h, i: (b, h)),
            pl.BlockSpec((VT_ROWS, seq), lambda b, h, i: (h, b)),
            pl.BlockSpec((4, DIFF_DQK), lambda b, h, i: (0, 0)),
            pl.BlockSpec((1, LANES), lambda b, h, i: (0, 0)),
            pl.BlockSpec((1, LANES), lambda b, h, i: (0, h)),
        ] + w_specs,
        out_specs=[pl.BlockSpec((tq, LANES), lambda b, h, i: (b * nq + i, h))] + w_specs,
        out_shape=[jax.ShapeDtypeStruct((T, DIFF_WIDTH), BF16)]
        + [jax.ShapeDtypeStruct(w.shape, BF16) for w in cast_weights],
        scratch_shapes=[pltpu.VMEM((2 * tq, LANES), BF16), pltpu.VMEM((tk, 2 * tq), F32),
                        pltpu.VMEM((F32_SUBLANES, 2 * tq), F32),
                        pltpu.VMEM((VT_ROWS, 2 * tq), F32)],
        compiler_params=pltpu.CompilerParams(
            dimension_semantics=("parallel", "parallel", "arbitrary"),
            vmem_limit_bytes=VMEM_LIMIT),
        name="diff_attn",
    )(dq, dk, dvt, lam_p, subln, gs_diff, *cast_weights)
    return outs[0], outs[1:]


def _mem_kv_kernel(mem_ref, gain_ref, w_ref, kg_ref, k_ref, v_ref):
    m = mem_ref[...]
    h = (m * _rms(m) * gain_ref[...]).astype(BF16)
    D = m.shape[1]
    dh = D // XATTN_HEADS
    k = jnp.dot(h, w_ref[:, :D], preferred_element_type=F32)
    v_ref[...] = jnp.dot(h, w_ref[:, D:], preferred_element_type=F32).astype(v_ref.dtype)
    for hh in range(XATTN_HEADS):
        kh = k[:, hh * dh:(hh + 1) * dh]
        k_ref[:, hh * dh:(hh + 1) * dh] = (kh * _rms(kh) * kg_ref[...]).astype(k_ref.dtype)


def _mem_kv(mem2, gain, xkv, kg, batch, mem_len):
    BM, D = mem2.shape
    const = lambda b: (0, 0)
    out = jax.ShapeDtypeStruct((BM, D), BF16)
    return pl.pallas_call(
        _mem_kv_kernel,
        grid=(batch,),
        in_specs=[
            pl.BlockSpec((mem_len, D), lambda b: (b, 0)),
            pl.BlockSpec((1, D), const),
            pl.BlockSpec((D, 2 * D), const),
            pl.BlockSpec((1, D // XATTN_HEADS), const),
        ],
        out_specs=[pl.BlockSpec((mem_len, D), lambda b: (b, 0))] * 2,
        out_shape=[out, out],
        compiler_params=pltpu.CompilerParams(
            dimension_semantics=("parallel",), vmem_limit_bytes=VMEM_LIMIT),
        name="mem_kv",
    )(mem2, gain, xkv, kg)


def _mix_xattn_kernel(x_ref, oret_ref, odiff_ref, wout_ref, nx_ref, xq_ref, qg_ref,
                      kmem_ref, vmem_ref, xo_ref, out_ref):
    D = x_ref.shape[1]
    dh = D // XATTN_HEADS
    x1 = (x_ref[...]
          + jnp.dot(oret_ref[...], wout_ref[:RET_WIDTH, :], preferred_element_type=F32)
          + jnp.dot(odiff_ref[...], wout_ref[RET_WIDTH:, :], preferred_element_type=F32))
    h = (x1 * _rms(x1) * nx_ref[...]).astype(BF16)
    q = jnp.dot(h, xq_ref[...], preferred_element_type=F32)
    nt = (((1,), (1,)), ((), ()))
    outs = []
    for hh in range(XATTN_HEADS):
        sl = slice(hh * dh, (hh + 1) * dh)
        qh = q[:, sl]
        qn = (qh * _rms(qh) * qg_ref[...] * (dh ** -0.5)).astype(BF16)
        s = lax.dot_general(qn, kmem_ref[:, sl], nt, preferred_element_type=F32)
        e = jnp.exp(s - jnp.max(s, axis=-1, keepdims=True))
        p = e * (1.0 / jnp.sum(e, axis=-1, keepdims=True))
        outs.append(jnp.dot(p.astype(BF16), vmem_ref[:, sl],
                            preferred_element_type=F32).astype(BF16))
    o = jnp.concatenate(outs, axis=1)
    out_ref[...] = x1 + jnp.dot(o, xo_ref[...], preferred_element_type=F32)


def _mix_xattn(x2, o_ret, o_diff, w_out, nx, xq, qg, kmem, vmem, xo, seq, mem_len, tm):
    T, D = x2.shape
    nblk = seq // tm
    tok = lambda i: (i, 0)
    const = lambda i: (0, 0)
    memb = lambda i: (i // nblk, 0)
    return pl.pallas_call(
        _mix_xattn_kernel,
        grid=(T // tm,),
        in_specs=[
            pl.BlockSpec((tm, D), tok),
            pl.BlockSpec((tm, RET_WIDTH), tok),
            pl.BlockSpec((tm, DIFF_WIDTH), tok),
            pl.BlockSpec((D, D), const),
            pl.BlockSpec((1, D), const),
            pl.BlockSpec((D, D), const),
            pl.BlockSpec((1, D // XATTN_HEADS), const),
            pl.BlockSpec((mem_len, D), memb),
            pl.BlockSpec((mem_len, D), memb),
            pl.BlockSpec((D, D), const),
        ],
        out_specs=pl.BlockSpec((tm, D), tok),
        out_shape=jax.ShapeDtypeStruct((T, D), F32),
        compiler_params=pltpu.CompilerParams(
            dimension_semantics=("parallel",), vmem_limit_bytes=VMEM_LIMIT),
        name="mix_xattn",
    )(x2, o_ret, o_diff, w_out, nx, xq, qg, kmem, vmem, xo)


def _ffn_kernel(x_ref, gain_ref, wg_ref, wu_ref, wd_ref, out_ref, *, hidden_chunk):
    x = x_ref[...]
    h = (x * _rms(x) * gain_ref[...]).astype(BF16)
    hidden = wg_ref.shape[1]
    acc = x
    for c in range(hidden // hidden_chunk):
        sl = slice(c * hidden_chunk, (c + 1) * hidden_chunk)
        g = jnp.dot(h, wg_ref[:, sl], preferred_element_type=F32)
        u = jnp.dot(h, wu_ref[:, sl], preferred_element_type=F32)
        a = (g * jax.nn.sigmoid(g) * u).astype(BF16)
        acc = acc + jnp.dot(a, wd_ref[sl, :], preferred_element_type=F32)
    out_ref[...] = acc


def _ffn(x2, gain, wg, wu, wd, tm, hidden_chunk):
    T, D = x2.shape
    hidden = wg.shape[1]
    tok = lambda i: (i, 0)
    const = lambda i: (0, 0)
    kern = functools.partial(_ffn_kernel, hidden_chunk=hidden_chunk)
    return pl.pallas_call(
        kern,
        grid=(T // tm,),
        in_specs=[
            pl.BlockSpec((tm, D), tok),
            pl.BlockSpec((1, D), const),
            pl.BlockSpec((D, hidden), const),
            pl.BlockSpec((D, hidden), const),
            pl.BlockSpec((hidden, D), const),
        ],
        out_specs=pl.BlockSpec((tm, D), tok),
        out_shape=jax.ShapeDtypeStruct((T, D), F32),
        compiler_params=pltpu.CompilerParams(
            dimension_semantics=("parallel",), vmem_limit_bytes=VMEM_LIMIT),
        name="ffn",
    )(x2, gain, wg, wu, wd)


def _rotary_tables(seq):
    half = RET_DK // 2
    inv = 1.0 / (RET_THETA_BASE ** np.linspace(0.0, 1.0, half))
    ang = np.arange(seq, dtype=np.float64)[:, None] * inv[None, :]
    cos, sin = np.cos(ang), np.sin(ang)
    cos_t = np.tile(cos, (1, LANES // half))
    sin_t = np.tile(np.concatenate([-sin, sin], axis=1), (1, LANES // RET_DK))
    return cos_t.astype(np.float32), sin_t.astype(np.float32)


def _decay_tables(chunk):
    H = RET_HEADS
    log_g = np.log1p(-(2.0 ** (-5.0 - np.arange(H, dtype=np.float64))))
    idx = np.arange(chunk, dtype=np.float64)
    rel = idx[:, None] - idx[None, :]
    dmask = np.where(rel >= 0, np.exp(log_g[:, None, None] * np.maximum(rel, 0.0)), 0.0)
    zeta = np.exp(log_g[:, None] * (chunk - 1.0 - idx))
    xi = np.exp(log_g[:, None] * (idx + 1.0))
    g_chunk = np.exp(log_g * chunk)
    widen = lambda t: np.repeat(t.T, RET_DK, axis=1)
    f32 = lambda t: t.astype(np.float32)
    dmask = dmask.reshape(H // 2, 2 * chunk, chunk)
    return (f32(dmask), f32(widen(zeta)), f32(widen(xi)),
            f32(np.repeat(g_chunk, RET_DK)[None, :]))


def kernel(x, mem, norm_mix, w_in, diff_q_gain, diff_k_gain, diff_lambda, diff_subln,
           group_scale, w_out, norm_x, norm_mem, xq, xkv, xq_gain, xk_gain, xo,
           norm_ffn, w_gate, w_up, w_down):
    B, S, D = x.shape
    M = mem.shape[1]
    depth = norm_mix.shape[0]
    x2 = x.reshape(B * S, D)
    mem2 = mem.reshape(B * M, D)

    ret_chunk = 256
    cos_t, sin_t = _rotary_tables(S)
    dmask, zeta, xi, gchunk = _decay_tables(ret_chunk)
    row = lambda a: a.reshape(1, -1)
    twice = lambda a: jnp.tile(a.reshape(1, -1), (1, 2))

    for l in range(depth):
        lam_init = 0.8 - 0.6 * math.exp(-0.3 * l)
        rq, rk, rv, rg, dq, dk, dv = _in_proj(
            x2, row(norm_mix[l]), w_in[l].astype(BF16), cos_t, sin_t,
            twice(diff_q_gain[l]), twice(diff_k_gain[l]), S, tm=1024)
        o_ret = _retention(rq, rk, rv, rg, dmask, zeta, xi, gchunk,
                           row(group_scale[l, :RET_WIDTH]), B, S, ts=1024, chunk=ret_chunk)
        later_weights = [w_out[l], xq[l], xkv[l], xo[l], w_gate[l], w_up[l], w_down[l]]
        o_diff, (w_out_b, xq_b, xkv_b, xo_b, w_gate_b, w_up_b, w_down_b) = _diff_attn(
            dq, dk, dv, diff_lambda[l], row(diff_subln[l]), row(group_scale[l, RET_WIDTH:]),
            later_weights, B, S, tq=1024, tk=512, rb=512, lam_init=lam_init)
        kmem, vmem = _mem_kv(mem2, row(norm_mem[l]), xkv_b, row(xk_gain[l]), B, M)
        x2 = _mix_xattn(x2, o_ret, o_diff, w_out_b, row(norm_x[l]), xq_b, row(xq_gain[l]),
                        kmem, vmem, xo_b, S, M, tm=1024)
        x2 = _ffn(x2, row(norm_ffn[l]), w_gate_b, w_up_b, w_down_b, tm=512, hidden_chunk=256)
    return x2.reshape(B, S, D)
```

```python
import functools
import math

import jax
import jax.numpy as jnp
import numpy as np
from jax import lax
from jax.experimental import pallas as pl
from jax.experimental.pallas import tpu as pltpu

F32 = jnp.float32
BF16 = jnp.bfloat16

EPS = 1e-6
LANES = 128
VMEM_LIMIT = 56 * 1024 * 1024

RET_HEADS = 8
RET_DK = 64
RET_WIDTH = 512
RET_THETA_BASE = 10000.0
DIFF_HEADS = 4
DIFF_DQK = 64
DIFF_DV = 128
DIFF_WIDTH = 512
XATTN_HEADS = 4
SEG = 512
N_SEG = 7
LOG2E = math.log2(math.e)
F32_SUBLANES = 8
BF16_SUBLANES = 16
VT_ROWS = DIFF_DV + BF16_SUBLANES


def _rms(x, axis=-1):
    return lax.rsqrt(jnp.mean(x * x, axis=axis, keepdims=True) + EPS)


def _low_half_mask(shape):
    lane = lax.broadcasted_iota(jnp.int32, shape, len(shape) - 1)
    return (lane % LANES) < (LANES // 2)


def _in_proj_kernel(x_ref, gain_ref, w_ref, cos_ref, sin_ref, qg_ref, kg_ref,
                    rq_ref, rk_ref, rv_ref, rg_ref, dq_ref, dk_ref, dvt_ref):
    x = x_ref[...]
    h = (x * _rms(x) * gain_ref[...]).astype(BF16)

    def proj(seg):
        return jnp.dot(h, w_ref[:, seg * SEG:(seg + 1) * SEG], preferred_element_type=F32)

    cos = cos_ref[...]
    sin = sin_ref[...]
    low = _low_half_mask(cos.shape)
    first32 = (lax.broadcasted_iota(jnp.int32, cos.shape, 1) % 64) < 32

    def rotary_store(y, out_ref, scale):
        for j in range(SEG // LANES):
            ys = y[:, j * LANES:(j + 1) * LANES]
            partner = jnp.where(first32, pltpu.roll(ys, LANES - 32, 1), pltpu.roll(ys, 32, 1))
            r = ys * cos + partner * sin
            if scale != 1.0:
                r = r * scale
            out_ref[:, j * LANES:(j + 1) * LANES] = r.astype(out_ref.dtype)

    def qknorm_store(y, g_ref, out_ref, scale):
        g = g_ref[...]
        for j in range(SEG // LANES):
            ys = y[:, j * LANES:(j + 1) * LANES]
            sq = ys * ys
            s_lo = jnp.sum(jnp.where(low, sq, 0.0), axis=-1, keepdims=True)
            s_hi = jnp.sum(jnp.where(low, 0.0, sq), axis=-1, keepdims=True)
            inv = jnp.where(low, lax.rsqrt(s_lo * (1.0 / DIFF_DQK) + EPS),
                            lax.rsqrt(s_hi * (1.0 / DIFF_DQK) + EPS))
            r = ys * inv * g
            if scale != 1.0:
                r = r * scale
            out_ref[:, j * LANES:(j + 1) * LANES] = r.astype(out_ref.dtype)

    rotary_store(proj(0), rq_ref, 1.0)
    rotary_store(proj(1), rk_ref, RET_DK ** -0.5)
    rv_ref[...] = proj(2).astype(BF16)
    g = proj(3)
    rg_ref[...] = (g * jax.nn.sigmoid(g)).astype(BF16)
    qknorm_store(proj(4), qg_ref, dq_ref, DIFF_DQK ** -0.5 * LOG2E)
    qknorm_store(proj(5), kg_ref, dk_ref, 1.0)
    dv = proj(6)
    ones = jnp.ones((VT_ROWS - DIFF_DV, dv.shape[0]), BF16)
    for hh in range(DIFF_HEADS):
        dvt_ref[hh * VT_ROWS:hh * VT_ROWS + DIFF_DV, :] = (
            dv[:, hh * DIFF_DV:(hh + 1) * DIFF_DV].T.astype(BF16))
        dvt_ref[hh * VT_ROWS + DIFF_DV:(hh + 1) * VT_ROWS, :] = ones


def _in_proj(x2, gain, w_in, cos_t, sin_t, qg, kg, seq, tm):
    T, D = x2.shape
    n_pos_blocks = seq // tm
    tok = lambda i: (i, 0)
    const = lambda i: (0, 0)
    pos = lambda i: (i % n_pos_blocks, 0)
    out = jax.ShapeDtypeStruct((T, SEG), BF16)
    return pl.pallas_call(
        _in_proj_kernel,
        grid=(T // tm,),
        in_specs=[
            pl.BlockSpec((tm, D), tok),
            pl.BlockSpec((1, D), const),
            pl.BlockSpec((D, N_SEG * SEG), const),
            pl.BlockSpec((tm, LANES), pos),
            pl.BlockSpec((tm, LANES), pos),
            pl.BlockSpec((1, LANES), const),
            pl.BlockSpec((1, LANES), const),
        ],
        out_specs=[pl.BlockSpec((tm, SEG), tok)] * (N_SEG - 1)
        + [pl.BlockSpec((DIFF_HEADS * VT_ROWS, tm), lambda i: (0, i))],
        out_shape=[out] * (N_SEG - 1)
        + [jax.ShapeDtypeStruct((DIFF_HEADS * VT_ROWS, T), BF16)],
        compiler_params=pltpu.CompilerParams(
            dimension_semantics=("parallel",), vmem_limit_bytes=VMEM_LIMIT),
        name="in_proj",
    )(x2, gain, w_in, cos_t, sin_t, qg, kg)


def _retention_kernel(q_ref, k_ref, v_ref, g_ref, dmask_ref, zeta_ref, xi_ref, gchunk_ref,
                      gs_ref, o_ref, state_ref, *, chunk, n_chunks):
    @pl.when(pl.program_id(1) == 0)
    def _():
        state_ref[...] = jnp.zeros_like(state_ref)

    n_slabs = RET_WIDTH // LANES
    low = _low_half_mask((chunk, LANES))
    row = lax.broadcasted_iota(jnp.int32, (LANES, LANES), 0)
    col = lax.broadcasted_iota(jnp.int32, (LANES, LANES), 1)
    blockdiag = (row < 64) == (col < 64)
    nt = (((1,), (1,)), ((), ()))
    tn = (((0,), (0,)), ((), ()))

    for c in range(n_chunks):
        rows = slice(c * chunk, (c + 1) * chunk)
        for s in range(n_slabs):
            sl = slice(s * LANES, (s + 1) * LANES)
            zeta = zeta_ref[:, sl]
            xi = xi_ref[:, sl]
            gch = gchunk_ref[:, sl]
            q = q_ref[rows, sl]
            k = k_ref[rows, sl]
            v = v_ref[rows, sl]
            zero = jnp.zeros_like(q)
            q2 = jnp.concatenate([jnp.where(low, q, zero), jnp.where(low, zero, q)], axis=0)
            s2 = lax.dot_general(q2, k, nt, preferred_element_type=F32) * dmask_ref[s]
            o2 = jnp.dot(s2.astype(BF16), v, preferred_element_type=F32)
            r_prev = state_ref[s]
            o_inter = jnp.dot(q, r_prev.astype(BF16), preferred_element_type=F32) * xi
            o = jnp.where(low, o2[:chunk], o2[chunk:]) + o_inter
            kz = (k.astype(F32) * zeta).astype(BF16)
            kv = lax.dot_general(kz, v, tn, preferred_element_type=F32)
            state_ref[s] = r_prev * gch + jnp.where(blockdiag, kv, 0.0)
            sq = o * o
            m_lo = jnp.sum(jnp.where(low, sq, 0.0), axis=-1, keepdims=True)
            m_hi = jnp.sum(jnp.where(low, 0.0, sq), axis=-1, keepdims=True)
            inv = jnp.where(low, lax.rsqrt(m_lo * (1.0 / 64) + EPS),
                            lax.rsqrt(m_hi * (1.0 / 64) + EPS))
            out = o * inv * g_ref[rows, sl].astype(F32) * gs_ref[:, sl]
            o_ref[rows, sl] = out.astype(o_ref.dtype)


def _retention(rq, rk, rv, rg, dmask, zeta, xi, gchunk, gs_ret, batch, seq, ts, chunk):
    T = rq.shape[0]
    nblk = seq // ts
    tok = lambda b, i: (b * nblk + i, 0)
    c2 = lambda b, i: (0, 0)
    c3 = lambda b, i: (0, 0, 0)
    kern = functools.partial(_retention_kernel, chunk=chunk, n_chunks=ts // chunk)
    return pl.pallas_call(
        kern,
        grid=(batch, nblk),
        in_specs=[pl.BlockSpec((ts, RET_WIDTH), tok)] * 4 + [
            pl.BlockSpec((RET_HEADS // 2, 2 * chunk, chunk), c3),
            pl.BlockSpec((chunk, RET_WIDTH), c2),
            pl.BlockSpec((chunk, RET_WIDTH), c2),
            pl.BlockSpec((1, RET_WIDTH), c2),
            pl.BlockSpec((1, RET_WIDTH), c2),
        ],
        out_specs=pl.BlockSpec((ts, RET_WIDTH), tok),
        out_shape=jax.ShapeDtypeStruct((T, RET_WIDTH), BF16),
        scratch_shapes=[pltpu.VMEM((RET_WIDTH // LANES, LANES, LANES), F32)],
        compiler_params=pltpu.CompilerParams(
            dimension_semantics=("parallel", "arbitrary"), vmem_limit_bytes=VMEM_LIMIT),
        name="retention",
    )(rq, rk, rv, rg, dmask, zeta, xi, gchunk, gs_ret)


def _diff_attn_kernel(q_ref, k_ref, vt_ref, lam_ref, subln_ref, gs_ref, *refs,
                      n_cast, tq, tk, rb, lam_init):
    w_refs = refs[:n_cast]
    o_ref = refs[n_cast]
    wb_refs = refs[n_cast + 1:2 * n_cast + 1]
    qs_ref, s_ref, mc_ref, m_ref, acc_ref = refs[2 * n_cast + 1:]
    i = pl.program_id(2)
    q = q_ref[...]
    low = _low_half_mask(q.shape)
    zero = jnp.zeros_like(q)
    qs_ref[:tq, :] = jnp.where(low, q, zero)
    qs_ref[tq:, :] = jnp.where(low, zero, q)
    nt = (((1,), (1,)), ((), ()))

    m_ref[...] = jnp.full_like(m_ref, -jnp.inf)
    acc_ref[...] = jnp.zeros_like(acc_ref)

    n_blocks = 2 * tq // rb
    n_diag = tq // tk
    n_full = i * n_diag

    def diag_kind(d, b):
        q_first = (b * rb) % tq
        if d * tk > q_first + rb - 1:
            return "skip"
        return "full" if (d + 1) * tk - 1 <= q_first else "mask"

    def scores(j, b, key_offset=None):
        start = pl.multiple_of(j * tk, tk)
        cols = slice(b * rb, (b + 1) * rb)
        s = lax.dot_general(k_ref[pl.ds(start, tk), :], qs_ref[cols, :], nt,
                            preferred_element_type=F32)
        if key_offset is not None:
            kpos = lax.broadcasted_iota(jnp.int32, s.shape, 0) + key_offset
            qpos = lax.broadcasted_iota(jnp.int32, s.shape, 1) + (b * rb) % tq
            s = jnp.where(kpos <= qpos, s, -jnp.inf)
        s_ref[:, cols] = s
        mc_ref[:, cols] = jnp.broadcast_to(jnp.max(s, axis=0, keepdims=True),
                                           (F32_SUBLANES, rb))

    def consume(j, b):
        start = pl.multiple_of(j * tk, tk)
        cols = slice(b * rb, (b + 1) * rb)
        m_prev = m_ref[:, cols]
        m_new = jnp.maximum(m_prev, mc_ref[:, cols])
        alpha = jnp.exp2(m_prev - m_new)
        p = jnp.exp2(s_ref[:, cols] - m_new[0:1, :])
        pv = jnp.dot(vt_ref[:, pl.ds(start, tk)], p.astype(BF16),
                     preferred_element_type=F32)
        acc_ref[:, cols] = alpha[0:1, :] * acc_ref[:, cols] + pv
        m_ref[:, cols] = m_new

    ahead = n_blocks // 2
    order = sorted(range(n_blocks), key=lambda b: -((b * rb) % tq))
    assert all(diag_kind(0, b) == "full" for b in order[:ahead])
    for b in order[:ahead]:
        scores(0, b)

    def full_chunks(j0, count):
        for u in range(count):
            j = j0 + u
            for pos, b in enumerate(order):
                if pos + ahead < n_blocks:
                    scores(j, order[pos + ahead])
                else:
                    scores(j + 1, order[pos + ahead - n_blocks])
                consume(j, b)

    def body(t, carry):
        full_chunks(t * (2 * n_diag), 2 * n_diag)
        return carry

    lax.fori_loop(0, i // 2, body, 0)

    @pl.when(i % 2 == 1)
    def _():
        full_chunks((i - 1) * n_diag, n_diag)

    for d in range(n_diag):
        for pos, b in enumerate(order):
            nd, npos = ((d, pos + ahead) if pos + ahead < n_blocks
                        else (d + 1, pos + ahead - n_blocks))
            if nd < n_diag and diag_kind(nd, order[npos]) != "skip":
                masked = diag_kind(nd, order[npos]) == "mask"
                scores(n_full + nd, order[npos], nd * tk if masked else None)
            if diag_kind(d, b) != "skip":
                consume(n_full + d, b)

    lv = lam_ref[...]
    lam = (jnp.exp(jnp.sum(lv[0:1] * lv[1:2], axis=-1, keepdims=True))
           - jnp.exp(jnp.sum(lv[2:3] * lv[3:4], axis=-1, keepdims=True)) + lam_init)
    acc = acc_ref[...]
    o_all = acc[:DIFF_DV, :] * (1.0 / acc[DIFF_DV:DIFF_DV + 1, :])
    o = (o_all[:, :tq] - lam * o_all[:, tq:]).T
    y = o * _rms(o) * subln_ref[...] * (1.0 - lam_init)
    o_ref[...] = (y * gs_ref[...]).astype(o_ref.dtype)

    for w_ref, wb_ref in zip(w_refs, wb_refs):
        wb_ref[...] = w_ref[...].astype(wb_ref.dtype)


def _diff_attn(dq, dk, dvt, lam_p, subln, gs_diff, cast_weights, batch, seq, tq, tk, rb,
               lam_init):
    T = dq.shape[0]
    nq = seq // tq
    n_steps = batch * DIFF_HEADS * nq
    assert seq % tq == 0 and tq % tk == 0 and tq % rb == 0

    def cast_spec(w):
        rows = w.shape[0]
        n_blk = max(n for n in range(1, n_steps + 1)
                    if n_steps % n == 0 and rows % (n * BF16_SUBLANES) == 0)
        hold = n_steps // n_blk
        return pl.BlockSpec(
            (rows // n_blk, w.shape[1]),
            lambda b, h, i: (((b * DIFF_HEADS + h) * nq + i) // hold, 0))

    w_specs = [cast_spec(w) for w in cast_weights]
    kern = functools.partial(_diff_attn_kernel, n_cast=len(cast_weights), tq=tq, tk=tk, rb=rb,
                             lam_init=lam_init)
    outs = pl.pallas_call(
        kern,
        grid=(batch, DIFF_HEADS, nq),
        in_specs=[
            pl.BlockSpec((tq, LANES), lambda b, h, i: (b * nq + i, h)),
            pl.BlockSpec((seq, LANES), lambda b, h, i: (b, h)),
            pl.BlockSpec((VT_ROWS, seq), lambda b, h, i: (h, b)),
            pl.BlockSpec((4, DIFF_DQK), lambda b, h, i: (0, 0)),
            pl.BlockSpec((1, LANES), lambda b, h, i: (0, 0)),
            pl.BlockSpec((1, LANES), lambda b, h, i: (0, h)),
        ] + w_specs,
        out_specs=[pl.BlockSpec((tq, LANES), lambda b, h, i: (b * nq + i, h))] + w_specs,
        out_shape=[jax.ShapeDtypeStruct((T, DIFF_WIDTH), BF16)]
        + [jax.ShapeDtypeStruct(w.shape, BF16) for w in cast_weights],
        scratch_shapes=[pltpu.VMEM((2 * tq, LANES), BF16), pltpu.VMEM((tk, 2 * tq), F32),
                        pltpu.VMEM((F32_SUBLANES, 2 * tq), F32),
                        pltpu.VMEM((F32_SUBLANES, 2 * tq), F32),
                        pltpu.VMEM((VT_ROWS, 2 * tq), F32)],
        compiler_params=pltpu.CompilerParams(
            dimension_semantics=("parallel", "parallel", "arbitrary"),
            vmem_limit_bytes=VMEM_LIMIT),
        name="diff_attn",
    )(dq, dk, dvt, lam_p, subln, gs_diff, *cast_weights)
    return outs[0], outs[1:]


def _mem_kv_kernel(mem_ref, gain_ref, w_ref, kg_ref, k_ref, v_ref):
    m = mem_ref[...]
    h = (m * _rms(m) * gain_ref[...]).astype(BF16)
    D = m.shape[1]
    dh = D // XATTN_HEADS
    k = jnp.dot(h, w_ref[:, :D], preferred_element_type=F32)
    v_ref[...] = jnp.dot(h, w_ref[:, D:], preferred_element_type=F32).astype(v_ref.dtype)
    for hh in range(XATTN_HEADS):
        kh = k[:, hh * dh:(hh + 1) * dh]
        k_ref[:, hh * dh:(hh + 1) * dh] = (kh * _rms(kh) * kg_ref[...]).astype(k_ref.dtype)


def _mem_kv(mem2, gain, xkv, kg, batch, mem_len):
    BM, D = mem2.shape
    const = lambda b: (0, 0)
    out = jax.ShapeDtypeStruct((BM, D), BF16)
    return pl.pallas_call(
        _mem_kv_kernel,
        grid=(batch,),
        in_specs=[
            pl.BlockSpec((mem_len, D), lambda b: (b, 0)),
            pl.BlockSpec((1, D), const),
            pl.BlockSpec((D, 2 * D), const),
            pl.BlockSpec((1, D // XATTN_HEADS), const),
        ],
        out_specs=[pl.BlockSpec((mem_len, D), lambda b: (b, 0))] * 2,
        out_shape=[out, out],
        compiler_params=pltpu.CompilerParams(
            dimension_semantics=("parallel",), vmem_limit_bytes=VMEM_LIMIT),
        name="mem_kv",
    )(mem2, gain, xkv, kg)


def _mix_xattn_kernel(x_ref, oret_ref, odiff_ref, wout_ref, nx_ref, xq_ref, qg_ref,
                      kmem_ref, vmem_ref, xo_ref, out_ref):
    D = x_ref.shape[1]
    dh = D // XATTN_HEADS
    x1 = (x_ref[...]
          + jnp.dot(oret_ref[...], wout_ref[:RET_WIDTH, :], preferred_element_type=F32)
          + jnp.dot(odiff_ref[...], wout_ref[RET_WIDTH:, :], preferred_element_type=F32))
    h = (x1 * _rms(x1) * nx_ref[...]).astype(BF16)
    q = jnp.dot(h, xq_ref[...], preferred_element_type=F32)
    nt = (((1,), (1,)), ((), ()))
    outs = []
    for hh in range(XATTN_HEADS):
        sl = slice(hh * dh, (hh + 1) * dh)
        qh = q[:, sl]
        qn = (qh * _rms(qh) * qg_ref[...] * (dh ** -0.5)).astype(BF16)
        s = lax.dot_general(qn, kmem_ref[:, sl], nt, preferred_element_type=F32)
        e = jnp.exp(s - jnp.max(s, axis=-1, keepdims=True))
        p = e * (1.0 / jnp.sum(e, axis=-1, keepdims=True))
        outs.append(jnp.dot(p.astype(BF16), vmem_ref[:, sl],
                            preferred_element_type=F32).astype(BF16))
    o = jnp.concatenate(outs, axis=1)
    out_ref[...] = x1 + jnp.dot(o, xo_ref[...], preferred_element_type=F32)


def _mix_xattn(x2, o_ret, o_diff, w_out, nx, xq, qg, kmem, vmem, xo, seq, mem_len, tm):
    T, D = x2.shape
    nblk = seq // tm
    tok = lambda i: (i, 0)
    const = lambda i: (0, 0)
    memb = lambda i: (i // nblk, 0)
    return pl.pallas_call(
        _mix_xattn_kernel,
        grid=(T // tm,),
        in_specs=[
            pl.BlockSpec((tm, D), tok),
            pl.BlockSpec((tm, RET_WIDTH), tok),
            pl.BlockSpec((tm, DIFF_WIDTH), tok),
            pl.BlockSpec((D, D), const),
            pl.BlockSpec((1, D), const),
            pl.BlockSpec((D, D), const),
            pl.BlockSpec((1, D // XATTN_HEADS), const),
            pl.BlockSpec((mem_len, D), memb),
            pl.BlockSpec((mem_len, D), memb),
            pl.BlockSpec((D, D), const),
        ],
        out_specs=pl.BlockSpec((tm, D), tok),
        out_shape=jax.ShapeDtypeStruct((T, D), F32),
        compiler_params=pltpu.CompilerParams(
            dimension_semantics=("parallel",), vmem_limit_bytes=VMEM_LIMIT),
        name="mix_xattn",
    )(x2, o_ret, o_diff, w_out, nx, xq, qg, kmem, vmem, xo)


def _ffn_kernel(x_ref, gain_ref, wg_ref, wu_ref, wd_ref, out_ref, *, hidden_chunk):
    x = x_ref[...]
    h = (x * _rms(x) * gain_ref[...]).astype(BF16)
    hidden = wg_ref.shape[1]
    acc = x
    for c in range(hidden // hidden_chunk):
        sl = slice(c * hidden_chunk, (c + 1) * hidden_chunk)
        g = jnp.dot(h, wg_ref[:, sl], preferred_element_type=F32)
        u = jnp.dot(h, wu_ref[:, sl], preferred_element_type=F32)
        a = (g * jax.nn.sigmoid(g) * u).astype(BF16)
        acc = acc + jnp.dot(a, wd_ref[sl, :], preferred_element_type=F32)
    out_ref[...] = acc


def _ffn(x2, gain, wg, wu, wd, tm, hidden_chunk):
    T, D = x2.shape
    hidden = wg.shape[1]
    tok = lambda i: (i, 0)
    const = lambda i: (0, 0)
    kern = functools.partial(_ffn_kernel, hidden_chunk=hidden_chunk)
    return pl.pallas_call(
        kern,
        grid=(T // tm,),
        in_specs=[
            pl.BlockSpec((tm, D), tok),
            pl.BlockSpec((1, D), const),
            pl.BlockSpec((D, hidden), const),
            pl.BlockSpec((D, hidden), const),
            pl.BlockSpec((hidden, D), const),
        ],
        out_specs=pl.BlockSpec((tm, D), tok),
        out_shape=jax.ShapeDtypeStruct((T, D), F32),
        compiler_params=pltpu.CompilerParams(
            dimension_semantics=("parallel",), vmem_limit_bytes=VMEM_LIMIT),
        name="ffn",
    )(x2, gain, wg, wu, wd)


def _rotary_tables(seq):
    half = RET_DK // 2
    inv = 1.0 / (RET_THETA_BASE ** np.linspace(0.0, 1.0, half))
    ang = np.arange(seq, dtype=np.float64)[:, None] * inv[None, :]
    cos, sin = np.cos(ang), np.sin(ang)
    cos_t = np.tile(cos, (1, LANES // half))
    sin_t = np.tile(np.concatenate([-sin, sin], axis=1), (1, LANES // RET_DK))
    return cos_t.astype(np.float32), sin_t.astype(np.float32)


def _decay_tables(chunk):
    H = RET_HEADS
    log_g = np.log1p(-(2.0 ** (-5.0 - np.arange(H, dtype=np.float64))))
    idx = np.arange(chunk, dtype=np.float64)
    rel = idx[:, None] - idx[None, :]
    dmask = np.where(rel >= 0, np.exp(log_g[:, None, None] * np.maximum(rel, 0.0)), 0.0)
    zeta = np.exp(log_g[:, None] * (chunk - 1.0 - idx))
    xi = np.exp(log_g[:, None] * (idx + 1.0))
    g_chunk = np.exp(log_g * chunk)
    widen = lambda t: np.repeat(t.T, RET_DK, axis=1)
    f32 = lambda t: t.astype(np.float32)
    dmask = dmask.reshape(H // 2, 2 * chunk, chunk)
    return (f32(dmask), f32(widen(zeta)), f32(widen(xi)),
            f32(np.repeat(g_chunk, RET_DK)[None, :]))


def kernel(x, mem, norm_mix, w_in, diff_q_gain, diff_k_gain, diff_lambda, diff_subln,
           group_scale, w_out, norm_x, norm_mem, xq, xkv, xq_gain, xk_gain, xo,
           norm_ffn, w_gate, w_up, w_down):
    B, S, D = x.shape
    M = mem.shape[1]
    depth = norm_mix.shape[0]
    x2 = x.reshape(B * S, D)
    mem2 = mem.reshape(B * M, D)

    ret_chunk = 256
    cos_t, sin_t = _rotary_tables(S)
    dmask, zeta, xi, gchunk = _decay_tables(ret_chunk)
    row = lambda a: a.reshape(1, -1)
    twice = lambda a: jnp.tile(a.reshape(1, -1), (1, 2))

    for l in range(depth):
        lam_init = 0.8 - 0.6 * math.exp(-0.3 * l)
        rq, rk, rv, rg, dq, dk, dv = _in_proj(
            x2, row(norm_mix[l]), w_in[l].astype(BF16), cos_t, sin_t,
            twice(diff_q_gain[l]), twice(diff_k_gain[l]), S, tm=1024)
        o_ret = _retention(rq, rk, rv, rg, dmask, zeta, xi, gchunk,
                           row(group_scale[l, :RET_WIDTH]), B, S, ts=1024, chunk=ret_chunk)
        later_weights = [w_out[l], xq[l], xkv[l], xo[l], w_gate[l], w_up[l], w_down[l]]
        o_diff, (w_out_b, xq_b, xkv_b, xo_b, w_gate_b, w_up_b, w_down_b) = _diff_attn(
            dq, dk, dv, diff_lambda[l], row(diff_subln[l]), row(group_scale[l, RET_WIDTH:]),
            later_weights, B, S, tq=1024, tk=512, rb=512, lam_init=lam_init)
        kmem, vmem = _mem_kv(mem2, row(norm_mem[l]), xkv_b, row(xk_gain[l]), B, M)
        x2 = _mix_xattn(x2, o_ret, o_diff, w_out_b, row(norm_x[l]), xq_b, row(xq_gain[l]),
                        kmem, vmem, xo_b, S, M, tm=1024)
        x2 = _ffn(x2, row(norm_ffn[l]), w_gate_b, w_up_b, w_down_b, tm=512, hidden_chunk=256)
    return x2.reshape(B, S, D)
```

```python
import functools
import math

import jax
import jax.numpy as jnp
import numpy as np
from jax import lax
from jax.experimental import pallas as pl
from jax.experimental.pallas import tpu as pltpu

F32 = jnp.float32
BF16 = jnp.bfloat16

EPS = 1e-6
LANES = 128
VMEM_LIMIT = 56 * 1024 * 1024

RET_HEADS = 8
RET_DK = 64
RET_WIDTH = 512
RET_THETA_BASE = 10000.0
DIFF_HEADS = 4
DIFF_DQK = 64
DIFF_DV = 128
DIFF_WIDTH = 512
XATTN_HEADS = 4
SEG = 512
N_SEG = 7
LOG2E = math.log2(math.e)
F32_SUBLANES = 8
BF16_SUBLANES = 16
VT_ROWS = DIFF_DV + BF16_SUBLANES


def _rms(x, axis=-1):
    return lax.rsqrt(jnp.mean(x * x, axis=axis, keepdims=True) + EPS)


def _low_half_mask(shape):
    lane = lax.broadcasted_iota(jnp.int32, shape, len(shape) - 1)
    return (lane % LANES) < (LANES // 2)


def _in_proj_kernel(x_ref, gain_ref, w_ref, cos_ref, sin_ref, qg_ref, kg_ref,
                    rq_ref, rk_ref, rv_ref, rg_ref, dq_ref, dk_ref, dvt_ref):
    x = x_ref[...]
    h = (x * _rms(x) * gain_ref[...]).astype(BF16)

    def proj(seg):
        return jnp.dot(h, w_ref[:, seg * SEG:(seg + 1) * SEG], preferred_element_type=F32)

    cos = cos_ref[...]
    sin = sin_ref[...]
    low = _low_half_mask(cos.shape)
    first32 = (lax.broadcasted_iota(jnp.int32, cos.shape, 1) % 64) < 32

    def rotary_store(y, out_ref, scale):
        for j in range(SEG // LANES):
            ys = y[:, j * LANES:(j + 1) * LANES]
            partner = jnp.where(first32, pltpu.roll(ys, LANES - 32, 1), pltpu.roll(ys, 32, 1))
            r = ys * cos + partner * sin
            if scale != 1.0:
                r = r * scale
            out_ref[:, j * LANES:(j + 1) * LANES] = r.astype(out_ref.dtype)

    def qknorm_store(y, g_ref, out_ref, scale):
        g = g_ref[...]
        for j in range(SEG // LANES):
            ys = y[:, j * LANES:(j + 1) * LANES]
            sq = ys * ys
            s_lo = jnp.sum(jnp.where(low, sq, 0.0), axis=-1, keepdims=True)
            s_hi = jnp.sum(jnp.where(low, 0.0, sq), axis=-1, keepdims=True)
            inv = jnp.where(low, lax.rsqrt(s_lo * (1.0 / DIFF_DQK) + EPS),
                            lax.rsqrt(s_hi * (1.0 / DIFF_DQK) + EPS))
            r = ys * inv * g
            if scale != 1.0:
                r = r * scale
            out_ref[:, j * LANES:(j + 1) * LANES] = r.astype(out_ref.dtype)

    rotary_store(proj(0), rq_ref, 1.0)
    rotary_store(proj(1), rk_ref, RET_DK ** -0.5)
    rv_ref[...] = proj(2).astype(BF16)
    g = proj(3)
    rg_ref[...] = (g * jax.nn.sigmoid(g)).astype(BF16)
    qknorm_store(proj(4), qg_ref, dq_ref, DIFF_DQK ** -0.5 * LOG2E)
    qknorm_store(proj(5), kg_ref, dk_ref, 1.0)
    dv = proj(6)
    ones = jnp.ones((VT_ROWS - DIFF_DV, dv.shape[0]), BF16)
    for hh in range(DIFF_HEADS):
        dvt_ref[hh * VT_ROWS:hh * VT_ROWS + DIFF_DV, :] = (
            dv[:, hh * DIFF_DV:(hh + 1) * DIFF_DV].T.astype(BF16))
        dvt_ref[hh * VT_ROWS + DIFF_DV:(hh + 1) * VT_ROWS, :] = ones


def _in_proj(x2, gain, w_in, cos_t, sin_t, qg, kg, seq, tm):
    T, D = x2.shape
    n_pos_blocks = seq // tm
    tok = lambda i: (i, 0)
    const = lambda i: (0, 0)
    pos = lambda i: (i % n_pos_blocks, 0)
    out = jax.ShapeDtypeStruct((T, SEG), BF16)
    return pl.pallas_call(
        _in_proj_kernel,
        grid=(T // tm,),
        in_specs=[
            pl.BlockSpec((tm, D), tok),
            pl.BlockSpec((1, D), const),
            pl.BlockSpec((D, N_SEG * SEG), const),
            pl.BlockSpec((tm, LANES), pos),
            pl.BlockSpec((tm, LANES), pos),
            pl.BlockSpec((1, LANES), const),
            pl.BlockSpec((1, LANES), const),
        ],
        out_specs=[pl.BlockSpec((tm, SEG), tok)] * (N_SEG - 1)
        + [pl.BlockSpec((DIFF_HEADS * VT_ROWS, tm), lambda i: (0, i))],
        out_shape=[out] * (N_SEG - 1)
        + [jax.ShapeDtypeStruct((DIFF_HEADS * VT_ROWS, T), BF16)],
        compiler_params=pltpu.CompilerParams(
            dimension_semantics=("parallel",), vmem_limit_bytes=VMEM_LIMIT),
        name="in_proj",
    )(x2, gain, w_in, cos_t, sin_t, qg, kg)


def _retention_kernel(q_ref, k_ref, v_ref, g_ref, dmask_ref, zeta_ref, xi_ref, gchunk_ref,
                      gs_ref, o_ref, state_ref, *, chunk, n_chunks):
    @pl.when(pl.program_id(1) == 0)
    def _():
        state_ref[...] = jnp.zeros_like(state_ref)

    n_slabs = RET_WIDTH // LANES
    low = _low_half_mask((chunk, LANES))
    row = lax.broadcasted_iota(jnp.int32, (LANES, LANES), 0)
    col = lax.broadcasted_iota(jnp.int32, (LANES, LANES), 1)
    blockdiag = (row < 64) == (col < 64)
    nt = (((1,), (1,)), ((), ()))
    tn = (((0,), (0,)), ((), ()))

    for c in range(n_chunks):
        rows = slice(c * chunk, (c + 1) * chunk)
        for s in range(n_slabs):
            sl = slice(s * LANES, (s + 1) * LANES)
            zeta = zeta_ref[:, sl]
            xi = xi_ref[:, sl]
            gch = gchunk_ref[:, sl]
            q = q_ref[rows, sl]
            k = k_ref[rows, sl]
            v = v_ref[rows, sl]
            zero = jnp.zeros_like(q)
            q2 = jnp.concatenate([jnp.where(low, q, zero), jnp.where(low, zero, q)], axis=0)
            s2 = lax.dot_general(q2, k, nt, preferred_element_type=F32) * dmask_ref[s]
            o2 = jnp.dot(s2.astype(BF16), v, preferred_element_type=F32)
            r_prev = state_ref[s]
            o_inter = jnp.dot(q, r_prev.astype(BF16), preferred_element_type=F32) * xi
            o = jnp.where(low, o2[:chunk], o2[chunk:]) + o_inter
            kz = (k.astype(F32) * zeta).astype(BF16)
            kv = lax.dot_general(kz, v, tn, preferred_element_type=F32)
            state_ref[s] = r_prev * gch + jnp.where(blockdiag, kv, 0.0)
            sq = o * o
            m_lo = jnp.sum(jnp.where(low, sq, 0.0), axis=-1, keepdims=True)
            m_hi = jnp.sum(jnp.where(low, 0.0, sq), axis=-1, keepdims=True)
            inv = jnp.where(low, lax.rsqrt(m_lo * (1.0 / 64) + EPS),
                            lax.rsqrt(m_hi * (1.0 / 64) + EPS))
            out = o * inv * g_ref[rows, sl].astype(F32) * gs_ref[:, sl]
            o_ref[rows, sl] = out.astype(o_ref.dtype)


def _retention(rq, rk, rv, rg, dmask, zeta, xi, gchunk, gs_ret, batch, seq, ts, chunk):
    T = rq.shape[0]
    nblk = seq // ts
    tok = lambda b, i: (b * nblk + i, 0)
    c2 = lambda b, i: (0, 0)
    c3 = lambda b, i: (0, 0, 0)
    kern = functools.partial(_retention_kernel, chunk=chunk, n_chunks=ts // chunk)
    return pl.pallas_call(
        kern,
        grid=(batch, nblk),
        in_specs=[pl.BlockSpec((ts, RET_WIDTH), tok)] * 4 + [
            pl.BlockSpec((RET_HEADS // 2, 2 * chunk, chunk), c3),
            pl.BlockSpec((chunk, RET_WIDTH), c2),
            pl.BlockSpec((chunk, RET_WIDTH), c2),
            pl.BlockSpec((1, RET_WIDTH), c2),
            pl.BlockSpec((1, RET_WIDTH), c2),
        ],
        out_specs=pl.BlockSpec((ts, RET_WIDTH), tok),
        out_shape=jax.ShapeDtypeStruct((T, RET_WIDTH), BF16),
        scratch_shapes=[pltpu.VMEM((RET_WIDTH // LANES, LANES, LANES), F32)],
        compiler_params=pltpu.CompilerParams(
            dimension_semantics=("parallel", "arbitrary"), vmem_limit_bytes=VMEM_LIMIT),
        name="retention",
    )(rq, rk, rv, rg, dmask, zeta, xi, gchunk, gs_ret)


def _diff_attn_kernel(q_ref, k_ref, vt_ref, lam_ref, subln_ref, gs_ref, *refs,
                      n_cast, heads, tq, tk, rb, lam_init):
    w_refs = refs[:n_cast]
    o_ref = refs[n_cast]
    wb_refs = refs[n_cast + 1:2 * n_cast + 1]
    scratch = refs[2 * n_cast + 1:]
    for hh in range(heads):
        lanes = pl.ds(hh * LANES, LANES)
        _diff_attn_head(
            q_ref.at[:, lanes], k_ref.at[:, lanes],
            vt_ref.at[pl.ds(hh * VT_ROWS, VT_ROWS), :], lam_ref, subln_ref,
            gs_ref.at[:, lanes], o_ref.at[:, lanes],
            w_refs if hh == 0 else (), wb_refs if hh == 0 else (),
            *[ref.at[hh] for ref in scratch], tq=tq, tk=tk, rb=rb, lam_init=lam_init)


def _diff_attn_head(q_ref, k_ref, vt_ref, lam_ref, subln_ref, gs_ref, o_ref, w_refs, wb_refs,
                    qs_ref, s_ref, mc_ref, m_ref, acc_ref, *, tq, tk, rb, lam_init):
    i = pl.program_id(2)
    q = q_ref[...]
    low = _low_half_mask(q.shape)
    zero = jnp.zeros_like(q)
    qs_ref[:tq, :] = jnp.where(low, q, zero)
    qs_ref[tq:, :] = jnp.where(low, zero, q)
    nt = (((1,), (1,)), ((), ()))

    n_blocks = 2 * tq // rb
    n_diag = tq // tk
    n_full = i * n_diag

    def diag_kind(d, b):
        q_first = (b * rb) % tq
        if d * tk > q_first + rb - 1:
            return "skip"
        return "full" if (d + 1) * tk - 1 <= q_first else "mask"

    def scores(j, b, key_offset=None):
        start = pl.multiple_of(j * tk, tk)
        cols = slice(b * rb, (b + 1) * rb)
        s = lax.dot_general(k_ref[pl.ds(start, tk), :], qs_ref[cols, :], nt,
                            preferred_element_type=F32)
        if key_offset is not None:
            kpos = lax.broadcasted_iota(jnp.int32, s.shape, 0) + key_offset
            qpos = lax.broadcasted_iota(jnp.int32, s.shape, 1) + (b * rb) % tq
            s = jnp.where(kpos <= qpos, s, -jnp.inf)
        s_ref[:, cols] = s
        mc_ref[:, cols] = jnp.broadcast_to(jnp.max(s, axis=0, keepdims=True),
                                           (F32_SUBLANES, rb))

    def consume(j, b):
        start = pl.multiple_of(j * tk, tk)
        cols = slice(b * rb, (b + 1) * rb)
        m_prev = m_ref[:, cols]
        m_new = jnp.maximum(m_prev, mc_ref[:, cols])
        alpha = jnp.exp2(m_prev - m_new)
        p = jnp.exp2(s_ref[:, cols] - m_new[0:1, :])
        pv = jnp.dot(vt_ref[:, pl.ds(start, tk)], p.astype(BF16),
                     preferred_element_type=F32)
        acc_ref[:, cols] = alpha[0:1, :] * acc_ref[:, cols] + pv
        m_ref[:, cols] = m_new

    ahead = n_blocks // 2
    order = sorted(range(n_blocks), key=lambda b: -((b * rb) % tq))
    assert all(diag_kind(0, b) == "full" for b in order[:ahead])
    for b in order[:ahead]:
        scores(0, b)
    m_ref[...] = jnp.full_like(m_ref, -jnp.inf)
    acc_ref[...] = jnp.zeros_like(acc_ref)

    def full_chunks(j0, count):
        for u in range(count):
            j = j0 + u
            for pos, b in enumerate(order):
                if pos + ahead < n_blocks:
                    scores(j, order[pos + ahead])
                else:
                    scores(j + 1, order[pos + ahead - n_blocks])
                consume(j, b)

    def body(t, carry):
        full_chunks(t * (2 * n_diag), 2 * n_diag)
        return carry

    lax.fori_loop(0, i // 2, body, 0)

    @pl.when(i % 2 == 1)
    def _():
        full_chunks((i - 1) * n_diag, n_diag)

    lv = lam_ref[...]
    lam = (jnp.exp(jnp.sum(lv[0:1] * lv[1:2], axis=-1, keepdims=True))
           - jnp.exp(jnp.sum(lv[2:3] * lv[3:4], axis=-1, keepdims=True)) + lam_init)

    def finalize(d):
        a0 = acc_ref[:, d * tk:(d + 1) * tk]
        a1 = acc_ref[:, tq + d * tk:tq + (d + 1) * tk]
        o0 = a0[:DIFF_DV, :] * (1.0 / a0[DIFF_DV:DIFF_DV + 1, :])
        o1 = a1[:DIFF_DV, :] * (1.0 / a1[DIFF_DV:DIFF_DV + 1, :])
        o = (o0 - lam * o1).T
        y = o * _rms(o) * subln_ref[...] * (1.0 - lam_init)
        o_ref[d * tk:(d + 1) * tk, :] = (y * gs_ref[...]).astype(o_ref.dtype)

    for d in range(n_diag):
        for pos, b in enumerate(order):
            nd, npos = ((d, pos + ahead) if pos + ahead < n_blocks
                        else (d + 1, pos + ahead - n_blocks))
            if nd < n_diag and diag_kind(nd, order[npos]) != "skip":
                masked = diag_kind(nd, order[npos]) == "mask"
                scores(n_full + nd, order[npos], nd * tk if masked else None)
            if diag_kind(d, b) != "skip":
                consume(n_full + d, b)
        finalize(d)
        if d == 0:
            for w_ref, wb_ref in zip(w_refs, wb_refs):
                wb_ref[...] = w_ref[...].astype(wb_ref.dtype)


def _diff_attn(dq, dk, dvt, lam_p, subln, gs_diff, cast_weights, batch, seq, tq, tk, rb, heads,
               lam_init):
    T = dq.shape[0]
    nq = seq // tq
    n_groups = DIFF_HEADS // heads
    n_steps = batch * n_groups * nq
    assert seq % tq == 0 and tq % tk == 0 and tq % rb == 0 and DIFF_HEADS % heads == 0

    def cast_spec(w):
        rows = w.shape[0]
        n_blk = max(n for n in range(1, n_steps + 1)
                    if n_steps % n == 0 and rows % (n * BF16_SUBLANES) == 0)
        hold = n_steps // n_blk
        return pl.BlockSpec(
            (rows // n_blk, w.shape[1]),
            lambda b, g, i: (((b * n_groups + g) * nq + i) // hold, 0))

    w_specs = [cast_spec(w) for w in cast_weights]
    kern = functools.partial(_diff_attn_kernel, n_cast=len(cast_weights), heads=heads, tq=tq,
                             tk=tk, rb=rb, lam_init=lam_init)
    per_head = lambda shape, dtype: pltpu.VMEM((heads,) + shape, dtype)
    outs = pl.pallas_call(
        kern,
        grid=(batch, n_groups, nq),
        in_specs=[
            pl.BlockSpec((tq, heads * LANES), lambda b, g, i: (b * nq + i, g)),
            pl.BlockSpec((seq, heads * LANES), lambda b, g, i: (b, g)),
            pl.BlockSpec((heads * VT_ROWS, seq), lambda b, g, i: (g, b)),
            pl.BlockSpec((4, DIFF_DQK), lambda b, g, i: (0, 0)),
            pl.BlockSpec((1, LANES), lambda b, g, i: (0, 0)),
            pl.BlockSpec((1, heads * LANES), lambda b, g, i: (0, g)),
        ] + w_specs,
        out_specs=[pl.BlockSpec((tq, heads * LANES), lambda b, g, i: (b * nq + i, g))]
        + w_specs,
        out_shape=[jax.ShapeDtypeStruct((T, DIFF_WIDTH), BF16)]
        + [jax.ShapeDtypeStruct(w.shape, BF16) for w in cast_weights],
        scratch_shapes=[per_head((2 * tq, LANES), BF16), per_head((tk, 2 * tq), F32),
                        per_head((F32_SUBLANES, 2 * tq), F32),
                        per_head((F32_SUBLANES, 2 * tq), F32),
                        per_head((VT_ROWS, 2 * tq), F32)],
        compiler_params=pltpu.CompilerParams(
            dimension_semantics=("parallel", "parallel", "arbitrary"),
            vmem_limit_bytes=VMEM_LIMIT),
        name="diff_attn",
    )(dq, dk, dvt, lam_p, subln, gs_diff, *cast_weights)
    return outs[0], outs[1:]


def _mem_kv_kernel(mem_ref, gain_ref, w_ref, kg_ref, k_ref, v_ref):
    m = mem_ref[...]
    h = (m * _rms(m) * gain_ref[...]).astype(BF16)
    D = m.shape[1]
    dh = D // XATTN_HEADS
    k = jnp.dot(h, w_ref[:, :D], preferred_element_type=F32)
    v_ref[...] = jnp.dot(h, w_ref[:, D:], preferred_element_type=F32).astype(v_ref.dtype)
    for hh in range(XATTN_HEADS):
        kh = k[:, hh * dh:(hh + 1) * dh]
        k_ref[:, hh * dh:(hh + 1) * dh] = (kh * _rms(kh) * kg_ref[...]).astype(k_ref.dtype)


def _mem_kv(mem2, gain, xkv, kg, batch, mem_len):
    BM, D = mem2.shape
    const = lambda b: (0, 0)
    out = jax.ShapeDtypeStruct((BM, D), BF16)
    return pl.pallas_call(
        _mem_kv_kernel,
        grid=(batch,),
        in_specs=[
            pl.BlockSpec((mem_len, D), lambda b: (b, 0)),
            pl.BlockSpec((1, D), const),
            pl.BlockSpec((D, 2 * D), const),
            pl.BlockSpec((1, D // XATTN_HEADS), const),
        ],
        out_specs=[pl.BlockSpec((mem_len, D), lambda b: (b, 0))] * 2,
        out_shape=[out, out],
        compiler_params=pltpu.CompilerParams(
            dimension_semantics=("parallel",), vmem_limit_bytes=VMEM_LIMIT),
        name="mem_kv",
    )(mem2, gain, xkv, kg)


def _mix_xattn_kernel(x_ref, oret_ref, odiff_ref, wout_ref, nx_ref, xq_ref, qg_ref,
                      kmem_ref, vmem_ref, xo_ref, out_ref):
    D = x_ref.shape[1]
    dh = D // XATTN_HEADS
    x1 = (x_ref[...]
          + jnp.dot(oret_ref[...], wout_ref[:RET_WIDTH, :], preferred_element_type=F32)
          + jnp.dot(odiff_ref[...], wout_ref[RET_WIDTH:, :], preferred_element_type=F32))
    h = (x1 * _rms(x1) * nx_ref[...]).astype(BF16)
    q = jnp.dot(h, xq_ref[...], preferred_element_type=F32)
    nt = (((1,), (1,)), ((), ()))
    outs = []
    for hh in range(XATTN_HEADS):
        sl = slice(hh * dh, (hh + 1) * dh)
        qh = q[:, sl]
        qn = (qh * _rms(qh) * qg_ref[...] * (dh ** -0.5)).astype(BF16)
        s = lax.dot_general(qn, kmem_ref[:, sl], nt, preferred_element_type=F32)
        e = jnp.exp(s - jnp.max(s, axis=-1, keepdims=True))
        p = e * (1.0 / jnp.sum(e, axis=-1, keepdims=True))
        outs.append(jnp.dot(p.astype(BF16), vmem_ref[:, sl],
                            preferred_element_type=F32).astype(BF16))
    o = jnp.concatenate(outs, axis=1)
    out_ref[...] = x1 + jnp.dot(o, xo_ref[...], preferred_element_type=F32)


def _mix_xattn(x2, o_ret, o_diff, w_out, nx, xq, qg, kmem, vmem, xo, seq, mem_len, tm):
    T, D = x2.shape
    nblk = seq // tm
    tok = lambda i: (i, 0)
    const = lambda i: (0, 0)
    memb = lambda i: (i // nblk, 0)
    return pl.pallas_call(
        _mix_xattn_kernel,
        grid=(T // tm,),
        in_specs=[
            pl.BlockSpec((tm, D), tok),
            pl.BlockSpec((tm, RET_WIDTH), tok),
            pl.BlockSpec((tm, DIFF_WIDTH), tok),
            pl.BlockSpec((D, D), const),
            pl.BlockSpec((1, D), const),
            pl.BlockSpec((D, D), const),
            pl.BlockSpec((1, D // XATTN_HEADS), const),
            pl.BlockSpec((mem_len, D), memb),
            pl.BlockSpec((mem_len, D), memb),
            pl.BlockSpec((D, D), const),
        ],
        out_specs=pl.BlockSpec((tm, D), tok),
        out_shape=jax.ShapeDtypeStruct((T, D), F32),
        compiler_params=pltpu.CompilerParams(
            dimension_semantics=("parallel",), vmem_limit_bytes=VMEM_LIMIT),
        name="mix_xattn",
    )(x2, o_ret, o_diff, w_out, nx, xq, qg, kmem, vmem, xo)


def _ffn_kernel(x_ref, gain_ref, wg_ref, wu_ref, wd_ref, out_ref, *, hidden_chunk):
    x = x_ref[...]
    h = (x * _rms(x) * gain_ref[...]).astype(BF16)
    hidden = wg_ref.shape[1]
    acc = x
    for c in range(hidden // hidden_chunk):
        sl = slice(c * hidden_chunk, (c + 1) * hidden_chunk)
        g = jnp.dot(h, wg_ref[:, sl], preferred_element_type=F32)
        u = jnp.dot(h, wu_ref[:, sl], preferred_element_type=F32)
        a = (g * jax.nn.sigmoid(g) * u).astype(BF16)
        acc = acc + jnp.dot(a, wd_ref[sl, :], preferred_element_type=F32)
    out_ref[...] = acc


def _ffn(x2, gain, wg, wu, wd, tm, hidden_chunk):
    T, D = x2.shape
    hidden = wg.shape[1]
    tok = lambda i: (i, 0)
    const = lambda i: (0, 0)
    kern = functools.partial(_ffn_kernel, hidden_chunk=hidden_chunk)
    return pl.pallas_call(
        kern,
        grid=(T // tm,),
        in_specs=[
            pl.BlockSpec((tm, D), tok),
            pl.BlockSpec((1, D), const),
            pl.BlockSpec((D, hidden), const),
            pl.BlockSpec((D, hidden), const),
            pl.BlockSpec((hidden, D), const),
        ],
        out_specs=pl.BlockSpec((tm, D), tok),
        out_shape=jax.ShapeDtypeStruct((T, D), F32),
        compiler_params=pltpu.CompilerParams(
            dimension_semantics=("parallel",), vmem_limit_bytes=VMEM_LIMIT),
        name="ffn",
    )(x2, gain, wg, wu, wd)


def _rotary_tables(seq):
    half = RET_DK // 2
    inv = 1.0 / (RET_THETA_BASE ** np.linspace(0.0, 1.0, half))
    ang = np.arange(seq, dtype=np.float64)[:, None] * inv[None, :]
    cos, sin = np.cos(ang), np.sin(ang)
    cos_t = np.tile(cos, (1, LANES // half))
    sin_t = np.tile(np.concatenate([-sin, sin], axis=1), (1, LANES // RET_DK))
    return cos_t.astype(np.float32), sin_t.astype(np.float32)


def _decay_tables(chunk):
    H = RET_HEADS
    log_g = np.log1p(-(2.0 ** (-5.0 - np.arange(H, dtype=np.float64))))
    idx = np.arange(chunk, dtype=np.float64)
    rel = idx[:, None] - idx[None, :]
    dmask = np.where(rel >= 0, np.exp(log_g[:, None, None] * np.maximum(rel, 0.0)), 0.0)
    zeta = np.exp(log_g[:, None] * (chunk - 1.0 - idx))
    xi = np.exp(log_g[:, None] * (idx + 1.0))
    g_chunk = np.exp(log_g * chunk)
    widen = lambda t: np.repeat(t.T, RET_DK, axis=1)
    f32 = lambda t: t.astype(np.float32)
    dmask = dmask.reshape(H // 2, 2 * chunk, chunk)
    return (f32(dmask), f32(widen(zeta)), f32(widen(xi)),
            f32(np.repeat(g_chunk, RET_DK)[None, :]))


def kernel(x, mem, norm_mix, w_in, diff_q_gain, diff_k_gain, diff_lambda, diff_subln,
           group_scale, w_out, norm_x, norm_mem, xq, xkv, xq_gain, xk_gain, xo,
           norm_ffn, w_gate, w_up, w_down):
    B, S, D = x.shape
    M = mem.shape[1]
    depth = norm_mix.shape[0]
    x2 = x.reshape(B * S, D)
    mem2 = mem.reshape(B * M, D)

    ret_chunk = 256
    cos_t, sin_t = _rotary_tables(S)
    dmask, zeta, xi, gchunk = _decay_tables(ret_chunk)
    row = lambda a: a.reshape(1, -1)
    twice = lambda a: jnp.tile(a.reshape(1, -1), (1, 2))

    for l in range(depth):
        lam_init = 0.8 - 0.6 * math.exp(-0.3 * l)
        rq, rk, rv, rg, dq, dk, dv = _in_proj(
            x2, row(norm_mix[l]), w_in[l].astype(BF16), cos_t, sin_t,
            twice(diff_q_gain[l]), twice(diff_k_gain[l]), S, tm=1024)
        o_ret = _retention(rq, rk, rv, rg, dmask, zeta, xi, gchunk,
                           row(group_scale[l, :RET_WIDTH]), B, S, ts=1024, chunk=ret_chunk)
        later_weights = [w_out[l], xq[l], xkv[l], xo[l], w_gate[l], w_up[l], w_down[l]]
        o_diff, (w_out_b, xq_b, xkv_b, xo_b, w_gate_b, w_up_b, w_down_b) = _diff_attn(
            dq, dk, dv, diff_lambda[l], row(diff_subln[l]), row(group_scale[l, RET_WIDTH:]),
            later_weights, B, S, tq=1024, tk=512, rb=512, heads=2, lam_init=lam_init)
        kmem, vmem = _mem_kv(mem2, row(norm_mem[l]), xkv_b, row(xk_gain[l]), B, M)
        x2 = _mix_xattn(x2, o_ret, o_diff, w_out_b, row(norm_x[l]), xq_b, row(xq_gain[l]),
                        kmem, vmem, xo_b, S, M, tm=1024)
        x2 = _ffn(x2, row(norm_ffn[l]), w_gate_b, w_up_b, w_down_b, tm=512, hidden_chunk=256)
    return x2.reshape(B, S, D)
```

```python
import functools
import math

import jax
import jax.numpy as jnp
import numpy as np
from jax import lax
from jax.experimental import pallas as pl
from jax.experimental.pallas import tpu as pltpu

F32 = jnp.float32
BF16 = jnp.bfloat16

EPS = 1e-6
LANES = 128
VMEM_LIMIT = 56 * 1024 * 1024

RET_HEADS = 8
RET_DK = 64
RET_WIDTH = 512
RET_THETA_BASE = 10000.0
DIFF_HEADS = 4
DIFF_DQK = 64
DIFF_DV = 128
DIFF_WIDTH = 512
XATTN_HEADS = 4
SEG = 512
N_SEG = 7
LOG2E = math.log2(math.e)
F32_SUBLANES = 8
BF16_SUBLANES = 16
VT_ROWS = DIFF_DV + BF16_SUBLANES


def _rms(x, axis=-1):
    return lax.rsqrt(jnp.mean(x * x, axis=axis, keepdims=True) + EPS)


def _low_half_mask(shape):
    lane = lax.broadcasted_iota(jnp.int32, shape, len(shape) - 1)
    return (lane % LANES) < (LANES // 2)


def _in_proj_kernel(x_ref, gain_ref, w_ref, cos_ref, sin_ref, qg_ref, kg_ref,
                    rq_ref, rk_ref, rv_ref, rg_ref, dq_ref, dk_ref, dvt_ref, wb_ref):
    @pl.when(pl.program_id(0) == 0)
    def _():
        for seg in range(N_SEG):
            cols = slice(seg * SEG, (seg + 1) * SEG)
            wb_ref[:, cols] = w_ref[:, cols].astype(BF16)

    x = x_ref[...]
    h = (x * _rms(x) * gain_ref[...]).astype(BF16)

    def proj(seg):
        return jnp.dot(h, wb_ref[:, seg * SEG:(seg + 1) * SEG], preferred_element_type=F32)

    cos = cos_ref[...]
    sin = sin_ref[...]
    low = _low_half_mask(cos.shape)
    first32 = (lax.broadcasted_iota(jnp.int32, cos.shape, 1) % 64) < 32

    def rotary_store(y, out_ref, scale):
        for j in range(SEG // LANES):
            ys = y[:, j * LANES:(j + 1) * LANES]
            partner = jnp.where(first32, pltpu.roll(ys, LANES - 32, 1), pltpu.roll(ys, 32, 1))
            r = ys * cos + partner * sin
            if scale != 1.0:
                r = r * scale
            out_ref[:, j * LANES:(j + 1) * LANES] = r.astype(out_ref.dtype)

    def qknorm_store(y, g_ref, out_ref, scale):
        g = g_ref[...]
        for j in range(SEG // LANES):
            ys = y[:, j * LANES:(j + 1) * LANES]
            sq = ys * ys
            s_lo = jnp.sum(jnp.where(low, sq, 0.0), axis=-1, keepdims=True)
            s_hi = jnp.sum(jnp.where(low, 0.0, sq), axis=-1, keepdims=True)
            inv = jnp.where(low, lax.rsqrt(s_lo * (1.0 / DIFF_DQK) + EPS),
                            lax.rsqrt(s_hi * (1.0 / DIFF_DQK) + EPS))
            r = ys * inv * g
            if scale != 1.0:
                r = r * scale
            out_ref[:, j * LANES:(j + 1) * LANES] = r.astype(out_ref.dtype)

    rotary_store(proj(0), rq_ref, 1.0)
    rotary_store(proj(1), rk_ref, RET_DK ** -0.5)
    rv_ref[...] = proj(2).astype(BF16)
    g = proj(3)
    rg_ref[...] = (g * jax.nn.sigmoid(g)).astype(BF16)
    qknorm_store(proj(4), qg_ref, dq_ref, DIFF_DQK ** -0.5 * LOG2E)
    qknorm_store(proj(5), kg_ref, dk_ref, 1.0)
    dv = proj(6)
    ones = jnp.ones((VT_ROWS - DIFF_DV, dv.shape[0]), BF16)
    for hh in range(DIFF_HEADS):
        dvt_ref[hh * VT_ROWS:hh * VT_ROWS + DIFF_DV, :] = (
            dv[:, hh * DIFF_DV:(hh + 1) * DIFF_DV].T.astype(BF16))
        dvt_ref[hh * VT_ROWS + DIFF_DV:(hh + 1) * VT_ROWS, :] = ones


def _in_proj(x2, gain, w_in, cos_t, sin_t, qg, kg, seq, tm):
    T, D = x2.shape
    n_pos_blocks = seq // tm
    tok = lambda i: (i, 0)
    const = lambda i: (0, 0)
    pos = lambda i: (i % n_pos_blocks, 0)
    out = jax.ShapeDtypeStruct((T, SEG), BF16)
    return pl.pallas_call(
        _in_proj_kernel,
        grid=(T // tm,),
        in_specs=[
            pl.BlockSpec((tm, D), tok),
            pl.BlockSpec((1, D), const),
            pl.BlockSpec((D, N_SEG * SEG), const, pipeline_mode=pl.Buffered(1)),
            pl.BlockSpec((tm, LANES), pos),
            pl.BlockSpec((tm, LANES), pos),
            pl.BlockSpec((1, LANES), const),
            pl.BlockSpec((1, LANES), const),
        ],
        out_specs=[pl.BlockSpec((tm, SEG), tok)] * (N_SEG - 1)
        + [pl.BlockSpec((DIFF_HEADS * VT_ROWS, tm), lambda i: (0, i))],
        out_shape=[out] * (N_SEG - 1)
        + [jax.ShapeDtypeStruct((DIFF_HEADS * VT_ROWS, T), BF16)],
        scratch_shapes=[pltpu.VMEM((D, N_SEG * SEG), BF16)],
        compiler_params=pltpu.CompilerParams(
            dimension_semantics=("arbitrary",), vmem_limit_bytes=VMEM_LIMIT),
        name="in_proj",
    )(x2, gain, w_in, cos_t, sin_t, qg, kg)


def _retention_kernel(q_ref, k_ref, v_ref, g_ref, dmask_ref, zeta_ref, xi_ref, gchunk_ref,
                      gs_ref, o_ref, state_ref, *, chunk, n_chunks):
    @pl.when(pl.program_id(1) == 0)
    def _():
        state_ref[...] = jnp.zeros_like(state_ref)

    n_slabs = RET_WIDTH // LANES
    low = _low_half_mask((chunk, LANES))
    row = lax.broadcasted_iota(jnp.int32, (LANES, LANES), 0)
    col = lax.broadcasted_iota(jnp.int32, (LANES, LANES), 1)
    blockdiag = (row < 64) == (col < 64)
    nt = (((1,), (1,)), ((), ()))
    tn = (((0,), (0,)), ((), ()))

    for c in range(n_chunks):
        rows = slice(c * chunk, (c + 1) * chunk)
        for s in range(n_slabs):
            sl = slice(s * LANES, (s + 1) * LANES)
            zeta = zeta_ref[:, sl]
            xi = xi_ref[:, sl]
            gch = gchunk_ref[:, sl]
            q = q_ref[rows, sl]
            k = k_ref[rows, sl]
            v = v_ref[rows, sl]
            zero = jnp.zeros_like(q)
            q2 = jnp.concatenate([jnp.where(low, q, zero), jnp.where(low, zero, q)], axis=0)
            s2 = lax.dot_general(q2, k, nt, preferred_element_type=F32) * dmask_ref[s]
            o2 = jnp.dot(s2.astype(BF16), v, preferred_element_type=F32)
            r_prev = state_ref[s]
            o_inter = jnp.dot(q, r_prev.astype(BF16), preferred_element_type=F32) * xi
            o = jnp.where(low, o2[:chunk], o2[chunk:]) + o_inter
            kz = (k.astype(F32) * zeta).astype(BF16)
            kv = lax.dot_general(kz, v, tn, preferred_element_type=F32)
            state_ref[s] = r_prev * gch + jnp.where(blockdiag, kv, 0.0)
            sq = o * o
            m_lo = jnp.sum(jnp.where(low, sq, 0.0), axis=-1, keepdims=True)
            m_hi = jnp.sum(jnp.where(low, 0.0, sq), axis=-1, keepdims=True)
            inv = jnp.where(low, lax.rsqrt(m_lo * (1.0 / 64) + EPS),
                            lax.rsqrt(m_hi * (1.0 / 64) + EPS))
            out = o * inv * g_ref[rows, sl].astype(F32) * gs_ref[:, sl]
            o_ref[rows, sl] = out.astype(o_ref.dtype)


def _retention(rq, rk, rv, rg, dmask, zeta, xi, gchunk, gs_ret, batch, seq, ts, chunk):
    T = rq.shape[0]
    nblk = seq // ts
    tok = lambda b, i: (b * nblk + i, 0)
    c2 = lambda b, i: (0, 0)
    c3 = lambda b, i: (0, 0, 0)
    kern = functools.partial(_retention_kernel, chunk=chunk, n_chunks=ts // chunk)
    return pl.pallas_call(
        kern,
        grid=(batch, nblk),
        in_specs=[pl.BlockSpec((ts, RET_WIDTH), tok)] * 4 + [
            pl.BlockSpec((RET_HEADS // 2, 2 * chunk, chunk), c3),
            pl.BlockSpec((chunk, RET_WIDTH), c2),
            pl.BlockSpec((chunk, RET_WIDTH), c2),
            pl.BlockSpec((1, RET_WIDTH), c2),
            pl.BlockSpec((1, RET_WIDTH), c2),
        ],
        out_specs=pl.BlockSpec((ts, RET_WIDTH), tok),
        out_shape=jax.ShapeDtypeStruct((T, RET_WIDTH), BF16),
        scratch_shapes=[pltpu.VMEM((RET_WIDTH // LANES, LANES, LANES), F32)],
        compiler_params=pltpu.CompilerParams(
            dimension_semantics=("parallel", "arbitrary"), vmem_limit_bytes=VMEM_LIMIT),
        name="retention",
    )(rq, rk, rv, rg, dmask, zeta, xi, gchunk, gs_ret)


def _diff_attn_kernel(q_ref, k_ref, vt_ref, lam_ref, subln_ref, gs_ref, *refs,
                      n_cast, heads, tq, tk, rb, lam_init):
    w_refs = refs[:n_cast]
    o_ref = refs[n_cast]
    wb_refs = refs[n_cast + 1:2 * n_cast + 1]
    scratch = refs[2 * n_cast + 1:]
    for hh in range(heads):
        lanes = pl.ds(hh * LANES, LANES)
        _diff_attn_head(
            q_ref.at[:, lanes], k_ref.at[:, lanes],
            vt_ref.at[pl.ds(hh * VT_ROWS, VT_ROWS), :], lam_ref, subln_ref,
            gs_ref.at[:, lanes], o_ref.at[:, lanes],
            w_refs if hh == 0 else (), wb_refs if hh == 0 else (),
            *[ref.at[hh] for ref in scratch], tq=tq, tk=tk, rb=rb, lam_init=lam_init)


def _diff_attn_head(q_ref, k_ref, vt_ref, lam_ref, subln_ref, gs_ref, o_ref, w_refs, wb_refs,
                    qs_ref, s_ref, mc_ref, m_ref, acc_ref, *, tq, tk, rb, lam_init):
    i = pl.program_id(2)
    q = q_ref[...]
    low = _low_half_mask(q.shape)
    zero = jnp.zeros_like(q)
    qs_ref[:tq, :] = jnp.where(low, q, zero)
    qs_ref[tq:, :] = jnp.where(low, zero, q)
    nt = (((1,), (1,)), ((), ()))

    n_blocks = 2 * tq // rb
    n_diag = tq // tk
    n_full = i * n_diag

    def diag_kind(d, b):
        q_first = (b * rb) % tq
        if d * tk > q_first + rb - 1:
            return "skip"
        return "full" if (d + 1) * tk - 1 <= q_first else "mask"

    def scores(j, b, key_offset=None):
        start = pl.multiple_of(j * tk, tk)
        cols = slice(b * rb, (b + 1) * rb)
        s = lax.dot_general(k_ref[pl.ds(start, tk), :], qs_ref[cols, :], nt,
                            preferred_element_type=F32)
        if key_offset is not None:
            kpos = lax.broadcasted_iota(jnp.int32, s.shape, 0) + key_offset
            qpos = lax.broadcasted_iota(jnp.int32, s.shape, 1) + (b * rb) % tq
            s = jnp.where(kpos <= qpos, s, -jnp.inf)
        s_ref[:, cols] = s
        mc_ref[:, cols] = jnp.broadcast_to(jnp.max(s, axis=0, keepdims=True),
                                           (F32_SUBLANES, rb))

    def consume(j, b):
        start = pl.multiple_of(j * tk, tk)
        cols = slice(b * rb, (b + 1) * rb)
        m_prev = m_ref[:, cols]
        m_new = jnp.maximum(m_prev, mc_ref[:, cols])
        alpha = jnp.exp2(m_prev - m_new)
        p = jnp.exp2(s_ref[:, cols] - m_new[0:1, :])
        pv = jnp.dot(vt_ref[:, pl.ds(start, tk)], p.astype(BF16),
                     preferred_element_type=F32)
        acc_ref[:, cols] = alpha[0:1, :] * acc_ref[:, cols] + pv
        m_ref[:, cols] = m_new

    ahead = n_blocks // 2
    order = sorted(range(n_blocks), key=lambda b: -((b * rb) % tq))
    assert all(diag_kind(0, b) == "full" for b in order[:ahead])
    for b in order[:ahead]:
        scores(0, b)
    m_ref[...] = jnp.full_like(m_ref, -jnp.inf)
    acc_ref[...] = jnp.zeros_like(acc_ref)

    def full_chunks(j0, count):
        for u in range(count):
            j = j0 + u
            for pos, b in enumerate(order):
                if pos + ahead < n_blocks:
                    scores(j, order[pos + ahead])
                else:
                    scores(j + 1, order[pos + ahead - n_blocks])
                consume(j, b)

    def body(t, carry):
        full_chunks(t * (2 * n_diag), 2 * n_diag)
        return carry

    lax.fori_loop(0, i // 2, body, 0)

    @pl.when(i % 2 == 1)
    def _():
        full_chunks((i - 1) * n_diag, n_diag)

    lv = lam_ref[...]
    lam = (jnp.exp(jnp.sum(lv[0:1] * lv[1:2], axis=-1, keepdims=True))
           - jnp.exp(jnp.sum(lv[2:3] * lv[3:4], axis=-1, keepdims=True)) + lam_init)

    out_gain = subln_ref[...] * (1.0 - lam_init) * gs_ref[...]

    def finalize(d):
        a0 = acc_ref[:, d * tk:(d + 1) * tk]
        a1 = acc_ref[:, tq + d * tk:tq + (d + 1) * tk]
        o0 = a0[:DIFF_DV, :] * (1.0 / a0[DIFF_DV:DIFF_DV + 1, :])
        o1 = a1[:DIFF_DV, :] * (1.0 / a1[DIFF_DV:DIFF_DV + 1, :])
        o = (o0 - lam * o1).T
        o_ref[d * tk:(d + 1) * tk, :] = (o * _rms(o) * out_gain).astype(o_ref.dtype)

    for d in range(n_diag):
        for pos, b in enumerate(order):
            nd, npos = ((d, pos + ahead) if pos + ahead < n_blocks
                        else (d + 1, pos + ahead - n_blocks))
            if nd < n_diag and diag_kind(nd, order[npos]) != "skip":
                masked = diag_kind(nd, order[npos]) == "mask"
                scores(n_full + nd, order[npos], nd * tk if masked else None)
            if diag_kind(d, b) != "skip":
                consume(n_full + d, b)
        finalize(d)
        if d == 0:
            for w_ref, wb_ref in zip(w_refs, wb_refs):
                wb_ref[...] = w_ref[...].astype(wb_ref.dtype)


def _diff_attn(dq, dk, dvt, lam_p, subln, gs_diff, cast_weights, batch, seq, tq, tk, rb, heads,
               lam_init):
    T = dq.shape[0]
    nq = seq // tq
    n_groups = DIFF_HEADS // heads
    n_steps = batch * n_groups * nq
    assert seq % tq == 0 and tq % tk == 0 and tq % rb == 0 and DIFF_HEADS % heads == 0

    def cast_spec(w):
        rows = w.shape[0]
        n_blk = max(n for n in range(1, n_steps + 1)
                    if n_steps % n == 0 and rows % (n * BF16_SUBLANES) == 0)
        hold = n_steps // n_blk
        return pl.BlockSpec(
            (rows // n_blk, w.shape[1]),
            lambda b, g, i: (((b * n_groups + g) * nq + i) // hold, 0))

    w_specs = [cast_spec(w) for w in cast_weights]
    kern = functools.partial(_diff_attn_kernel, n_cast=len(cast_weights), heads=heads, tq=tq,
                             tk=tk, rb=rb, lam_init=lam_init)
    per_head = lambda shape, dtype: pltpu.VMEM((heads,) + shape, dtype)
    outs = pl.pallas_call(
        kern,
        grid=(batch, n_groups, nq),
        in_specs=[
            pl.BlockSpec((tq, heads * LANES), lambda b, g, i: (b * nq + i, g)),
            pl.BlockSpec((seq, heads * LANES), lambda b, g, i: (b, g)),
            pl.BlockSpec((heads * VT_ROWS, seq), lambda b, g, i: (g, b)),
            pl.BlockSpec((4, DIFF_DQK), lambda b, g, i: (0, 0)),
            pl.BlockSpec((1, LANES), lambda b, g, i: (0, 0)),
            pl.BlockSpec((1, heads * LANES), lambda b, g, i: (0, g)),
        ] + w_specs,
        out_specs=[pl.BlockSpec((tq, heads * LANES), lambda b, g, i: (b * nq + i, g))]
        + w_specs,
        out_shape=[jax.ShapeDtypeStruct((T, DIFF_WIDTH), BF16)]
        + [jax.ShapeDtypeStruct(w.shape, BF16) for w in cast_weights],
        scratch_shapes=[per_head((2 * tq, LANES), BF16), per_head((tk, 2 * tq), F32),
                        per_head((F32_SUBLANES, 2 * tq), F32),
                        per_head((F32_SUBLANES, 2 * tq), F32),
                        per_head((VT_ROWS, 2 * tq), F32)],
        compiler_params=pltpu.CompilerParams(
            dimension_semantics=("parallel", "parallel", "arbitrary"),
            vmem_limit_bytes=VMEM_LIMIT),
        name="diff_attn",
    )(dq, dk, dvt, lam_p, subln, gs_diff, *cast_weights)
    return outs[0], outs[1:]


def _mem_kv_kernel(mem_ref, gain_ref, w_ref, kg_ref, k_ref, v_ref):
    m = mem_ref[...]
    h = (m * _rms(m) * gain_ref[...]).astype(BF16)
    D = m.shape[1]
    dh = D // XATTN_HEADS
    k = jnp.dot(h, w_ref[:, :D], preferred_element_type=F32)
    v_ref[...] = jnp.dot(h, w_ref[:, D:], preferred_element_type=F32).astype(v_ref.dtype)
    for hh in range(XATTN_HEADS):
        kh = k[:, hh * dh:(hh + 1) * dh]
        k_ref[:, hh * dh:(hh + 1) * dh] = (kh * _rms(kh) * kg_ref[...]).astype(k_ref.dtype)


def _mem_kv(mem2, gain, xkv, kg, batch, mem_len):
    BM, D = mem2.shape
    const = lambda b: (0, 0)
    out = jax.ShapeDtypeStruct((BM, D), BF16)
    return pl.pallas_call(
        _mem_kv_kernel,
        grid=(batch,),
        in_specs=[
            pl.BlockSpec((mem_len, D), lambda b: (b, 0)),
            pl.BlockSpec((1, D), const),
            pl.BlockSpec((D, 2 * D), const),
            pl.BlockSpec((1, D // XATTN_HEADS), const),
        ],
        out_specs=[pl.BlockSpec((mem_len, D), lambda b: (b, 0))] * 2,
        out_shape=[out, out],
        compiler_params=pltpu.CompilerParams(
            dimension_semantics=("parallel",), vmem_limit_bytes=VMEM_LIMIT),
        name="mem_kv",
    )(mem2, gain, xkv, kg)


def _mix_xattn_kernel(x_ref, oret_ref, odiff_ref, wout_ref, nx_ref, xq_ref, qg_ref,
                      kmem_ref, vmem_ref, xo_ref, out_ref):
    D = x_ref.shape[1]
    dh = D // XATTN_HEADS
    x1 = (x_ref[...]
          + jnp.dot(oret_ref[...], wout_ref[:RET_WIDTH, :], preferred_element_type=F32)
          + jnp.dot(odiff_ref[...], wout_ref[RET_WIDTH:, :], preferred_element_type=F32))
    h = (x1 * _rms(x1) * nx_ref[...]).astype(BF16)
    q = jnp.dot(h, xq_ref[...], preferred_element_type=F32)
    nt = (((1,), (1,)), ((), ()))
    outs = []
    for hh in range(XATTN_HEADS):
        sl = slice(hh * dh, (hh + 1) * dh)
        qh = q[:, sl]
        qn = (qh * _rms(qh) * qg_ref[...] * (dh ** -0.5)).astype(BF16)
        s = lax.dot_general(qn, kmem_ref[:, sl], nt, preferred_element_type=F32)
        e = jnp.exp(s - jnp.max(s, axis=-1, keepdims=True))
        p = e * (1.0 / jnp.sum(e, axis=-1, keepdims=True))
        outs.append(jnp.dot(p.astype(BF16), vmem_ref[:, sl],
                            preferred_element_type=F32).astype(BF16))
    o = jnp.concatenate(outs, axis=1)
    out_ref[...] = x1 + jnp.dot(o, xo_ref[...], preferred_element_type=F32)


def _mix_xattn(x2, o_ret, o_diff, w_out, nx, xq, qg, kmem, vmem, xo, seq, mem_len, tm):
    T, D = x2.shape
    nblk = seq // tm
    tok = lambda i: (i, 0)
    const = lambda i: (0, 0)
    memb = lambda i: (i // nblk, 0)
    return pl.pallas_call(
        _mix_xattn_kernel,
        grid=(T // tm,),
        in_specs=[
            pl.BlockSpec((tm, D), tok),
            pl.BlockSpec((tm, RET_WIDTH), tok),
            pl.BlockSpec((tm, DIFF_WIDTH), tok),
            pl.BlockSpec((D, D), const),
            pl.BlockSpec((1, D), const),
            pl.BlockSpec((D, D), const),
            pl.BlockSpec((1, D // XATTN_HEADS), const),
            pl.BlockSpec((mem_len, D), memb),
            pl.BlockSpec((mem_len, D), memb),
            pl.BlockSpec((D, D), const),
        ],
        out_specs=pl.BlockSpec((tm, D), tok),
        out_shape=jax.ShapeDtypeStruct((T, D), F32),
        compiler_params=pltpu.CompilerParams(
            dimension_semantics=("parallel",), vmem_limit_bytes=VMEM_LIMIT),
        name="mix_xattn",
    )(x2, o_ret, o_diff, w_out, nx, xq, qg, kmem, vmem, xo)


def _ffn_kernel(x_ref, gain_ref, wg_ref, wu_ref, wd_ref, out_ref, *, hidden_chunk):
    x = x_ref[...]
    h = (x * _rms(x) * gain_ref[...]).astype(BF16)
    hidden = wg_ref.shape[1]
    acc = x
    for c in range(hidden // hidden_chunk):
        sl = slice(c * hidden_chunk, (c + 1) * hidden_chunk)
        g = jnp.dot(h, wg_ref[:, sl], preferred_element_type=F32)
        u = jnp.dot(h, wu_ref[:, sl], preferred_element_type=F32)
        a = (g * jax.nn.sigmoid(g) * u).astype(BF16)
        acc = acc + jnp.dot(a, wd_ref[sl, :], preferred_element_type=F32)
    out_ref[...] = acc


def _ffn(x2, gain, wg, wu, wd, tm, hidden_chunk):
    T, D = x2.shape
    hidden = wg.shape[1]
    tok = lambda i: (i, 0)
    const = lambda i: (0, 0)
    kern = functools.partial(_ffn_kernel, hidden_chunk=hidden_chunk)
    return pl.pallas_call(
        kern,
        grid=(T // tm,),
        in_specs=[
            pl.BlockSpec((tm, D), tok),
            pl.BlockSpec((1, D), const),
            pl.BlockSpec((D, hidden), const),
            pl.BlockSpec((D, hidden), const),
            pl.BlockSpec((hidden, D), const),
        ],
        out_specs=pl.BlockSpec((tm, D), tok),
        out_shape=jax.ShapeDtypeStruct((T, D), F32),
        compiler_params=pltpu.CompilerParams(
            dimension_semantics=("parallel",), vmem_limit_bytes=VMEM_LIMIT),
        name="ffn",
    )(x2, gain, wg, wu, wd)


def _rotary_tables(seq):
    half = RET_DK // 2
    inv = 1.0 / (RET_THETA_BASE ** np.linspace(0.0, 1.0, half))
    ang = np.arange(seq, dtype=np.float64)[:, None] * inv[None, :]
    cos, sin = np.cos(ang), np.sin(ang)
    cos_t = np.tile(cos, (1, LANES // half))
    sin_t = np.tile(np.concatenate([-sin, sin], axis=1), (1, LANES // RET_DK))
    return cos_t.astype(np.float32), sin_t.astype(np.float32)


def _decay_tables(chunk):
    H = RET_HEADS
    log_g = np.log1p(-(2.0 ** (-5.0 - np.arange(H, dtype=np.float64))))
    idx = np.arange(chunk, dtype=np.float64)
    rel = idx[:, None] - idx[None, :]
    dmask = np.where(rel >= 0, np.exp(log_g[:, None, None] * np.maximum(rel, 0.0)), 0.0)
    zeta = np.exp(log_g[:, None] * (chunk - 1.0 - idx))
    xi = np.exp(log_g[:, None] * (idx + 1.0))
    g_chunk = np.exp(log_g * chunk)
    widen = lambda t: np.repeat(t.T, RET_DK, axis=1)
    f32 = lambda t: t.astype(np.float32)
    dmask = dmask.reshape(H // 2, 2 * chunk, chunk)
    return (f32(dmask), f32(widen(zeta)), f32(widen(xi)),
            f32(np.repeat(g_chunk, RET_DK)[None, :]))


def kernel(x, mem, norm_mix, w_in, diff_q_gain, diff_k_gain, diff_lambda, diff_subln,
           group_scale, w_out, norm_x, norm_mem, xq, xkv, xq_gain, xk_gain, xo,
           norm_ffn, w_gate, w_up, w_down):
    B, S, D = x.shape
    M = mem.shape[1]
    depth = norm_mix.shape[0]
    x2 = x.reshape(B * S, D)
    mem2 = mem.reshape(B * M, D)

    ret_chunk = 256
    cos_t, sin_t = _rotary_tables(S)
    dmask, zeta, xi, gchunk = _decay_tables(ret_chunk)
    row = lambda a: a.reshape(1, -1)
    twice = lambda a: jnp.tile(a.reshape(1, -1), (1, 2))

    for l in range(depth):
        lam_init = 0.8 - 0.6 * math.exp(-0.3 * l)
        rq, rk, rv, rg, dq, dk, dv = _in_proj(
            x2, row(norm_mix[l]), w_in[l], cos_t, sin_t,
            twice(diff_q_gain[l]), twice(diff_k_gain[l]), S, tm=1024)
        o_ret = _retention(rq, rk, rv, rg, dmask, zeta, xi, gchunk,
                           row(group_scale[l, :RET_WIDTH]), B, S, ts=1024, chunk=ret_chunk)
        later_weights = [w_out[l], xq[l], xkv[l], xo[l], w_gate[l], w_up[l], w_down[l]]
        o_diff, (w_out_b, xq_b, xkv_b, xo_b, w_gate_b, w_up_b, w_down_b) = _diff_attn(
            dq, dk, dv, diff_lambda[l], row(diff_subln[l]), row(group_scale[l, RET_WIDTH:]),
            later_weights, B, S, tq=1024, tk=512, rb=512, heads=2, lam_init=lam_init)
        kmem, vmem = _mem_kv(mem2, row(norm_mem[l]), xkv_b, row(xk_gain[l]), B, M)
        x2 = _mix_xattn(x2, o_ret, o_diff, w_out_b, row(norm_x[l]), xq_b, row(xq_gain[l]),
                        kmem, vmem, xo_b, S, M, tm=1024)
        x2 = _ffn(x2, row(norm_ffn[l]), w_gate_b, w_up_b, w_down_b, tm=512, hidden_chunk=256)
    return x2.reshape(B, S, D)
```

```python
import functools
import math

import jax
import jax.numpy as jnp
import numpy as np
from jax import lax
from jax.experimental import pallas as pl
from jax.experimental.pallas import tpu as pltpu

F32 = jnp.float32
BF16 = jnp.bfloat16

EPS = 1e-6
LANES = 128
VMEM_LIMIT = 56 * 1024 * 1024

RET_HEADS = 8
RET_DK = 64
RET_WIDTH = 512
RET_THETA_BASE = 10000.0
DIFF_HEADS = 4
DIFF_DQK = 64
DIFF_DV = 128
DIFF_WIDTH = 512
XATTN_HEADS = 4
SEG = 512
N_SEG = 7
LOG2E = math.log2(math.e)
F32_SUBLANES = 8
BF16_SUBLANES = 16
VT_ROWS = DIFF_DV + BF16_SUBLANES


def _rms(x, axis=-1):
    return lax.rsqrt(jnp.mean(x * x, axis=axis, keepdims=True) + EPS)


def _low_half_mask(shape):
    lane = lax.broadcasted_iota(jnp.int32, shape, len(shape) - 1)
    return (lane % LANES) < (LANES // 2)


def _in_proj_kernel(x_ref, gain_ref, w_ref, cos_ref, sin_ref, qg_ref, kg_ref,
                    rq_ref, rk_ref, rv_ref, rg_ref, dq_ref, dk_ref, dvt_ref, wb_ref):
    @pl.when(pl.program_id(0) == 0)
    def _():
        for seg in range(N_SEG):
            cols = slice(seg * SEG, (seg + 1) * SEG)
            wb_ref[:, cols] = w_ref[:, cols].astype(BF16)

    x = x_ref[...]
    h = (x * _rms(x) * gain_ref[...]).astype(BF16)

    def proj(seg):
        return jnp.dot(h, wb_ref[:, seg * SEG:(seg + 1) * SEG], preferred_element_type=F32)

    cos = cos_ref[...]
    sin = sin_ref[...]
    low = _low_half_mask(cos.shape)
    first32 = (lax.broadcasted_iota(jnp.int32, cos.shape, 1) % 64) < 32

    def rotary_store(y, out_ref, scale):
        for j in range(SEG // LANES):
            ys = y[:, j * LANES:(j + 1) * LANES]
            partner = jnp.where(first32, pltpu.roll(ys, LANES - 32, 1), pltpu.roll(ys, 32, 1))
            r = ys * cos + partner * sin
            if scale != 1.0:
                r = r * scale
            out_ref[:, j * LANES:(j + 1) * LANES] = r.astype(out_ref.dtype)

    def qknorm_store(y, g_ref, out_ref, scale):
        g = g_ref[...]
        for j in range(SEG // LANES):
            ys = y[:, j * LANES:(j + 1) * LANES]
            sq = ys * ys
            s_lo = jnp.sum(jnp.where(low, sq, 0.0), axis=-1, keepdims=True)
            s_hi = jnp.sum(jnp.where(low, 0.0, sq), axis=-1, keepdims=True)
            inv = jnp.where(low, lax.rsqrt(s_lo * (1.0 / DIFF_DQK) + EPS),
                            lax.rsqrt(s_hi * (1.0 / DIFF_DQK) + EPS))
            r = ys * inv * g
            if scale != 1.0:
                r = r * scale
            out_ref[:, j * LANES:(j + 1) * LANES] = r.astype(out_ref.dtype)

    rotary_store(proj(0), rq_ref, 1.0)
    rotary_store(proj(1), rk_ref, RET_DK ** -0.5)
    rv_ref[...] = proj(2).astype(BF16)
    g = proj(3)
    rg_ref[...] = (g * jax.nn.sigmoid(g)).astype(BF16)
    qknorm_store(proj(4), qg_ref, dq_ref, DIFF_DQK ** -0.5 * LOG2E)
    qknorm_store(proj(5), kg_ref, dk_ref, 1.0)
    dv = proj(6)
    ones = jnp.ones((VT_ROWS - DIFF_DV, dv.shape[0]), BF16)
    for hh in range(DIFF_HEADS):
        dvt_ref[hh * VT_ROWS:hh * VT_ROWS + DIFF_DV, :] = (
            dv[:, hh * DIFF_DV:(hh + 1) * DIFF_DV].T.astype(BF16))
        dvt_ref[hh * VT_ROWS + DIFF_DV:(hh + 1) * VT_ROWS, :] = ones


def _in_proj(x2, gain, w_in, cos_t, sin_t, qg, kg, seq, tm):
    T, D = x2.shape
    n_pos_blocks = seq // tm
    tok = lambda i: (i, 0)
    const = lambda i: (0, 0)
    pos = lambda i: (i % n_pos_blocks, 0)
    out = jax.ShapeDtypeStruct((T, SEG), BF16)
    return pl.pallas_call(
        _in_proj_kernel,
        grid=(T // tm,),
        in_specs=[
            pl.BlockSpec((tm, D), tok),
            pl.BlockSpec((1, D), const),
            pl.BlockSpec((D, N_SEG * SEG), const, pipeline_mode=pl.Buffered(1)),
            pl.BlockSpec((tm, LANES), pos),
            pl.BlockSpec((tm, LANES), pos),
            pl.BlockSpec((1, LANES), const),
            pl.BlockSpec((1, LANES), const),
        ],
        out_specs=[pl.BlockSpec((tm, SEG), tok)] * (N_SEG - 1)
        + [pl.BlockSpec((DIFF_HEADS * VT_ROWS, tm), lambda i: (0, i))],
        out_shape=[out] * (N_SEG - 1)
        + [jax.ShapeDtypeStruct((DIFF_HEADS * VT_ROWS, T), BF16)],
        scratch_shapes=[pltpu.VMEM((D, N_SEG * SEG), BF16)],
        compiler_params=pltpu.CompilerParams(
            dimension_semantics=("arbitrary",), vmem_limit_bytes=VMEM_LIMIT),
        name="in_proj",
    )(x2, gain, w_in, cos_t, sin_t, qg, kg)


def _retention_kernel(q_ref, k_ref, v_ref, g_ref, dmask_ref, zeta_ref, xi_ref, gchunk_ref,
                      gs_ref, o_ref, state_ref, *, chunk, n_chunks):
    @pl.when(pl.program_id(1) == 0)
    def _():
        state_ref[...] = jnp.zeros_like(state_ref)

    n_slabs = RET_WIDTH // LANES
    low = _low_half_mask((chunk, LANES))
    row = lax.broadcasted_iota(jnp.int32, (LANES, LANES), 0)
    col = lax.broadcasted_iota(jnp.int32, (LANES, LANES), 1)
    blockdiag = (row < 64) == (col < 64)
    nt = (((1,), (1,)), ((), ()))
    tn = (((0,), (0,)), ((), ()))

    for c in range(n_chunks):
        rows = slice(c * chunk, (c + 1) * chunk)
        for s in range(n_slabs):
            sl = slice(s * LANES, (s + 1) * LANES)
            zeta = zeta_ref[:, sl]
            xi = xi_ref[:, sl]
            gch = gchunk_ref[:, sl]
            q = q_ref[rows, sl]
            k = k_ref[rows, sl]
            v = v_ref[rows, sl]
            zero = jnp.zeros_like(q)
            q2 = jnp.concatenate([jnp.where(low, q, zero), jnp.where(low, zero, q)], axis=0)
            s2 = lax.dot_general(q2, k, nt, preferred_element_type=F32) * dmask_ref[s]
            o2 = jnp.dot(s2.astype(BF16), v, preferred_element_type=F32)
            r_prev = state_ref[s]
            o_inter = jnp.dot(q, r_prev.astype(BF16), preferred_element_type=F32) * xi
            o = jnp.where(low, o2[:chunk], o2[chunk:]) + o_inter
            kz = (k.astype(F32) * zeta).astype(BF16)
            kv = lax.dot_general(kz, v, tn, preferred_element_type=F32)
            state_ref[s] = r_prev * gch + jnp.where(blockdiag, kv, 0.0)
            sq = o * o
            m_lo = jnp.sum(jnp.where(low, sq, 0.0), axis=-1, keepdims=True)
            m_hi = jnp.sum(jnp.where(low, 0.0, sq), axis=-1, keepdims=True)
            inv = jnp.where(low, lax.rsqrt(m_lo * (1.0 / 64) + EPS),
                            lax.rsqrt(m_hi * (1.0 / 64) + EPS))
            out = o * inv * g_ref[rows, sl].astype(F32) * gs_ref[:, sl]
            o_ref[rows, sl] = out.astype(o_ref.dtype)


def _retention(rq, rk, rv, rg, dmask, zeta, xi, gchunk, gs_ret, batch, seq, ts, chunk):
    T = rq.shape[0]
    nblk = seq // ts
    tok = lambda b, i: (b * nblk + i, 0)
    c2 = lambda b, i: (0, 0)
    c3 = lambda b, i: (0, 0, 0)
    kern = functools.partial(_retention_kernel, chunk=chunk, n_chunks=ts // chunk)
    return pl.pallas_call(
        kern,
        grid=(batch, nblk),
        in_specs=[pl.BlockSpec((ts, RET_WIDTH), tok)] * 4 + [
            pl.BlockSpec((RET_HEADS // 2, 2 * chunk, chunk), c3),
            pl.BlockSpec((chunk, RET_WIDTH), c2),
            pl.BlockSpec((chunk, RET_WIDTH), c2),
            pl.BlockSpec((1, RET_WIDTH), c2),
            pl.BlockSpec((1, RET_WIDTH), c2),
        ],
        out_specs=pl.BlockSpec((ts, RET_WIDTH), tok),
        out_shape=jax.ShapeDtypeStruct((T, RET_WIDTH), BF16),
        scratch_shapes=[pltpu.VMEM((RET_WIDTH // LANES, LANES, LANES), F32)],
        compiler_params=pltpu.CompilerParams(
            dimension_semantics=("parallel", "arbitrary"), vmem_limit_bytes=VMEM_LIMIT),
        name="retention",
    )(rq, rk, rv, rg, dmask, zeta, xi, gchunk, gs_ret)


def _diff_attn_kernel(q_ref, k_ref, vt_ref, lam_ref, subln_ref, gs_ref, *refs,
                      n_cast, heads, tq, tk, rb, lam_init):
    w_refs = refs[:n_cast]
    o_ref = refs[n_cast]
    wb_refs = refs[n_cast + 1:2 * n_cast + 1]
    scratch = refs[2 * n_cast + 1:]
    for hh in range(heads):
        lanes = pl.ds(hh * LANES, LANES)
        _diff_attn_head(
            q_ref.at[:, lanes], k_ref.at[:, lanes],
            vt_ref.at[pl.ds(hh * VT_ROWS, VT_ROWS), :], lam_ref, subln_ref,
            gs_ref.at[:, lanes], o_ref.at[:, lanes],
            w_refs if hh == 0 else (), wb_refs if hh == 0 else (),
            *[ref.at[hh] for ref in scratch], tq=tq, tk=tk, rb=rb, lam_init=lam_init)


def _diff_attn_head(q_ref, k_ref, vt_ref, lam_ref, subln_ref, gs_ref, o_ref, w_refs, wb_refs,
                    qs_ref, s_ref, mc_ref, m_ref, acc_ref, *, tq, tk, rb, lam_init):
    i = pl.program_id(2)
    q = q_ref[...]
    low = _low_half_mask(q.shape)
    zero = jnp.zeros_like(q)
    qs_ref[:tq, :] = jnp.where(low, q, zero)
    qs_ref[tq:, :] = jnp.where(low, zero, q)
    nt = (((1,), (1,)), ((), ()))

    n_blocks = 2 * tq // rb
    n_diag = tq // tk
    n_full = i * n_diag

    def diag_kind(d, b):
        q_first = (b * rb) % tq
        if d * tk > q_first + rb - 1:
            return "skip"
        return "full" if (d + 1) * tk - 1 <= q_first else "mask"

    def scores(j, b, key_offset=None):
        start = pl.multiple_of(j * tk, tk)
        cols = slice(b * rb, (b + 1) * rb)
        s = lax.dot_general(k_ref[pl.ds(start, tk), :], qs_ref[cols, :], nt,
                            preferred_element_type=F32)
        if key_offset is not None:
            kpos = lax.broadcasted_iota(jnp.int32, s.shape, 0) + key_offset
            qpos = lax.broadcasted_iota(jnp.int32, s.shape, 1) + (b * rb) % tq
            s = jnp.where(kpos <= qpos, s, -jnp.inf)
        s_ref[:, cols] = s
        mc_ref[:, cols] = jnp.broadcast_to(jnp.max(s, axis=0, keepdims=True),
                                           (F32_SUBLANES, rb))

    def consume(j, b):
        start = pl.multiple_of(j * tk, tk)
        cols = slice(b * rb, (b + 1) * rb)
        m_prev = m_ref[:, cols]
        m_new = jnp.maximum(m_prev, mc_ref[:, cols])
        alpha = jnp.exp2(m_prev - m_new)
        p = jnp.exp2(s_ref[:, cols] - m_new[0:1, :])
        pv = jnp.dot(vt_ref[:, pl.ds(start, tk)], p.astype(BF16),
                     preferred_element_type=F32)
        acc_ref[:, cols] = alpha[0:1, :] * acc_ref[:, cols] + pv
        m_ref[:, cols] = m_new

    ahead = n_blocks // 2
    order = sorted(range(n_blocks), key=lambda b: -((b * rb) % tq))
    assert all(diag_kind(0, b) == "full" for b in order[:ahead])
    for b in order[:ahead]:
        scores(0, b)
    m_ref[...] = jnp.full_like(m_ref, -jnp.inf)
    acc_ref[...] = jnp.zeros_like(acc_ref)

    def full_chunks(j0, count):
        for u in range(count):
            j = j0 + u
            for pos, b in enumerate(order):
                if pos + ahead < n_blocks:
                    scores(j, order[pos + ahead])
                else:
                    scores(j + 1, order[pos + ahead - n_blocks])
                consume(j, b)

    def body(t, carry):
        full_chunks(t * (2 * n_diag), 2 * n_diag)
        return carry

    lax.fori_loop(0, i // 2, body, 0)

    @pl.when(i % 2 == 1)
    def _():
        full_chunks((i - 1) * n_diag, n_diag)

    lv = lam_ref[...]
    lam = (jnp.exp(jnp.sum(lv[0:1] * lv[1:2], axis=-1, keepdims=True))
           - jnp.exp(jnp.sum(lv[2:3] * lv[3:4], axis=-1, keepdims=True)) + lam_init)

    out_gain = subln_ref[...] * (1.0 - lam_init) * gs_ref[...]

    def finalize(d):
        a0 = acc_ref[:, d * tk:(d + 1) * tk]
        a1 = acc_ref[:, tq + d * tk:tq + (d + 1) * tk]
        o0 = a0[:DIFF_DV, :] * (1.0 / a0[DIFF_DV:DIFF_DV + 1, :])
        o1 = a1[:DIFF_DV, :] * (1.0 / a1[DIFF_DV:DIFF_DV + 1, :])
        o = (o0 - lam * o1).T
        o_ref[d * tk:(d + 1) * tk, :] = (o * _rms(o) * out_gain).astype(o_ref.dtype)

    for d in range(n_diag):
        for pos, b in enumerate(order):
            nd, npos = ((d, pos + ahead) if pos + ahead < n_blocks
                        else (d + 1, pos + ahead - n_blocks))
            if nd < n_diag and diag_kind(nd, order[npos]) != "skip":
                masked = diag_kind(nd, order[npos]) == "mask"
                scores(n_full + nd, order[npos], nd * tk if masked else None)
            if diag_kind(d, b) != "skip":
                consume(n_full + d, b)
        finalize(d)
        if d == 0:
            for w_ref, wb_ref in zip(w_refs, wb_refs):
                wb_ref[...] = w_ref[...].astype(wb_ref.dtype)


def _diff_attn(dq, dk, dvt, lam_p, subln, gs_diff, cast_weights, batch, seq, tq, tk, rb, heads,
               lam_init):
    T = dq.shape[0]
    nq = seq // tq
    n_groups = DIFF_HEADS // heads
    n_steps = batch * n_groups * nq
    assert seq % tq == 0 and tq % tk == 0 and tq % rb == 0 and DIFF_HEADS % heads == 0

    def cast_spec(w):
        rows = w.shape[0]
        n_blk = max(n for n in range(1, n_steps + 1)
                    if n_steps % n == 0 and rows % (n * BF16_SUBLANES) == 0)
        hold = n_steps // n_blk
        return pl.BlockSpec(
            (rows // n_blk, w.shape[1]),
            lambda b, g, i: (((b * n_groups + g) * nq + i) // hold, 0))

    w_specs = [cast_spec(w) for w in cast_weights]
    kern = functools.partial(_diff_attn_kernel, n_cast=len(cast_weights), heads=heads, tq=tq,
                             tk=tk, rb=rb, lam_init=lam_init)
    per_head = lambda shape, dtype: pltpu.VMEM((heads,) + shape, dtype)
    outs = pl.pallas_call(
        kern,
        grid=(batch, n_groups, nq),
        in_specs=[
            pl.BlockSpec((tq, heads * LANES), lambda b, g, i: (b * nq + i, g)),
            pl.BlockSpec((seq, heads * LANES), lambda b, g, i: (b, g)),
            pl.BlockSpec((heads * VT_ROWS, seq), lambda b, g, i: (g, b)),
            pl.BlockSpec((4, DIFF_DQK), lambda b, g, i: (0, 0)),
            pl.BlockSpec((1, LANES), lambda b, g, i: (0, 0)),
            pl.BlockSpec((1, heads * LANES), lambda b, g, i: (0, g)),
        ] + w_specs,
        out_specs=[pl.BlockSpec((tq, heads * LANES), lambda b, g, i: (b * nq + i, g))]
        + w_specs,
        out_shape=[jax.ShapeDtypeStruct((T, DIFF_WIDTH), BF16)]
        + [jax.ShapeDtypeStruct(w.shape, BF16) for w in cast_weights],
        scratch_shapes=[per_head((2 * tq, LANES), BF16), per_head((tk, 2 * tq), F32),
                        per_head((F32_SUBLANES, 2 * tq), F32),
                        per_head((F32_SUBLANES, 2 * tq), F32),
                        per_head((VT_ROWS, 2 * tq), F32)],
        compiler_params=pltpu.CompilerParams(
            dimension_semantics=("parallel", "parallel", "arbitrary"),
            vmem_limit_bytes=VMEM_LIMIT),
        name="diff_attn",
    )(dq, dk, dvt, lam_p, subln, gs_diff, *cast_weights)
    return outs[0], outs[1:]


def _mem_kv_kernel(mem_ref, gain_ref, w_ref, kg_ref, k_ref, v_ref):
    m = mem_ref[...]
    h = (m * _rms(m) * gain_ref[...]).astype(BF16)
    D = m.shape[1]
    dh = D // XATTN_HEADS
    k = jnp.dot(h, w_ref[:, :D], preferred_element_type=F32)
    v_ref[...] = jnp.dot(h, w_ref[:, D:], preferred_element_type=F32).astype(v_ref.dtype)
    for hh in range(XATTN_HEADS):
        kh = k[:, hh * dh:(hh + 1) * dh]
        k_ref[:, hh * dh:(hh + 1) * dh] = (kh * _rms(kh) * kg_ref[...]).astype(k_ref.dtype)


def _mem_kv(mem2, gain, xkv, kg, batch, mem_len):
    BM, D = mem2.shape
    const = lambda b: (0, 0)
    out = jax.ShapeDtypeStruct((BM, D), BF16)
    return pl.pallas_call(
        _mem_kv_kernel,
        grid=(batch,),
        in_specs=[
            pl.BlockSpec((mem_len, D), lambda b: (b, 0)),
            pl.BlockSpec((1, D), const),
            pl.BlockSpec((D, 2 * D), const),
            pl.BlockSpec((1, D // XATTN_HEADS), const),
        ],
        out_specs=[pl.BlockSpec((mem_len, D), lambda b: (b, 0))] * 2,
        out_shape=[out, out],
        compiler_params=pltpu.CompilerParams(
            dimension_semantics=("parallel",), vmem_limit_bytes=VMEM_LIMIT),
        name="mem_kv",
    )(mem2, gain, xkv, kg)


def _mix_xattn_kernel(x_ref, oret_ref, odiff_ref, wout_ref, nx_ref, xq_ref, qg_ref,
                      kmem_ref, vmem_ref, xo_ref, out_ref):
    D = x_ref.shape[1]
    dh = D // XATTN_HEADS
    x1 = (x_ref[...]
          + jnp.dot(oret_ref[...], wout_ref[:RET_WIDTH, :], preferred_element_type=F32)
          + jnp.dot(odiff_ref[...], wout_ref[RET_WIDTH:, :], preferred_element_type=F32))
    h = (x1 * _rms(x1) * nx_ref[...]).astype(BF16)
    q = jnp.dot(h, xq_ref[...], preferred_element_type=F32)
    nt = (((1,), (1,)), ((), ()))
    outs = []
    for hh in range(XATTN_HEADS):
        sl = slice(hh * dh, (hh + 1) * dh)
        qh = q[:, sl]
        qn = (qh * _rms(qh) * qg_ref[...] * (dh ** -0.5)).astype(BF16)
        s = lax.dot_general(qn, kmem_ref[:, sl], nt, preferred_element_type=F32)
        e = jnp.exp(s - jnp.max(s, axis=-1, keepdims=True))
        p = e * (1.0 / jnp.sum(e, axis=-1, keepdims=True))
        outs.append(jnp.dot(p.astype(BF16), vmem_ref[:, sl],
                            preferred_element_type=F32).astype(BF16))
    o = jnp.concatenate(outs, axis=1)
    out_ref[...] = x1 + jnp.dot(o, xo_ref[...], preferred_element_type=F32)


def _mix_xattn(x2, o_ret, o_diff, w_out, nx, xq, qg, kmem, vmem, xo, seq, mem_len, tm):
    T, D = x2.shape
    nblk = seq // tm
    tok = lambda i: (i, 0)
    const = lambda i: (0, 0)
    memb = lambda i: (i // nblk, 0)
    return pl.pallas_call(
        _mix_xattn_kernel,
        grid=(T // tm,),
        in_specs=[
            pl.BlockSpec((tm, D), tok),
            pl.BlockSpec((tm, RET_WIDTH), tok),
            pl.BlockSpec((tm, DIFF_WIDTH), tok),
            pl.BlockSpec((D, D), const),
            pl.BlockSpec((1, D), const),
            pl.BlockSpec((D, D), const),
            pl.BlockSpec((1, D // XATTN_HEADS), const),
            pl.BlockSpec((mem_len, D), memb),
            pl.BlockSpec((mem_len, D), memb),
            pl.BlockSpec((D, D), const),
        ],
        out_specs=pl.BlockSpec((tm, D), tok),
        out_shape=jax.ShapeDtypeStruct((T, D), F32),
        compiler_params=pltpu.CompilerParams(
            dimension_semantics=("parallel",), vmem_limit_bytes=VMEM_LIMIT),
        name="mix_xattn",
    )(x2, o_ret, o_diff, w_out, nx, xq, qg, kmem, vmem, xo)


def _ffn_kernel(x_ref, gain_ref, wg_ref, wu_ref, wd_ref, out_ref, *, hidden_chunk):
    x = x_ref[...]
    h = (x * _rms(x) * gain_ref[...]).astype(BF16)
    hidden = wg_ref.shape[1]
    acc = x
    for c in range(hidden // hidden_chunk):
        sl = slice(c * hidden_chunk, (c + 1) * hidden_chunk)
        g = jnp.dot(h, wg_ref[:, sl], preferred_element_type=F32)
        u = jnp.dot(h, wu_ref[:, sl], preferred_element_type=F32)
        a = (g * jax.nn.sigmoid(g) * u).astype(BF16)
        acc = acc + jnp.dot(a, wd_ref[sl, :], preferred_element_type=F32)
    out_ref[...] = acc


def _ffn(x2, gain, wg, wu, wd, tm, hidden_chunk):
    T, D = x2.shape
    hidden = wg.shape[1]
    tok = lambda i: (i, 0)
    const = lambda i: (0, 0)
    kern = functools.partial(_ffn_kernel, hidden_chunk=hidden_chunk)
    return pl.pallas_call(
        kern,
        grid=(T // tm,),
        in_specs=[
            pl.BlockSpec((tm, D), tok),
            pl.BlockSpec((1, D), const),
            pl.BlockSpec((D, hidden), const, pipeline_mode=pl.Buffered(1)),
            pl.BlockSpec((D, hidden), const, pipeline_mode=pl.Buffered(1)),
            pl.BlockSpec((hidden, D), const, pipeline_mode=pl.Buffered(1)),
        ],
        out_specs=pl.BlockSpec((tm, D), tok),
        out_shape=jax.ShapeDtypeStruct((T, D), F32),
        compiler_params=pltpu.CompilerParams(
            dimension_semantics=("parallel",), vmem_limit_bytes=VMEM_LIMIT),
        name="ffn",
    )(x2, gain, wg, wu, wd)


def _rotary_tables(seq):
    half = RET_DK // 2
    inv = 1.0 / (RET_THETA_BASE ** np.linspace(0.0, 1.0, half))
    ang = np.arange(seq, dtype=np.float64)[:, None] * inv[None, :]
    cos, sin = np.cos(ang), np.sin(ang)
    cos_t = np.tile(cos, (1, LANES // half))
    sin_t = np.tile(np.concatenate([-sin, sin], axis=1), (1, LANES // RET_DK))
    return cos_t.astype(np.float32), sin_t.astype(np.float32)


def _decay_tables(chunk):
    H = RET_HEADS
    log_g = np.log1p(-(2.0 ** (-5.0 - np.arange(H, dtype=np.float64))))
    idx = np.arange(chunk, dtype=np.float64)
    rel = idx[:, None] - idx[None, :]
    dmask = np.where(rel >= 0, np.exp(log_g[:, None, None] * np.maximum(rel, 0.0)), 0.0)
    zeta = np.exp(log_g[:, None] * (chunk - 1.0 - idx))
    xi = np.exp(log_g[:, None] * (idx + 1.0))
    g_chunk = np.exp(log_g * chunk)
    widen = lambda t: np.repeat(t.T, RET_DK, axis=1)
    f32 = lambda t: t.astype(np.float32)
    dmask = dmask.reshape(H // 2, 2 * chunk, chunk)
    return (f32(dmask), f32(widen(zeta)), f32(widen(xi)),
            f32(np.repeat(g_chunk, RET_DK)[None, :]))


def kernel(x, mem, norm_mix, w_in, diff_q_gain, diff_k_gain, diff_lambda, diff_subln,
           group_scale, w_out, norm_x, norm_mem, xq, xkv, xq_gain, xk_gain, xo,
           norm_ffn, w_gate, w_up, w_down):
    B, S, D = x.shape
    M = mem.shape[1]
    depth = norm_mix.shape[0]
    x2 = x.reshape(B * S, D)
    mem2 = mem.reshape(B * M, D)

    ret_chunk = 256
    cos_t, sin_t = _rotary_tables(S)
    dmask, zeta, xi, gchunk = _decay_tables(ret_chunk)
    row = lambda a: a.reshape(1, -1)
    twice = lambda a: jnp.tile(a.reshape(1, -1), (1, 2))

    for l in range(depth):
        lam_init = 0.8 - 0.6 * math.exp(-0.3 * l)
        rq, rk, rv, rg, dq, dk, dv = _in_proj(
            x2, row(norm_mix[l]), w_in[l], cos_t, sin_t,
            twice(diff_q_gain[l]), twice(diff_k_gain[l]), S, tm=1024)
        o_ret = _retention(rq, rk, rv, rg, dmask, zeta, xi, gchunk,
                           row(group_scale[l, :RET_WIDTH]), B, S, ts=1024, chunk=ret_chunk)
        later_weights = [w_out[l], xq[l], xkv[l], xo[l], w_gate[l], w_up[l], w_down[l]]
        o_diff, (w_out_b, xq_b, xkv_b, xo_b, w_gate_b, w_up_b, w_down_b) = _diff_attn(
            dq, dk, dv, diff_lambda[l], row(diff_subln[l]), row(group_scale[l, RET_WIDTH:]),
            later_weights, B, S, tq=1024, tk=512, rb=512, heads=2, lam_init=lam_init)
        kmem, vmem = _mem_kv(mem2, row(norm_mem[l]), xkv_b, row(xk_gain[l]), B, M)
        x2 = _mix_xattn(x2, o_ret, o_diff, w_out_b, row(norm_x[l]), xq_b, row(xq_gain[l]),
                        kmem, vmem, xo_b, S, M, tm=1024)
        x2 = _ffn(x2, row(norm_ffn[l]), w_gate_b, w_up_b, w_down_b, tm=1024, hidden_chunk=256)
    return x2.reshape(B, S, D)
```

```python
import functools
import math

import jax
import jax.numpy as jnp
import numpy as np
from jax import lax
from jax.experimental import pallas as pl
from jax.experimental.pallas import tpu as pltpu

F32 = jnp.float32
BF16 = jnp.bfloat16

EPS = 1e-6
LANES = 128
VMEM_LIMIT = 56 * 1024 * 1024

RET_HEADS = 8
RET_DK = 64
RET_DV = 64
RET_WIDTH = 512
RET_THETA_BASE = 10000.0
DIFF_HEADS = 4
DIFF_DQK = 64
DIFF_DV = 128
DIFF_WIDTH = 512
XATTN_HEADS = 4
SEG = 512
N_SEG = 7
LOG2E = math.log2(math.e)
F32_SUBLANES = 8
BF16_SUBLANES = 16
VT_ROWS = DIFF_DV + BF16_SUBLANES


def _rms(x, axis=-1):
    return lax.rsqrt(jnp.mean(x * x, axis=axis, keepdims=True) + EPS)


def _low_half_mask(shape):
    lane = lax.broadcasted_iota(jnp.int32, shape, len(shape) - 1)
    return (lane % LANES) < (LANES // 2)


def _in_proj_kernel(x_ref, gain_ref, w_ref, cos_ref, sin_ref, qg_ref, kg_ref,
                    rq_ref, rk_ref, rv_ref, rg_ref, dq_ref, dk_ref, dvt_ref, wb_ref):
    @pl.when(pl.program_id(0) == 0)
    def _():
        for seg in range(N_SEG):
            cols = slice(seg * SEG, (seg + 1) * SEG)
            wb_ref[:, cols] = w_ref[:, cols].astype(BF16)

    x = x_ref[...]
    h = (x * _rms(x) * gain_ref[...]).astype(BF16)

    def proj(seg):
        return jnp.dot(h, wb_ref[:, seg * SEG:(seg + 1) * SEG], preferred_element_type=F32)

    cos = cos_ref[...]
    sin = sin_ref[...]
    low = _low_half_mask(cos.shape)
    half = RET_DK // 2
    first_half = (lax.broadcasted_iota(jnp.int32, cos.shape, 1) % RET_DK) < half

    def rotary_store(y, out_ref, scale):
        for j in range(SEG // LANES):
            ys = y[:, j * LANES:(j + 1) * LANES]
            partner = jnp.where(first_half, pltpu.roll(ys, LANES - half, 1),
                                pltpu.roll(ys, half, 1))
            r = ys * cos + partner * sin
            if scale != 1.0:
                r = r * scale
            out_ref[:, j * LANES:(j + 1) * LANES] = r.astype(out_ref.dtype)

    def qknorm_store(y, g_ref, out_ref, scale):
        g = g_ref[...]
        for j in range(SEG // LANES):
            ys = y[:, j * LANES:(j + 1) * LANES]
            sq = ys * ys
            s_lo = jnp.sum(jnp.where(low, sq, 0.0), axis=-1, keepdims=True)
            s_hi = jnp.sum(jnp.where(low, 0.0, sq), axis=-1, keepdims=True)
            inv = jnp.where(low, lax.rsqrt(s_lo * (1.0 / DIFF_DQK) + EPS),
                            lax.rsqrt(s_hi * (1.0 / DIFF_DQK) + EPS))
            r = ys * inv * g
            if scale != 1.0:
                r = r * scale
            out_ref[:, j * LANES:(j + 1) * LANES] = r.astype(out_ref.dtype)

    rotary_store(proj(0), rq_ref, 1.0)
    rotary_store(proj(1), rk_ref, RET_DK ** -0.5)
    rv_ref[...] = proj(2).astype(BF16)
    g = proj(3)
    rg_ref[...] = (g * jax.nn.sigmoid(g)).astype(BF16)
    qknorm_store(proj(4), qg_ref, dq_ref, DIFF_DQK ** -0.5 * LOG2E)
    qknorm_store(proj(5), kg_ref, dk_ref, 1.0)
    dv = proj(6)
    ones = jnp.ones((VT_ROWS - DIFF_DV, dv.shape[0]), BF16)
    for hh in range(DIFF_HEADS):
        dvt_ref[hh * VT_ROWS:hh * VT_ROWS + DIFF_DV, :] = (
            dv[:, hh * DIFF_DV:(hh + 1) * DIFF_DV].T.astype(BF16))
        dvt_ref[hh * VT_ROWS + DIFF_DV:(hh + 1) * VT_ROWS, :] = ones


def _in_proj(x2, gain, w_in, cos_t, sin_t, qg, kg, seq, tm):
    T, D = x2.shape
    n_pos_blocks = seq // tm
    tok = lambda i: (i, 0)
    const = lambda i: (0, 0)
    pos = lambda i: (i % n_pos_blocks, 0)
    out = jax.ShapeDtypeStruct((T, SEG), BF16)
    return pl.pallas_call(
        _in_proj_kernel,
        grid=(T // tm,),
        in_specs=[
            pl.BlockSpec((tm, D), tok),
            pl.BlockSpec((1, D), const),
            pl.BlockSpec((D, N_SEG * SEG), const, pipeline_mode=pl.Buffered(1)),
            pl.BlockSpec((tm, LANES), pos),
            pl.BlockSpec((tm, LANES), pos),
            pl.BlockSpec((1, LANES), const),
            pl.BlockSpec((1, LANES), const),
        ],
        out_specs=[pl.BlockSpec((tm, SEG), tok)] * (N_SEG - 1)
        + [pl.BlockSpec((DIFF_HEADS * VT_ROWS, tm), lambda i: (0, i))],
        out_shape=[out] * (N_SEG - 1)
        + [jax.ShapeDtypeStruct((DIFF_HEADS * VT_ROWS, T), BF16)],
        scratch_shapes=[pltpu.VMEM((D, N_SEG * SEG), BF16)],
        compiler_params=pltpu.CompilerParams(
            dimension_semantics=("arbitrary",), vmem_limit_bytes=VMEM_LIMIT),
        name="in_proj",
    )(x2, gain, w_in, cos_t, sin_t, qg, kg)


def _retention_kernel(q_ref, k_ref, v_ref, g_ref, dmask_ref, zeta_ref, xi_ref, gchunk_ref,
                      gs_ref, o_ref, state_ref, *, chunk, n_chunks):
    @pl.when(pl.program_id(1) == 0)
    def _():
        state_ref[...] = jnp.zeros_like(state_ref)

    n_slabs = RET_WIDTH // LANES
    low = _low_half_mask((chunk, LANES))
    row = lax.broadcasted_iota(jnp.int32, (LANES, LANES), 0)
    col = lax.broadcasted_iota(jnp.int32, (LANES, LANES), 1)
    blockdiag = (row < RET_DK) == (col < RET_DK)
    nt = (((1,), (1,)), ((), ()))
    tn = (((0,), (0,)), ((), ()))

    for c in range(n_chunks):
        rows = slice(c * chunk, (c + 1) * chunk)
        for s in range(n_slabs):
            sl = slice(s * LANES, (s + 1) * LANES)
            zeta = zeta_ref[:, sl]
            xi = xi_ref[:, sl]
            gch = gchunk_ref[:, sl]
            q = q_ref[rows, sl]
            k = k_ref[rows, sl]
            v = v_ref[rows, sl]
            zero = jnp.zeros_like(q)
            q2 = jnp.concatenate([jnp.where(low, q, zero), jnp.where(low, zero, q)], axis=0)
            s2 = lax.dot_general(q2, k, nt, preferred_element_type=F32) * dmask_ref[s]
            o2 = jnp.dot(s2.astype(BF16), v, preferred_element_type=F32)
            r_prev = state_ref[s]
            o_inter = jnp.dot(q, r_prev.astype(BF16), preferred_element_type=F32) * xi
            o = jnp.where(low, o2[:chunk], o2[chunk:]) + o_inter
            kz = (k.astype(F32) * zeta).astype(BF16)
            kv = lax.dot_general(kz, v, tn, preferred_element_type=F32)
            state_ref[s] = r_prev * gch + jnp.where(blockdiag, kv, 0.0)
            sq = o * o
            m_lo = jnp.sum(jnp.where(low, sq, 0.0), axis=-1, keepdims=True)
            m_hi = jnp.sum(jnp.where(low, 0.0, sq), axis=-1, keepdims=True)
            inv = jnp.where(low, lax.rsqrt(m_lo * (1.0 / RET_DV) + EPS),
                            lax.rsqrt(m_hi * (1.0 / RET_DV) + EPS))
            out = o * inv * g_ref[rows, sl].astype(F32) * gs_ref[:, sl]
            o_ref[rows, sl] = out.astype(o_ref.dtype)


def _retention(rq, rk, rv, rg, dmask, zeta, xi, gchunk, gs_ret, batch, seq, ts, chunk):
    T = rq.shape[0]
    nblk = seq // ts
    tok = lambda b, i: (b * nblk + i, 0)
    c2 = lambda b, i: (0, 0)
    c3 = lambda b, i: (0, 0, 0)
    kern = functools.partial(_retention_kernel, chunk=chunk, n_chunks=ts // chunk)
    return pl.pallas_call(
        kern,
        grid=(batch, nblk),
        in_specs=[pl.BlockSpec((ts, RET_WIDTH), tok)] * 4 + [
            pl.BlockSpec((RET_HEADS // 2, 2 * chunk, chunk), c3),
            pl.BlockSpec((chunk, RET_WIDTH), c2),
            pl.BlockSpec((chunk, RET_WIDTH), c2),
            pl.BlockSpec((1, RET_WIDTH), c2),
            pl.BlockSpec((1, RET_WIDTH), c2),
        ],
        out_specs=pl.BlockSpec((ts, RET_WIDTH), tok),
        out_shape=jax.ShapeDtypeStruct((T, RET_WIDTH), BF16),
        scratch_shapes=[pltpu.VMEM((RET_WIDTH // LANES, LANES, LANES), F32)],
        compiler_params=pltpu.CompilerParams(
            dimension_semantics=("parallel", "arbitrary"), vmem_limit_bytes=VMEM_LIMIT),
        name="retention",
    )(rq, rk, rv, rg, dmask, zeta, xi, gchunk, gs_ret)


def _diff_attn_kernel(q_ref, k_ref, vt_ref, lam_ref, subln_ref, gs_ref, *refs,
                      n_cast, heads, tq, tk, rb, lam_init):
    w_refs = refs[:n_cast]
    o_ref = refs[n_cast]
    wb_refs = refs[n_cast + 1:2 * n_cast + 1]
    scratch = refs[2 * n_cast + 1:]
    for hh in range(heads):
        lanes = pl.ds(hh * LANES, LANES)
        _diff_attn_head(
            q_ref.at[:, lanes], k_ref.at[:, lanes],
            vt_ref.at[pl.ds(hh * VT_ROWS, VT_ROWS), :], lam_ref, subln_ref,
            gs_ref.at[:, lanes], o_ref.at[:, lanes],
            w_refs if hh == 0 else (), wb_refs if hh == 0 else (),
            *[ref.at[hh] for ref in scratch], tq=tq, tk=tk, rb=rb, lam_init=lam_init)


def _diff_attn_head(q_ref, k_ref, vt_ref, lam_ref, subln_ref, gs_ref, o_ref, w_refs, wb_refs,
                    qs_ref, s_ref, mc_ref, m_ref, acc_ref, *, tq, tk, rb, lam_init):
    i = pl.program_id(2)
    q = q_ref[...]
    low = _low_half_mask(q.shape)
    zero = jnp.zeros_like(q)
    qs_ref[:tq, :] = jnp.where(low, q, zero)
    qs_ref[tq:, :] = jnp.where(low, zero, q)
    nt = (((1,), (1,)), ((), ()))

    n_blocks = 2 * tq // rb
    n_diag = tq // tk
    n_full = i * n_diag

    def diag_kind(d, b):
        q_first = (b * rb) % tq
        if d * tk > q_first + rb - 1:
            return "skip"
        return "full" if (d + 1) * tk - 1 <= q_first else "mask"

    def scores(j, b, key_offset=None):
        start = pl.multiple_of(j * tk, tk)
        cols = slice(b * rb, (b + 1) * rb)
        s = lax.dot_general(k_ref[pl.ds(start, tk), :], qs_ref[cols, :], nt,
                            preferred_element_type=F32)
        if key_offset is not None:
            kpos = lax.broadcasted_iota(jnp.int32, s.shape, 0) + key_offset
            qpos = lax.broadcasted_iota(jnp.int32, s.shape, 1) + (b * rb) % tq
            s = jnp.where(kpos <= qpos, s, -jnp.inf)
        s_ref[:, cols] = s
        mc_ref[:, cols] = jnp.broadcast_to(jnp.max(s, axis=0, keepdims=True),
                                           (F32_SUBLANES, rb))

    def consume(j, b):
        start = pl.multiple_of(j * tk, tk)
        cols = slice(b * rb, (b + 1) * rb)
        m_prev = m_ref[:, cols]
        m_new = jnp.maximum(m_prev, mc_ref[:, cols])
        alpha = jnp.exp2(m_prev - m_new)
        p = jnp.exp2(s_ref[:, cols] - m_new[0:1, :])
        pv = jnp.dot(vt_ref[:, pl.ds(start, tk)], p.astype(BF16),
                     preferred_element_type=F32)
        acc_ref[:, cols] = alpha[0:1, :] * acc_ref[:, cols] + pv
        m_ref[:, cols] = m_new

    ahead = n_blocks // 2
    order = sorted(range(n_blocks), key=lambda b: -((b * rb) % tq))
    assert all(diag_kind(0, b) == "full" for b in order[:ahead])
    for b in order[:ahead]:
        scores(0, b)
    m_ref[...] = jnp.full_like(m_ref, -jnp.inf)
    acc_ref[...] = jnp.zeros_like(acc_ref)

    def full_chunks(j0, count):
        for u in range(count):
            j = j0 + u
            for pos, b in enumerate(order):
                if pos + ahead < n_blocks:
                    scores(j, order[pos + ahead])
                else:
                    scores(j + 1, order[pos + ahead - n_blocks])
                consume(j, b)

    def body(t, carry):
        full_chunks(t * (2 * n_diag), 2 * n_diag)
        return carry

    lax.fori_loop(0, i // 2, body, 0)

    @pl.when(i % 2 == 1)
    def _():
        full_chunks((i - 1) * n_diag, n_diag)

    lv = lam_ref[...]
    lam = (jnp.exp(jnp.sum(lv[0:1] * lv[1:2], axis=-1, keepdims=True))
           - jnp.exp(jnp.sum(lv[2:3] * lv[3:4], axis=-1, keepdims=True)) + lam_init)

    out_gain = subln_ref[...] * (1.0 - lam_init) * gs_ref[...]

    def finalize(d):
        a0 = acc_ref[:, d * tk:(d + 1) * tk]
        a1 = acc_ref[:, tq + d * tk:tq + (d + 1) * tk]
        o0 = a0[:DIFF_DV, :] * (1.0 / a0[DIFF_DV:DIFF_DV + 1, :])
        o1 = a1[:DIFF_DV, :] * (1.0 / a1[DIFF_DV:DIFF_DV + 1, :])
        o = (o0 - lam * o1).T
        o_ref[d * tk:(d + 1) * tk, :] = (o * _rms(o) * out_gain).astype(o_ref.dtype)

    for d in range(n_diag):
        for pos, b in enumerate(order):
            nd, npos = ((d, pos + ahead) if pos + ahead < n_blocks
                        else (d + 1, pos + ahead - n_blocks))
            if nd < n_diag and diag_kind(nd, order[npos]) != "skip":
                masked = diag_kind(nd, order[npos]) == "mask"
                scores(n_full + nd, order[npos], nd * tk if masked else None)
            if diag_kind(d, b) != "skip":
                consume(n_full + d, b)
        finalize(d)
        if d == 0:
            for w_ref, wb_ref in zip(w_refs, wb_refs):
                wb_ref[...] = w_ref[...].astype(wb_ref.dtype)


def _diff_attn(dq, dk, dvt, lam_p, subln, gs_diff, cast_weights, batch, seq, tq, tk, rb, heads,
               lam_init):
    T = dq.shape[0]
    nq = seq // tq
    n_groups = DIFF_HEADS // heads
    n_steps = batch * n_groups * nq
    assert seq % tq == 0 and tq % tk == 0 and tq % rb == 0 and DIFF_HEADS % heads == 0

    def cast_spec(w):
        rows = w.shape[0]
        n_blk = max(n for n in range(1, n_steps + 1)
                    if n_steps % n == 0 and rows % (n * BF16_SUBLANES) == 0)
        hold = n_steps // n_blk
        return pl.BlockSpec(
            (rows // n_blk, w.shape[1]),
            lambda b, g, i: (((b * n_groups + g) * nq + i) // hold, 0))

    w_specs = [cast_spec(w) for w in cast_weights]
    kern = functools.partial(_diff_attn_kernel, n_cast=len(cast_weights), heads=heads, tq=tq,
                             tk=tk, rb=rb, lam_init=lam_init)
    per_head = lambda shape, dtype: pltpu.VMEM((heads,) + shape, dtype)
    outs = pl.pallas_call(
        kern,
        grid=(batch, n_groups, nq),
        in_specs=[
            pl.BlockSpec((tq, heads * LANES), lambda b, g, i: (b * nq + i, g)),
            pl.BlockSpec((seq, heads * LANES), lambda b, g, i: (b, g)),
            pl.BlockSpec((heads * VT_ROWS, seq), lambda b, g, i: (g, b)),
            pl.BlockSpec((4, DIFF_DQK), lambda b, g, i: (0, 0)),
            pl.BlockSpec((1, LANES), lambda b, g, i: (0, 0)),
            pl.BlockSpec((1, heads * LANES), lambda b, g, i: (0, g)),
        ] + w_specs,
        out_specs=[pl.BlockSpec((tq, heads * LANES), lambda b, g, i: (b * nq + i, g))]
        + w_specs,
        out_shape=[jax.ShapeDtypeStruct((T, DIFF_WIDTH), BF16)]
        + [jax.ShapeDtypeStruct(w.shape, BF16) for w in cast_weights],
        scratch_shapes=[per_head((2 * tq, LANES), BF16), per_head((tk, 2 * tq), F32),
                        per_head((F32_SUBLANES, 2 * tq), F32),
                        per_head((F32_SUBLANES, 2 * tq), F32),
                        per_head((VT_ROWS, 2 * tq), F32)],
        compiler_params=pltpu.CompilerParams(
            dimension_semantics=("parallel", "parallel", "arbitrary"),
            vmem_limit_bytes=VMEM_LIMIT),
        name="diff_attn",
    )(dq, dk, dvt, lam_p, subln, gs_diff, *cast_weights)
    return outs[0], outs[1:]


def _mem_kv_kernel(mem_ref, gain_ref, w_ref, kg_ref, k_ref, v_ref):
    m = mem_ref[...]
    h = (m * _rms(m) * gain_ref[...]).astype(BF16)
    D = m.shape[1]
    dh = D // XATTN_HEADS
    k = jnp.dot(h, w_ref[:, :D], preferred_element_type=F32)
    v_ref[...] = jnp.dot(h, w_ref[:, D:], preferred_element_type=F32).astype(v_ref.dtype)
    for hh in range(XATTN_HEADS):
        kh = k[:, hh * dh:(hh + 1) * dh]
        k_ref[:, hh * dh:(hh + 1) * dh] = (kh * _rms(kh) * kg_ref[...]).astype(k_ref.dtype)


def _mem_kv(mem2, gain, xkv, kg, batch, mem_len):
    BM, D = mem2.shape
    const = lambda b: (0, 0)
    out = jax.ShapeDtypeStruct((BM, D), BF16)
    return pl.pallas_call(
        _mem_kv_kernel,
        grid=(batch,),
        in_specs=[
            pl.BlockSpec((mem_len, D), lambda b: (b, 0)),
            pl.BlockSpec((1, D), const),
            pl.BlockSpec((D, 2 * D), const),
            pl.BlockSpec((1, D // XATTN_HEADS), const),
        ],
        out_specs=[pl.BlockSpec((mem_len, D), lambda b: (b, 0))] * 2,
        out_shape=[out, out],
        compiler_params=pltpu.CompilerParams(
            dimension_semantics=("parallel",), vmem_limit_bytes=VMEM_LIMIT),
        name="mem_kv",
    )(mem2, gain, xkv, kg)


def _mix_xattn_kernel(x_ref, oret_ref, odiff_ref, wout_ref, nx_ref, xq_ref, qg_ref,
                      kmem_ref, vmem_ref, xo_ref, out_ref):
    D = x_ref.shape[1]
    dh = D // XATTN_HEADS
    x1 = (x_ref[...]
          + jnp.dot(oret_ref[...], wout_ref[:RET_WIDTH, :], preferred_element_type=F32)
          + jnp.dot(odiff_ref[...], wout_ref[RET_WIDTH:, :], preferred_element_type=F32))
    h = (x1 * _rms(x1) * nx_ref[...]).astype(BF16)
    q = jnp.dot(h, xq_ref[...], preferred_element_type=F32)
    nt = (((1,), (1,)), ((), ()))
    outs = []
    for hh in range(XATTN_HEADS):
        sl = slice(hh * dh, (hh + 1) * dh)
        qh = q[:, sl]
        qn = (qh * _rms(qh) * qg_ref[...] * (dh ** -0.5 * LOG2E)).astype(BF16)
        s = lax.dot_general(qn, kmem_ref[:, sl], nt, preferred_element_type=F32)
        e = jnp.exp2(s - jnp.max(s, axis=-1, keepdims=True))
        inv_l = 1.0 / jnp.sum(e, axis=-1, keepdims=True)
        pv = jnp.dot(e.astype(BF16), vmem_ref[:, sl], preferred_element_type=F32)
        outs.append((pv * inv_l).astype(BF16))
    o = jnp.concatenate(outs, axis=1)
    out_ref[...] = x1 + jnp.dot(o, xo_ref[...], preferred_element_type=F32)


def _mix_xattn(x2, o_ret, o_diff, w_out, nx, xq, qg, kmem, vmem, xo, seq, mem_len, tm):
    T, D = x2.shape
    nblk = seq // tm
    tok = lambda i: (i, 0)
    const = lambda i: (0, 0)
    memb = lambda i: (i // nblk, 0)
    return pl.pallas_call(
        _mix_xattn_kernel,
        grid=(T // tm,),
        in_specs=[
            pl.BlockSpec((tm, D), tok),
            pl.BlockSpec((tm, RET_WIDTH), tok),
            pl.BlockSpec((tm, DIFF_WIDTH), tok),
            pl.BlockSpec((D, D), const),
            pl.BlockSpec((1, D), const),
            pl.BlockSpec((D, D), const),
            pl.BlockSpec((1, D // XATTN_HEADS), const),
            pl.BlockSpec((mem_len, D), memb),
            pl.BlockSpec((mem_len, D), memb),
            pl.BlockSpec((D, D), const),
        ],
        out_specs=pl.BlockSpec((tm, D), tok),
        out_shape=jax.ShapeDtypeStruct((T, D), F32),
        compiler_params=pltpu.CompilerParams(
            dimension_semantics=("parallel",), vmem_limit_bytes=VMEM_LIMIT),
        name="mix_xattn",
    )(x2, o_ret, o_diff, w_out, nx, xq, qg, kmem, vmem, xo)


def _ffn_kernel(x_ref, gain_ref, wg_ref, wu_ref, wd_ref, out_ref, *, hidden_chunk):
    x = x_ref[...]
    h = (x * _rms(x) * gain_ref[...]).astype(BF16)
    hidden = wg_ref.shape[1]
    acc = x
    for c in range(hidden // hidden_chunk):
        sl = slice(c * hidden_chunk, (c + 1) * hidden_chunk)
        g = jnp.dot(h, wg_ref[:, sl], preferred_element_type=F32)
        u = jnp.dot(h, wu_ref[:, sl], preferred_element_type=F32)
        a = (g * jax.nn.sigmoid(g) * u).astype(BF16)
        acc = acc + jnp.dot(a, wd_ref[sl, :], preferred_element_type=F32)
    out_ref[...] = acc


def _ffn(x2, gain, wg, wu, wd, tm, hidden_chunk):
    T, D = x2.shape
    hidden = wg.shape[1]
    tok = lambda i: (i, 0)
    const = lambda i: (0, 0)
    kern = functools.partial(_ffn_kernel, hidden_chunk=hidden_chunk)
    return pl.pallas_call(
        kern,
        grid=(T // tm,),
        in_specs=[
            pl.BlockSpec((tm, D), tok),
            pl.BlockSpec((1, D), const),
            pl.BlockSpec((D, hidden), const, pipeline_mode=pl.Buffered(1)),
            pl.BlockSpec((D, hidden), const, pipeline_mode=pl.Buffered(1)),
            pl.BlockSpec((hidden, D), const, pipeline_mode=pl.Buffered(1)),
        ],
        out_specs=pl.BlockSpec((tm, D), tok),
        out_shape=jax.ShapeDtypeStruct((T, D), F32),
        compiler_params=pltpu.CompilerParams(
            dimension_semantics=("parallel",), vmem_limit_bytes=VMEM_LIMIT),
        name="ffn",
    )(x2, gain, wg, wu, wd)


def _rotary_tables(seq):
    half = RET_DK // 2
    inv = 1.0 / (RET_THETA_BASE ** np.linspace(0.0, 1.0, half))
    ang = np.arange(seq, dtype=np.float64)[:, None] * inv[None, :]
    cos, sin = np.cos(ang), np.sin(ang)
    cos_t = np.tile(cos, (1, LANES // half))
    sin_t = np.tile(np.concatenate([-sin, sin], axis=1), (1, LANES // RET_DK))
    return cos_t.astype(np.float32), sin_t.astype(np.float32)


def _decay_tables(chunk):
    H = RET_HEADS
    log_g = np.log1p(-(2.0 ** (-5.0 - np.arange(H, dtype=np.float64))))
    idx = np.arange(chunk, dtype=np.float64)
    rel = idx[:, None] - idx[None, :]
    dmask = np.where(rel >= 0, np.exp(log_g[:, None, None] * np.maximum(rel, 0.0)), 0.0)
    zeta = np.exp(log_g[:, None] * (chunk - 1.0 - idx))
    xi = np.exp(log_g[:, None] * (idx + 1.0))
    g_chunk = np.exp(log_g * chunk)
    widen = lambda t: np.repeat(t.T, RET_DK, axis=1)
    f32 = lambda t: t.astype(np.float32)
    dmask = dmask.reshape(H // 2, 2 * chunk, chunk)
    return (f32(dmask), f32(widen(zeta)), f32(widen(xi)),
            f32(np.repeat(g_chunk, RET_DK)[None, :]))


def kernel(x, mem, norm_mix, w_in, diff_q_gain, diff_k_gain, diff_lambda, diff_subln,
           group_scale, w_out, norm_x, norm_mem, xq, xkv, xq_gain, xk_gain, xo,
           norm_ffn, w_gate, w_up, w_down):
    B, S, D = x.shape
    M = mem.shape[1]
    depth = norm_mix.shape[0]
    x2 = x.reshape(B * S, D)
    mem2 = mem.reshape(B * M, D)

    ret_chunk = 256
    cos_t, sin_t = _rotary_tables(S)
    dmask, zeta, xi, gchunk = _decay_tables(ret_chunk)
    row = lambda a: a.reshape(1, -1)
    twice = lambda a: jnp.tile(a.reshape(1, -1), (1, 2))

    for l in range(depth):
        lam_init = 0.8 - 0.6 * math.exp(-0.3 * l)
        rq, rk, rv, rg, dq, dk, dv = _in_proj(
            x2, row(norm_mix[l]), w_in[l], cos_t, sin_t,
            twice(diff_q_gain[l]), twice(diff_k_gain[l]), S, tm=1024)
        o_ret = _retention(rq, rk, rv, rg, dmask, zeta, xi, gchunk,
                           row(group_scale[l, :RET_WIDTH]), B, S, ts=1024, chunk=ret_chunk)
        later_weights = [w_out[l], xq[l], xkv[l], xo[l], w_gate[l], w_up[l], w_down[l]]
        o_diff, (w_out_b, xq_b, xkv_b, xo_b, w_gate_b, w_up_b, w_down_b) = _diff_attn(
            dq, dk, dv, diff_lambda[l], row(diff_subln[l]), row(group_scale[l, RET_WIDTH:]),
            later_weights, B, S, tq=1024, tk=512, rb=512, heads=2, lam_init=lam_init)
        kmem, vmem = _mem_kv(mem2, row(norm_mem[l]), xkv_b, row(xk_gain[l]), B, M)
        x2 = _mix_xattn(x2, o_ret, o_diff, w_out_b, row(norm_x[l]), xq_b, row(xq_gain[l]),
                        kmem, vmem, xo_b, S, M, tm=1024)
        x2 = _ffn(x2, row(norm_ffn[l]), w_gate_b, w_up_b, w_down_b, tm=1024, hidden_chunk=256)
    return x2.reshape(B, S, D)
```

```python
import functools
import math

import jax
import jax.numpy as jnp
import numpy as np
from jax import lax
from jax.experimental import pallas as pl
from jax.experimental.pallas import tpu as pltpu

F32 = jnp.float32
BF16 = jnp.bfloat16

EPS = 1e-6
LANES = 128
VMEM_LIMIT = 56 * 1024 * 1024

RET_HEADS = 8
RET_DK = 64
RET_DV = 64
RET_WIDTH = 512
RET_THETA_BASE = 10000.0
DIFF_HEADS = 4
DIFF_DQK = 64
DIFF_DV = 128
DIFF_WIDTH = 512
XATTN_HEADS = 4
SEG = 512
N_SEG = 7
LOG2E = math.log2(math.e)
F32_SUBLANES = 8
BF16_SUBLANES = 16
VT_ROWS = DIFF_DV + BF16_SUBLANES


def _rms(x, axis=-1):
    return lax.rsqrt(jnp.mean(x * x, axis=axis, keepdims=True) + EPS)


def _low_half_mask(shape):
    lane = lax.broadcasted_iota(jnp.int32, shape, len(shape) - 1)
    return (lane % LANES) < (LANES // 2)


def _in_proj_kernel(x_ref, gain_ref, w_ref, cos_ref, sin_ref, qg_ref, kg_ref,
                    rq_ref, rk_ref, rv_ref, rg_ref, dq_ref, dk_ref, dvt_ref, wb_ref):
    @pl.when(pl.program_id(0) == 0)
    def _():
        for seg in range(N_SEG):
            cols = slice(seg * SEG, (seg + 1) * SEG)
            wb_ref[:, cols] = w_ref[:, cols].astype(BF16)

    x = x_ref[...]
    h = (x * _rms(x) * gain_ref[...]).astype(BF16)

    def proj(seg):
        return jnp.dot(h, wb_ref[:, seg * SEG:(seg + 1) * SEG], preferred_element_type=F32)

    cos = cos_ref[...]
    sin = sin_ref[...]
    low = _low_half_mask(cos.shape)
    half = RET_DK // 2
    first_half = (lax.broadcasted_iota(jnp.int32, cos.shape, 1) % RET_DK) < half

    def rotary_store(y, out_ref, scale):
        for j in range(SEG // LANES):
            ys = y[:, j * LANES:(j + 1) * LANES]
            partner = jnp.where(first_half, pltpu.roll(ys, LANES - half, 1),
                                pltpu.roll(ys, half, 1))
            r = ys * cos + partner * sin
            if scale != 1.0:
                r = r * scale
            out_ref[:, j * LANES:(j + 1) * LANES] = r.astype(out_ref.dtype)

    def qknorm_store(y, g_ref, out_ref, scale):
        g = g_ref[...]
        for j in range(SEG // LANES):
            ys = y[:, j * LANES:(j + 1) * LANES]
            sq = ys * ys
            s_lo = jnp.sum(jnp.where(low, sq, 0.0), axis=-1, keepdims=True)
            s_hi = jnp.sum(jnp.where(low, 0.0, sq), axis=-1, keepdims=True)
            inv = jnp.where(low, lax.rsqrt(s_lo * (1.0 / DIFF_DQK) + EPS),
                            lax.rsqrt(s_hi * (1.0 / DIFF_DQK) + EPS))
            r = ys * inv * g
            if scale != 1.0:
                r = r * scale
            out_ref[:, j * LANES:(j + 1) * LANES] = r.astype(out_ref.dtype)

    rotary_store(proj(0), rq_ref, 1.0)
    rotary_store(proj(1), rk_ref, RET_DK ** -0.5)
    rv_ref[...] = proj(2).astype(BF16)
    g = proj(3)
    rg_ref[...] = (g * jax.nn.sigmoid(g)).astype(BF16)
    qknorm_store(proj(4), qg_ref, dq_ref, DIFF_DQK ** -0.5 * LOG2E)
    qknorm_store(proj(5), kg_ref, dk_ref, 1.0)
    dv = proj(6)
    ones = jnp.ones((VT_ROWS - DIFF_DV, dv.shape[0]), BF16)
    for hh in range(DIFF_HEADS):
        dvt_ref[hh * VT_ROWS:hh * VT_ROWS + DIFF_DV, :] = (
            dv[:, hh * DIFF_DV:(hh + 1) * DIFF_DV].T.astype(BF16))
        dvt_ref[hh * VT_ROWS + DIFF_DV:(hh + 1) * VT_ROWS, :] = ones


def _in_proj(x2, gain, w_in, cos_t, sin_t, qg, kg, seq, tm):
    T, D = x2.shape
    n_pos_blocks = seq // tm
    tok = lambda i: (i, 0)
    const = lambda i: (0, 0)
    pos = lambda i: (i % n_pos_blocks, 0)
    out = jax.ShapeDtypeStruct((T, SEG), BF16)
    return pl.pallas_call(
        _in_proj_kernel,
        grid=(T // tm,),
        in_specs=[
            pl.BlockSpec((tm, D), tok),
            pl.BlockSpec((1, D), const),
            pl.BlockSpec((D, N_SEG * SEG), const, pipeline_mode=pl.Buffered(1)),
            pl.BlockSpec((tm, LANES), pos),
            pl.BlockSpec((tm, LANES), pos),
            pl.BlockSpec((1, LANES), const),
            pl.BlockSpec((1, LANES), const),
        ],
        out_specs=[pl.BlockSpec((tm, SEG), tok)] * (N_SEG - 1)
        + [pl.BlockSpec((DIFF_HEADS * VT_ROWS, tm), lambda i: (0, i))],
        out_shape=[out] * (N_SEG - 1)
        + [jax.ShapeDtypeStruct((DIFF_HEADS * VT_ROWS, T), BF16)],
        scratch_shapes=[pltpu.VMEM((D, N_SEG * SEG), BF16)],
        compiler_params=pltpu.CompilerParams(
            dimension_semantics=("arbitrary",), vmem_limit_bytes=VMEM_LIMIT),
        name="in_proj",
    )(x2, gain, w_in, cos_t, sin_t, qg, kg)


def _retention_kernel(q_ref, k_ref, v_ref, g_ref, dmask_ref, zeta_ref, xi_ref, gchunk_ref,
                      gs_ref, o_ref, state_ref, *, chunk, n_chunks):
    @pl.when(pl.program_id(1) == 0)
    def _():
        state_ref[...] = jnp.zeros_like(state_ref)

    n_slabs = RET_WIDTH // LANES
    low = _low_half_mask((chunk, LANES))
    row = lax.broadcasted_iota(jnp.int32, (LANES, LANES), 0)
    col = lax.broadcasted_iota(jnp.int32, (LANES, LANES), 1)
    blockdiag = (row < RET_DK) == (col < RET_DK)
    nt = (((1,), (1,)), ((), ()))
    tn = (((0,), (0,)), ((), ()))

    for c in range(n_chunks):
        rows = slice(c * chunk, (c + 1) * chunk)
        for s in range(n_slabs):
            sl = slice(s * LANES, (s + 1) * LANES)
            zeta = zeta_ref[:, sl]
            xi = xi_ref[:, sl]
            gch = gchunk_ref[:, sl]
            q = q_ref[rows, sl]
            k = k_ref[rows, sl]
            v = v_ref[rows, sl]
            zero = jnp.zeros_like(q)
            q2 = jnp.concatenate([jnp.where(low, q, zero), jnp.where(low, zero, q)], axis=0)
            s2 = lax.dot_general(q2, k, nt, preferred_element_type=F32) * dmask_ref[s]
            o2 = jnp.dot(s2.astype(BF16), v, preferred_element_type=F32)
            r_prev = state_ref[s]
            o_inter = jnp.dot(q, r_prev.astype(BF16), preferred_element_type=F32) * xi
            o = jnp.where(low, o2[:chunk], o2[chunk:]) + o_inter
            kz = (k.astype(F32) * zeta).astype(BF16)
            kv = lax.dot_general(kz, v, tn, preferred_element_type=F32)
            state_ref[s] = r_prev * gch + jnp.where(blockdiag, kv, 0.0)
            sq = o * o
            m_lo = jnp.sum(jnp.where(low, sq, 0.0), axis=-1, keepdims=True)
            m_hi = jnp.sum(jnp.where(low, 0.0, sq), axis=-1, keepdims=True)
            inv = jnp.where(low, lax.rsqrt(m_lo * (1.0 / RET_DV) + EPS),
                            lax.rsqrt(m_hi * (1.0 / RET_DV) + EPS))
            out = o * inv * g_ref[rows, sl].astype(F32) * gs_ref[:, sl]
            o_ref[rows, sl] = out.astype(o_ref.dtype)


def _retention(rq, rk, rv, rg, dmask, zeta, xi, gchunk, gs_ret, batch, seq, ts, chunk):
    T = rq.shape[0]
    nblk = seq // ts
    tok = lambda b, i: (b * nblk + i, 0)
    c2 = lambda b, i: (0, 0)
    c3 = lambda b, i: (0, 0, 0)
    kern = functools.partial(_retention_kernel, chunk=chunk, n_chunks=ts // chunk)
    return pl.pallas_call(
        kern,
        grid=(batch, nblk),
        in_specs=[pl.BlockSpec((ts, RET_WIDTH), tok)] * 4 + [
            pl.BlockSpec((RET_HEADS // 2, 2 * chunk, chunk), c3),
            pl.BlockSpec((chunk, RET_WIDTH), c2),
            pl.BlockSpec((chunk, RET_WIDTH), c2),
            pl.BlockSpec((1, RET_WIDTH), c2),
            pl.BlockSpec((1, RET_WIDTH), c2),
        ],
        out_specs=pl.BlockSpec((ts, RET_WIDTH), tok),
        out_shape=jax.ShapeDtypeStruct((T, RET_WIDTH), BF16),
        scratch_shapes=[pltpu.VMEM((RET_WIDTH // LANES, LANES, LANES), F32)],
        compiler_params=pltpu.CompilerParams(
            dimension_semantics=("parallel", "arbitrary"), vmem_limit_bytes=VMEM_LIMIT),
        name="retention",
    )(rq, rk, rv, rg, dmask, zeta, xi, gchunk, gs_ret)


def _diff_attn_kernel(q_ref, k_ref, vt_ref, lam_ref, subln_ref, gs_ref, *refs,
                      n_cast, heads, tq, tk, rb, lam_init):
    w_refs = refs[:n_cast]
    o_ref = refs[n_cast]
    wb_refs = refs[n_cast + 1:2 * n_cast + 1]
    scratch = refs[2 * n_cast + 1:]
    for hh in range(heads):
        lanes = pl.ds(hh * LANES, LANES)
        _diff_attn_head(
            q_ref.at[:, lanes], k_ref.at[:, lanes],
            vt_ref.at[pl.ds(hh * VT_ROWS, VT_ROWS), :], lam_ref, subln_ref,
            gs_ref.at[:, lanes], o_ref.at[:, lanes],
            w_refs if hh == 0 else (), wb_refs if hh == 0 else (),
            *[ref.at[hh] for ref in scratch], tq=tq, tk=tk, rb=rb, lam_init=lam_init)


def _diff_attn_head(q_ref, k_ref, vt_ref, lam_ref, subln_ref, gs_ref, o_ref, w_refs, wb_refs,
                    qs_ref, s_ref, mc_ref, m_ref, acc_ref, *, tq, tk, rb, lam_init):
    i = pl.program_id(2)
    q = q_ref[...]
    low = _low_half_mask(q.shape)
    zero = jnp.zeros_like(q)
    qs_ref[:tq, :] = jnp.where(low, q, zero)
    qs_ref[tq:, :] = jnp.where(low, zero, q)
    nt = (((1,), (1,)), ((), ()))

    n_blocks = 2 * tq // rb
    n_diag = tq // tk
    n_full = i * n_diag

    def diag_kind(d, b):
        q_first = (b * rb) % tq
        if d * tk > q_first + rb - 1:
            return "skip"
        return "full" if (d + 1) * tk - 1 <= q_first else "mask"

    def scores(j, b, key_offset=None):
        start = pl.multiple_of(j * tk, tk)
        cols = slice(b * rb, (b + 1) * rb)
        s = lax.dot_general(k_ref[pl.ds(start, tk), :], qs_ref[cols, :], nt,
                            preferred_element_type=F32)
        if key_offset is not None:
            kpos = lax.broadcasted_iota(jnp.int32, s.shape, 0) + key_offset
            qpos = lax.broadcasted_iota(jnp.int32, s.shape, 1) + (b * rb) % tq
            s = jnp.where(kpos <= qpos, s, -jnp.inf)
        s_ref[:, cols] = s
        mc_ref[:, cols] = jnp.broadcast_to(jnp.max(s, axis=0, keepdims=True),
                                           (F32_SUBLANES, rb))

    def consume(j, b):
        start = pl.multiple_of(j * tk, tk)
        cols = slice(b * rb, (b + 1) * rb)
        m_prev = m_ref[:, cols]
        m_new = jnp.maximum(m_prev, mc_ref[:, cols])
        alpha = jnp.exp2(m_prev - m_new)
        p = jnp.exp2(s_ref[:, cols] - m_new[0:1, :])
        pv = jnp.dot(vt_ref[:, pl.ds(start, tk)], p.astype(BF16),
                     preferred_element_type=F32)
        acc_ref[:, cols] = alpha[0:1, :] * acc_ref[:, cols] + pv
        m_ref[:, cols] = m_new

    ahead = n_blocks // 2
    order = sorted(range(n_blocks), key=lambda b: -((b * rb) % tq))
    assert all(diag_kind(0, b) == "full" for b in order[:ahead])
    for b in order[:ahead]:
        scores(0, b)
    m_ref[...] = jnp.full_like(m_ref, -jnp.inf)
    acc_ref[...] = jnp.zeros_like(acc_ref)

    def full_chunks(j0, count):
        for u in range(count):
            j = j0 + u
            for pos, b in enumerate(order):
                if pos + ahead < n_blocks:
                    scores(j, order[pos + ahead])
                else:
                    scores(j + 1, order[pos + ahead - n_blocks])
                consume(j, b)

    def body(t, carry):
        full_chunks(t * (2 * n_diag), 2 * n_diag)
        return carry

    lax.fori_loop(0, i // 2, body, 0)

    @pl.when(i % 2 == 1)
    def _():
        full_chunks((i - 1) * n_diag, n_diag)

    lv = lam_ref[...]
    lam = (jnp.exp(jnp.sum(lv[0:1] * lv[1:2], axis=-1, keepdims=True))
           - jnp.exp(jnp.sum(lv[2:3] * lv[3:4], axis=-1, keepdims=True)) + lam_init)

    out_gain = subln_ref[...] * (1.0 - lam_init) * gs_ref[...]

    def finalize(d):
        a0 = acc_ref[:, d * tk:(d + 1) * tk]
        a1 = acc_ref[:, tq + d * tk:tq + (d + 1) * tk]
        o0 = a0[:DIFF_DV, :] * (1.0 / a0[DIFF_DV:DIFF_DV + 1, :])
        o1 = a1[:DIFF_DV, :] * (1.0 / a1[DIFF_DV:DIFF_DV + 1, :])
        o = (o0 - lam * o1).T
        o_ref[d * tk:(d + 1) * tk, :] = (o * _rms(o) * out_gain).astype(o_ref.dtype)

    for d in range(n_diag):
        for pos, b in enumerate(order):
            nd, npos = ((d, pos + ahead) if pos + ahead < n_blocks
                        else (d + 1, pos + ahead - n_blocks))
            if nd < n_diag and diag_kind(nd, order[npos]) != "skip":
                masked = diag_kind(nd, order[npos]) == "mask"
                scores(n_full + nd, order[npos], nd * tk if masked else None)
            if diag_kind(d, b) != "skip":
                consume(n_full + d, b)
        finalize(d)
        if d == 0:
            for w_ref, wb_ref in zip(w_refs, wb_refs):
                wb_ref[...] = w_ref[...].astype(wb_ref.dtype)


def _diff_attn(dq, dk, dvt, lam_p, subln, gs_diff, cast_weights, batch, seq, tq, tk, rb, heads,
               lam_init):
    T = dq.shape[0]
    nq = seq // tq
    n_groups = DIFF_HEADS // heads
    n_steps = batch * n_groups * nq
    assert seq % tq == 0 and tq % tk == 0 and tq % rb == 0 and DIFF_HEADS % heads == 0

    def cast_spec(w):
        rows = w.shape[0]
        n_blk = max(n for n in range(1, n_steps + 1)
                    if n_steps % n == 0 and rows % (n * BF16_SUBLANES) == 0)
        hold = n_steps // n_blk
        return pl.BlockSpec(
            (rows // n_blk, w.shape[1]),
            lambda b, g, i: (((b * n_groups + g) * nq + i) // hold, 0))

    w_specs = [cast_spec(w) for w in cast_weights]
    kern = functools.partial(_diff_attn_kernel, n_cast=len(cast_weights), heads=heads, tq=tq,
                             tk=tk, rb=rb, lam_init=lam_init)
    per_head = lambda shape, dtype: pltpu.VMEM((heads,) + shape, dtype)
    outs = pl.pallas_call(
        kern,
        grid=(batch, n_groups, nq),
        in_specs=[
            pl.BlockSpec((tq, heads * LANES), lambda b, g, i: (b * nq + i, g)),
            pl.BlockSpec((seq, heads * LANES), lambda b, g, i: (b, g)),
            pl.BlockSpec((heads * VT_ROWS, seq), lambda b, g, i: (g, b)),
            pl.BlockSpec((4, DIFF_DQK), lambda b, g, i: (0, 0)),
            pl.BlockSpec((1, LANES), lambda b, g, i: (0, 0)),
            pl.BlockSpec((1, heads * LANES), lambda b, g, i: (0, g)),
        ] + w_specs,
        out_specs=[pl.BlockSpec((tq, heads * LANES), lambda b, g, i: (b * nq + i, g))]
        + w_specs,
        out_shape=[jax.ShapeDtypeStruct((T, DIFF_WIDTH), BF16)]
        + [jax.ShapeDtypeStruct(w.shape, BF16) for w in cast_weights],
        scratch_shapes=[per_head((2 * tq, LANES), BF16), per_head((tk, 2 * tq), F32),
                        per_head((F32_SUBLANES, 2 * tq), F32),
                        per_head((F32_SUBLANES, 2 * tq), F32),
                        per_head((VT_ROWS, 2 * tq), F32)],
        compiler_params=pltpu.CompilerParams(
            dimension_semantics=("parallel", "parallel", "arbitrary"),
            vmem_limit_bytes=VMEM_LIMIT),
        name="diff_attn",
    )(dq, dk, dvt, lam_p, subln, gs_diff, *cast_weights)
    return outs[0], outs[1:]


def _mem_kv_kernel(mem_ref, gain_ref, w_ref, kg_ref, k_ref, v_ref):
    m = mem_ref[...]
    h = (m * _rms(m) * gain_ref[...]).astype(BF16)
    D = m.shape[1]
    dh = D // XATTN_HEADS
    k = jnp.dot(h, w_ref[:, :D], preferred_element_type=F32)
    v_ref[...] = jnp.dot(h, w_ref[:, D:], preferred_element_type=F32).astype(v_ref.dtype)
    for hh in range(XATTN_HEADS):
        kh = k[:, hh * dh:(hh + 1) * dh]
        k_ref[:, hh * dh:(hh + 1) * dh] = (kh * _rms(kh) * kg_ref[...]).astype(k_ref.dtype)


def _mem_kv(mem2, gain, xkv, kg, batch, mem_len):
    BM, D = mem2.shape
    const = lambda b: (0, 0)
    out = jax.ShapeDtypeStruct((BM, D), BF16)
    return pl.pallas_call(
        _mem_kv_kernel,
        grid=(batch,),
        in_specs=[
            pl.BlockSpec((mem_len, D), lambda b: (b, 0)),
            pl.BlockSpec((1, D), const),
            pl.BlockSpec((D, 2 * D), const),
            pl.BlockSpec((1, D // XATTN_HEADS), const),
        ],
        out_specs=[pl.BlockSpec((mem_len, D), lambda b: (b, 0))] * 2,
        out_shape=[out, out],
        compiler_params=pltpu.CompilerParams(
            dimension_semantics=("parallel",), vmem_limit_bytes=VMEM_LIMIT),
        name="mem_kv",
    )(mem2, gain, xkv, kg)


def _mix_xattn_kernel(x_ref, oret_ref, odiff_ref, wout_ref, nx_ref, xq_ref, qg_ref,
                      kmem_ref, vmem_ref, xo_ref, out_ref):
    D = x_ref.shape[1]
    dh = D // XATTN_HEADS
    x1 = (x_ref[...]
          + jnp.dot(oret_ref[...], wout_ref[:RET_WIDTH, :], preferred_element_type=F32)
          + jnp.dot(odiff_ref[...], wout_ref[RET_WIDTH:, :], preferred_element_type=F32))
    h = (x1 * _rms(x1) * nx_ref[...]).astype(BF16)
    q = jnp.dot(h, xq_ref[...], preferred_element_type=F32)
    nt = (((1,), (1,)), ((), ()))
    outs = []
    for hh in range(XATTN_HEADS):
        sl = slice(hh * dh, (hh + 1) * dh)
        qh = q[:, sl]
        qn = (qh * _rms(qh) * qg_ref[...] * (dh ** -0.5)).astype(BF16)
        s = lax.dot_general(qn, kmem_ref[:, sl], nt, preferred_element_type=F32)
        e = jnp.exp(s - jnp.max(s, axis=-1, keepdims=True))
        p = e * (1.0 / jnp.sum(e, axis=-1, keepdims=True))
        outs.append(jnp.dot(p.astype(BF16), vmem_ref[:, sl],
                            preferred_element_type=F32).astype(BF16))
    o = jnp.concatenate(outs, axis=1)
    out_ref[...] = x1 + jnp.dot(o, xo_ref[...], preferred_element_type=F32)


def _mix_xattn(x2, o_ret, o_diff, w_out, nx, xq, qg, kmem, vmem, xo, seq, mem_len, tm):
    T, D = x2.shape
    nblk = seq // tm
    tok = lambda i: (i, 0)
    const = lambda i: (0, 0)
    memb = lambda i: (i // nblk, 0)
    return pl.pallas_call(
        _mix_xattn_kernel,
        grid=(T // tm,),
        in_specs=[
            pl.BlockSpec((tm, D), tok),
            pl.BlockSpec((tm, RET_WIDTH), tok),
            pl.BlockSpec((tm, DIFF_WIDTH), tok),
            pl.BlockSpec((D, D), const),
            pl.BlockSpec((1, D), const),
            pl.BlockSpec((D, D), const),
            pl.BlockSpec((1, D // XATTN_HEADS), const),
            pl.BlockSpec((mem_len, D), memb),
            pl.BlockSpec((mem_len, D), memb),
            pl.BlockSpec((D, D), const),
        ],
        out_specs=pl.BlockSpec((tm, D), tok),
        out_shape=jax.ShapeDtypeStruct((T, D), F32),
        compiler_params=pltpu.CompilerParams(
            dimension_semantics=("parallel",), vmem_limit_bytes=VMEM_LIMIT),
        name="mix_xattn",
    )(x2, o_ret, o_diff, w_out, nx, xq, qg, kmem, vmem, xo)


def _ffn_kernel(x_ref, gain_ref, wg_ref, wu_ref, wd_ref, out_ref, *, hidden_chunk):
    x = x_ref[...]
    h = (x * _rms(x) * gain_ref[...]).astype(BF16)
    hidden = wg_ref.shape[1]
    acc = x
    for c in range(hidden // hidden_chunk):
        sl = slice(c * hidden_chunk, (c + 1) * hidden_chunk)
        g = jnp.dot(h, wg_ref[:, sl], preferred_element_type=F32)
        u = jnp.dot(h, wu_ref[:, sl], preferred_element_type=F32)
        a = (g * jax.nn.sigmoid(g) * u).astype(BF16)
        acc = acc + jnp.dot(a, wd_ref[sl, :], preferred_element_type=F32)
    out_ref[...] = acc


def _ffn(x2, gain, wg, wu, wd, tm, hidden_chunk):
    T, D = x2.shape
    hidden = wg.shape[1]
    tok = lambda i: (i, 0)
    const = lambda i: (0, 0)
    kern = functools.partial(_ffn_kernel, hidden_chunk=hidden_chunk)
    return pl.pallas_call(
        kern,
        grid=(T // tm,),
        in_specs=[
            pl.BlockSpec((tm, D), tok),
            pl.BlockSpec((1, D), const),
            pl.BlockSpec((D, hidden), const, pipeline_mode=pl.Buffered(1)),
            pl.BlockSpec((D, hidden), const, pipeline_mode=pl.Buffered(1)),
            pl.BlockSpec((hidden, D), const, pipeline_mode=pl.Buffered(1)),
        ],
        out_specs=pl.BlockSpec((tm, D), tok),
        out_shape=jax.ShapeDtypeStruct((T, D), F32),
        compiler_params=pltpu.CompilerParams(
            dimension_semantics=("parallel",), vmem_limit_bytes=VMEM_LIMIT),
        name="ffn",
    )(x2, gain, wg, wu, wd)


def _rotary_tables(seq):
    half = RET_DK // 2
    inv = 1.0 / (RET_THETA_BASE ** np.linspace(0.0, 1.0, half))
    ang = np.arange(seq, dtype=np.float64)[:, None] * inv[None, :]
    cos, sin = np.cos(ang), np.sin(ang)
    cos_t = np.tile(cos, (1, LANES // half))
    sin_t = np.tile(np.concatenate([-sin, sin], axis=1), (1, LANES // RET_DK))
    return cos_t.astype(np.float32), sin_t.astype(np.float32)


def _decay_tables(chunk):
    H = RET_HEADS
    log_g = np.log1p(-(2.0 ** (-5.0 - np.arange(H, dtype=np.float64))))
    idx = np.arange(chunk, dtype=np.float64)
    rel = idx[:, None] - idx[None, :]
    dmask = np.where(rel >= 0, np.exp(log_g[:, None, None] * np.maximum(rel, 0.0)), 0.0)
    zeta = np.exp(log_g[:, None] * (chunk - 1.0 - idx))
    xi = np.exp(log_g[:, None] * (idx + 1.0))
    g_chunk = np.exp(log_g * chunk)
    widen = lambda t: np.repeat(t.T, RET_DK, axis=1)
    f32 = lambda t: t.astype(np.float32)
    dmask = dmask.reshape(H // 2, 2 * chunk, chunk)
    return (f32(dmask), f32(widen(zeta)), f32(widen(xi)),
            f32(np.repeat(g_chunk, RET_DK)[None, :]))


def kernel(x, mem, norm_mix, w_in, diff_q_gain, diff_k_gain, diff_lambda, diff_subln,
           group_scale, w_out, norm_x, norm_mem, xq, xkv, xq_gain, xk_gain, xo,
           norm_ffn, w_gate, w_up, w_down):
    B, S, D = x.shape
    M = mem.shape[1]
    depth = norm_mix.shape[0]
    x2 = x.reshape(B * S, D)
    mem2 = mem.reshape(B * M, D)

    ret_chunk = 256
    cos_t, sin_t = _rotary_tables(S)
    dmask, zeta, xi, gchunk = _decay_tables(ret_chunk)
    row = lambda a: a.reshape(1, -1)
    twice = lambda a: jnp.tile(a.reshape(1, -1), (1, 2))

    for l in range(depth):
        lam_init = 0.8 - 0.6 * math.exp(-0.3 * l)
        rq, rk, rv, rg, dq, dk, dv = _in_proj(
            x2, row(norm_mix[l]), w_in[l], cos_t, sin_t,
            twice(diff_q_gain[l]), twice(diff_k_gain[l]), S, tm=1024)
        o_ret = _retention(rq, rk, rv, rg, dmask, zeta, xi, gchunk,
                           row(group_scale[l, :RET_WIDTH]), B, S, ts=1024, chunk=ret_chunk)
        later_weights = [w_out[l], xq[l], xkv[l], xo[l], w_gate[l], w_up[l], w_down[l]]
        o_diff, (w_out_b, xq_b, xkv_b, xo_b, w_gate_b, w_up_b, w_down_b) = _diff_attn(
            dq, dk, dv, diff_lambda[l], row(diff_subln[l]), row(group_scale[l, RET_WIDTH:]),
            later_weights, B, S, tq=1024, tk=512, rb=512, heads=2, lam_init=lam_init)
        kmem, vmem = _mem_kv(mem2, row(norm_mem[l]), xkv_b, row(xk_gain[l]), B, M)
        x2 = _mix_xattn(x2, o_ret, o_diff, w_out_b, row(norm_x[l]), xq_b, row(xq_gain[l]),
                        kmem, vmem, xo_b, S, M, tm=1024)
        x2 = _ffn(x2, row(norm_ffn[l]), w_gate_b, w_up_b, w_down_b, tm=1024, hidden_chunk=256)
    return x2.reshape(B, S, D)
```

```python
import functools
import math

import jax
import jax.numpy as jnp
import numpy as np
from jax import lax
from jax.experimental import pallas as pl
from jax.experimental.pallas import tpu as pltpu

F32 = jnp.float32
BF16 = jnp.bfloat16

EPS = 1e-6
LANES = 128
VMEM_LIMIT = 56 * 1024 * 1024

RET_HEADS = 8
RET_DK = 64
RET_DV = 64
RET_WIDTH = 512
RET_THETA_BASE = 10000.0
DIFF_HEADS = 4
DIFF_DQK = 64
DIFF_DV = 128
DIFF_WIDTH = 512
XATTN_HEADS = 4
SEG = 512
N_SEG = 7
LOG2E = math.log2(math.e)
F32_SUBLANES = 8
BF16_SUBLANES = 16
VT_ROWS = DIFF_DV + BF16_SUBLANES


def _rms(x, axis=-1):
    return lax.rsqrt(jnp.mean(x * x, axis=axis, keepdims=True) + EPS)


def _low_half_mask(shape):
    lane = lax.broadcasted_iota(jnp.int32, shape, len(shape) - 1)
    return (lane % LANES) < (LANES // 2)


def _in_proj_kernel(x_ref, gain_ref, w_ref, cos_ref, sin_ref, qg_ref, kg_ref,
                    rq_ref, rk_ref, rv_ref, rg_ref, dq_ref, dk_ref, dvt_ref, wb_ref):
    @pl.when(pl.program_id(0) == 0)
    def _():
        for seg in range(N_SEG):
            cols = slice(seg * SEG, (seg + 1) * SEG)
            wb_ref[:, cols] = w_ref[:, cols].astype(BF16)

    x = x_ref[...]
    h = (x * _rms(x) * gain_ref[...]).astype(BF16)

    def proj(seg):
        return jnp.dot(h, wb_ref[:, seg * SEG:(seg + 1) * SEG], preferred_element_type=F32)

    cos = cos_ref[...]
    sin = sin_ref[...]
    low = _low_half_mask(cos.shape)
    half = RET_DK // 2
    first_half = (lax.broadcasted_iota(jnp.int32, cos.shape, 1) % RET_DK) < half

    def rotary_store(y, out_ref, scale):
        for j in range(SEG // LANES):
            ys = y[:, j * LANES:(j + 1) * LANES]
            partner = jnp.where(first_half, pltpu.roll(ys, LANES - half, 1),
                                pltpu.roll(ys, half, 1))
            r = ys * cos + partner * sin
            if scale != 1.0:
                r = r * scale
            out_ref[:, j * LANES:(j + 1) * LANES] = r.astype(out_ref.dtype)

    def qknorm_store(y, g_ref, out_ref, scale):
        g = g_ref[...]
        for j in range(SEG // LANES):
            ys = y[:, j * LANES:(j + 1) * LANES]
            sq = ys * ys
            s_lo = jnp.sum(jnp.where(low, sq, 0.0), axis=-1, keepdims=True)
            s_hi = jnp.sum(jnp.where(low, 0.0, sq), axis=-1, keepdims=True)
            inv = jnp.where(low, lax.rsqrt(s_lo * (1.0 / DIFF_DQK) + EPS),
                            lax.rsqrt(s_hi * (1.0 / DIFF_DQK) + EPS))
            r = ys * inv * g
            if scale != 1.0:
                r = r * scale
            out_ref[:, j * LANES:(j + 1) * LANES] = r.astype(out_ref.dtype)

    rotary_store(proj(0), rq_ref, 1.0)
    rotary_store(proj(1), rk_ref, RET_DK ** -0.5)
    rv_ref[...] = proj(2).astype(BF16)
    g = proj(3)
    rg_ref[...] = (g * jax.nn.sigmoid(g)).astype(BF16)
    qknorm_store(proj(4), qg_ref, dq_ref, DIFF_DQK ** -0.5 * LOG2E)
    qknorm_store(proj(5), kg_ref, dk_ref, 1.0)
    dv = proj(6)
    ones = jnp.ones((VT_ROWS - DIFF_DV, dv.shape[0]), BF16)
    for hh in range(DIFF_HEADS):
        dvt_ref[hh * VT_ROWS:hh * VT_ROWS + DIFF_DV, :] = (
            dv[:, hh * DIFF_DV:(hh + 1) * DIFF_DV].T.astype(BF16))
        dvt_ref[hh * VT_ROWS + DIFF_DV:(hh + 1) * VT_ROWS, :] = ones


def _in_proj(x2, gain, w_in, cos_t, sin_t, qg, kg, seq, tm):
    T, D = x2.shape
    n_pos_blocks = seq // tm
    tok = lambda i: (i, 0)
    const = lambda i: (0, 0)
    pos = lambda i: (i % n_pos_blocks, 0)
    out = jax.ShapeDtypeStruct((T, SEG), BF16)
    return pl.pallas_call(
        _in_proj_kernel,
        grid=(T // tm,),
        in_specs=[
            pl.BlockSpec((tm, D), tok),
            pl.BlockSpec((1, D), const),
            pl.BlockSpec((D, N_SEG * SEG), const, pipeline_mode=pl.Buffered(1)),
            pl.BlockSpec((tm, LANES), pos),
            pl.BlockSpec((tm, LANES), pos),
            pl.BlockSpec((1, LANES), const),
            pl.BlockSpec((1, LANES), const),
        ],
        out_specs=[pl.BlockSpec((tm, SEG), tok)] * (N_SEG - 1)
        + [pl.BlockSpec((DIFF_HEADS * VT_ROWS, tm), lambda i: (0, i))],
        out_shape=[out] * (N_SEG - 1)
        + [jax.ShapeDtypeStruct((DIFF_HEADS * VT_ROWS, T), BF16)],
        scratch_shapes=[pltpu.VMEM((D, N_SEG * SEG), BF16)],
        compiler_params=pltpu.CompilerParams(
            dimension_semantics=("arbitrary",), vmem_limit_bytes=VMEM_LIMIT),
        name="in_proj",
    )(x2, gain, w_in, cos_t, sin_t, qg, kg)


def _retention_kernel(q_ref, k_ref, v_ref, g_ref, dmask_ref, zeta_ref, xi_ref, gchunk_ref,
                      gs_ref, o_ref, state_ref, *, chunk, n_chunks):
    @pl.when(pl.program_id(1) == 0)
    def _():
        state_ref[...] = jnp.zeros_like(state_ref)

    n_slabs = RET_WIDTH // LANES
    low = _low_half_mask((chunk, LANES))
    row = lax.broadcasted_iota(jnp.int32, (LANES, LANES), 0)
    col = lax.broadcasted_iota(jnp.int32, (LANES, LANES), 1)
    blockdiag = (row < RET_DK) == (col < RET_DK)
    nt = (((1,), (1,)), ((), ()))
    tn = (((0,), (0,)), ((), ()))

    for c in range(n_chunks):
        rows = slice(c * chunk, (c + 1) * chunk)
        for s in range(n_slabs):
            sl = slice(s * LANES, (s + 1) * LANES)
            zeta = zeta_ref[:, sl]
            xi = xi_ref[:, sl]
            gch = gchunk_ref[:, sl]
            q = q_ref[rows, sl]
            k = k_ref[rows, sl]
            v = v_ref[rows, sl]
            zero = jnp.zeros_like(q)
            q2 = jnp.concatenate([jnp.where(low, q, zero), jnp.where(low, zero, q)], axis=0)
            s2 = lax.dot_general(q2, k, nt, preferred_element_type=F32) * dmask_ref[s]
            o2 = jnp.dot(s2.astype(BF16), v, preferred_element_type=F32)
            r_prev = state_ref[s]
            o_inter = jnp.dot(q, r_prev.astype(BF16), preferred_element_type=F32) * xi
            o = jnp.where(low, o2[:chunk], o2[chunk:]) + o_inter
            kz = (k.astype(F32) * zeta).astype(BF16)
            kv = lax.dot_general(kz, v, tn, preferred_element_type=F32)
            state_ref[s] = r_prev * gch + jnp.where(blockdiag, kv, 0.0)
            sq = o * o
            m_lo = jnp.sum(jnp.where(low, sq, 0.0), axis=-1, keepdims=True)
            m_hi = jnp.sum(jnp.where(low, 0.0, sq), axis=-1, keepdims=True)
            inv = jnp.where(low, lax.rsqrt(m_lo * (1.0 / RET_DV) + EPS),
                            lax.rsqrt(m_hi * (1.0 / RET_DV) + EPS))
            out = o * inv * g_ref[rows, sl].astype(F32) * gs_ref[:, sl]
            o_ref[rows, sl] = out.astype(o_ref.dtype)


def _retention(rq, rk, rv, rg, dmask, zeta, xi, gchunk, gs_ret, batch, seq, ts, chunk):
    T = rq.shape[0]
    nblk = seq // ts
    tok = lambda b, i: (b * nblk + i, 0)
    c2 = lambda b, i: (0, 0)
    c3 = lambda b, i: (0, 0, 0)
    kern = functools.partial(_retention_kernel, chunk=chunk, n_chunks=ts // chunk)
    return pl.pallas_call(
        kern,
        grid=(batch, nblk),
        in_specs=[pl.BlockSpec((ts, RET_WIDTH), tok)] * 4 + [
            pl.BlockSpec((RET_HEADS // 2, 2 * chunk, chunk), c3),
            pl.BlockSpec((chunk, RET_WIDTH), c2),
            pl.BlockSpec((chunk, RET_WIDTH), c2),
            pl.BlockSpec((1, RET_WIDTH), c2),
            pl.BlockSpec((1, RET_WIDTH), c2),
        ],
        out_specs=pl.BlockSpec((ts, RET_WIDTH), tok),
        out_shape=jax.ShapeDtypeStruct((T, RET_WIDTH), BF16),
        scratch_shapes=[pltpu.VMEM((RET_WIDTH // LANES, LANES, LANES), F32)],
        compiler_params=pltpu.CompilerParams(
            dimension_semantics=("parallel", "arbitrary"), vmem_limit_bytes=VMEM_LIMIT),
        name="retention",
    )(rq, rk, rv, rg, dmask, zeta, xi, gchunk, gs_ret)


def _diff_attn_kernel(q_ref, k_ref, vt_ref, lam_ref, subln_ref, gs_ref, *refs,
                      n_cast, heads, tq, tk, rb, lam_init):
    w_refs = refs[:n_cast]
    o_ref = refs[n_cast]
    wb_refs = refs[n_cast + 1:2 * n_cast + 1]
    scratch = refs[2 * n_cast + 1:]
    for hh in range(heads):
        lanes = pl.ds(hh * LANES, LANES)
        _diff_attn_head(
            q_ref.at[:, lanes], k_ref.at[:, lanes],
            vt_ref.at[pl.ds(hh * VT_ROWS, VT_ROWS), :], lam_ref, subln_ref,
            gs_ref.at[:, lanes], o_ref.at[:, lanes],
            w_refs if hh == 0 else (), wb_refs if hh == 0 else (),
            *[ref.at[hh] for ref in scratch], tq=tq, tk=tk, rb=rb, lam_init=lam_init)


def _diff_attn_head(q_ref, k_ref, vt_ref, lam_ref, subln_ref, gs_ref, o_ref, w_refs, wb_refs,
                    qs_ref, s_ref, mc_ref, m_ref, acc_ref, *, tq, tk, rb, lam_init):
    i = pl.program_id(2)
    q = q_ref[...]
    low = _low_half_mask(q.shape)
    zero = jnp.zeros_like(q)
    qs_ref[:tq, :] = jnp.where(low, q, zero)
    qs_ref[tq:, :] = jnp.where(low, zero, q)
    nt = (((1,), (1,)), ((), ()))

    n_blocks = 2 * tq // rb
    n_diag = tq // tk
    n_full = i * n_diag

    def diag_kind(d, b):
        q_first = (b * rb) % tq
        if d * tk > q_first + rb - 1:
            return "skip"
        return "full" if (d + 1) * tk - 1 <= q_first else "mask"

    def scores(j, b, key_offset=None):
        start = pl.multiple_of(j * tk, tk)
        cols = slice(b * rb, (b + 1) * rb)
        s = lax.dot_general(k_ref[pl.ds(start, tk), :], qs_ref[cols, :], nt,
                            preferred_element_type=F32)
        if key_offset is not None:
            kpos = lax.broadcasted_iota(jnp.int32, s.shape, 0) + key_offset
            qpos = lax.broadcasted_iota(jnp.int32, s.shape, 1) + (b * rb) % tq
            s = jnp.where(kpos <= qpos, s, -jnp.inf)
        s_ref[:, cols] = s
        mc_ref[:, cols] = jnp.broadcast_to(jnp.max(s, axis=0, keepdims=True),
                                           (F32_SUBLANES, rb))

    def consume(j, b):
        start = pl.multiple_of(j * tk, tk)
        cols = slice(b * rb, (b + 1) * rb)
        m_prev = m_ref[:, cols]
        m_new = jnp.maximum(m_prev, mc_ref[:, cols])
        alpha = jnp.exp2(m_prev - m_new)
        p = jnp.exp2(s_ref[:, cols] - m_new[0:1, :])
        pv = jnp.dot(vt_ref[:, pl.ds(start, tk)], p.astype(BF16),
                     preferred_element_type=F32)
        acc_ref[:, cols] = alpha[0:1, :] * acc_ref[:, cols] + pv
        m_ref[:, cols] = m_new

    ahead = n_blocks // 2
    order = sorted(range(n_blocks), key=lambda b: -((b * rb) % tq))
    assert all(diag_kind(0, b) == "full" for b in order[:ahead])
    for b in order[:ahead]:
        scores(0, b)
    m_ref[...] = jnp.full_like(m_ref, -jnp.inf)
    acc_ref[...] = jnp.zeros_like(acc_ref)

    def full_chunks(j0, count):
        for u in range(count):
            j = j0 + u
            for pos, b in enumerate(order):
                if pos + ahead < n_blocks:
                    scores(j, order[pos + ahead])
                else:
                    scores(j + 1, order[pos + ahead - n_blocks])
                consume(j, b)

    def body(t, carry):
        full_chunks(t * (2 * n_diag), 2 * n_diag)
        return carry

    lax.fori_loop(0, i // 2, body, 0)

    @pl.when(i % 2 == 1)
    def _():
        full_chunks((i - 1) * n_diag, n_diag)

    lv = lam_ref[...]
    lam = (jnp.exp(jnp.sum(lv[0:1] * lv[1:2], axis=-1, keepdims=True))
           - jnp.exp(jnp.sum(lv[2:3] * lv[3:4], axis=-1, keepdims=True)) + lam_init)

    out_gain = subln_ref[...] * (1.0 - lam_init) * gs_ref[...]

    def finalize(d):
        a0 = acc_ref[:, d * tk:(d + 1) * tk]
        a1 = acc_ref[:, tq + d * tk:tq + (d + 1) * tk]
        o0 = a0[:DIFF_DV, :] * (1.0 / a0[DIFF_DV:DIFF_DV + 1, :])
        o1 = a1[:DIFF_DV, :] * (1.0 / a1[DIFF_DV:DIFF_DV + 1, :])
        o = (o0 - lam * o1).T
        o_ref[d * tk:(d + 1) * tk, :] = (o * _rms(o) * out_gain).astype(o_ref.dtype)

    for d in range(n_diag):
        for pos, b in enumerate(order):
            nd, npos = ((d, pos + ahead) if pos + ahead < n_blocks
                        else (d + 1, pos + ahead - n_blocks))
            if nd < n_diag and diag_kind(nd, order[npos]) != "skip":
                masked = diag_kind(nd, order[npos]) == "mask"
                scores(n_full + nd, order[npos], nd * tk if masked else None)
            if diag_kind(d, b) != "skip":
                consume(n_full + d, b)
        finalize(d)
        if d == 0:
            for w_ref, wb_ref in zip(w_refs, wb_refs):
                wb_ref[...] = w_ref[...].astype(wb_ref.dtype)


def _diff_attn(dq, dk, dvt, lam_p, subln, gs_diff, cast_weights, batch, seq, tq, tk, rb, heads,
               lam_init):
    T = dq.shape[0]
    nq = seq // tq
    n_groups = DIFF_HEADS // heads
    n_steps = batch * n_groups * nq
    assert seq % tq == 0 and tq % tk == 0 and tq % rb == 0 and DIFF_HEADS % heads == 0

    def cast_spec(w):
        rows = w.shape[0]
        n_blk = max(n for n in range(1, n_steps + 1)
                    if n_steps % n == 0 and rows % (n * BF16_SUBLANES) == 0)
        hold = n_steps // n_blk
        return pl.BlockSpec(
            (rows // n_blk, w.shape[1]),
            lambda b, g, i: (((b * n_groups + g) * nq + i) // hold, 0))

    w_specs = [cast_spec(w) for w in cast_weights]
    kern = functools.partial(_diff_attn_kernel, n_cast=len(cast_weights), heads=heads, tq=tq,
                             tk=tk, rb=rb, lam_init=lam_init)
    per_head = lambda shape, dtype: pltpu.VMEM((heads,) + shape, dtype)
    outs = pl.pallas_call(
        kern,
        grid=(batch, n_groups, nq),
        in_specs=[
            pl.BlockSpec((tq, heads * LANES), lambda b, g, i: (b * nq + i, g)),
            pl.BlockSpec((seq, heads * LANES), lambda b, g, i: (b, g)),
            pl.BlockSpec((heads * VT_ROWS, seq), lambda b, g, i: (g, b)),
            pl.BlockSpec((4, DIFF_DQK), lambda b, g, i: (0, 0)),
            pl.BlockSpec((1, LANES), lambda b, g, i: (0, 0)),
            pl.BlockSpec((1, heads * LANES), lambda b, g, i: (0, g)),
        ] + w_specs,
        out_specs=[pl.BlockSpec((tq, heads * LANES), lambda b, g, i: (b * nq + i, g))]
        + w_specs,
        out_shape=[jax.ShapeDtypeStruct((T, DIFF_WIDTH), BF16)]
        + [jax.ShapeDtypeStruct(w.shape, BF16) for w in cast_weights],
        scratch_shapes=[per_head((2 * tq, LANES), BF16), per_head((tk, 2 * tq), F32),
                        per_head((F32_SUBLANES, 2 * tq), F32),
                        per_head((F32_SUBLANES, 2 * tq), F32),
                        per_head((VT_ROWS, 2 * tq), F32)],
        compiler_params=pltpu.CompilerParams(
            dimension_semantics=("parallel", "parallel", "arbitrary"),
            vmem_limit_bytes=VMEM_LIMIT),
        name="diff_attn",
    )(dq, dk, dvt, lam_p, subln, gs_diff, *cast_weights)
    return outs[0], outs[1:]


def _mem_kv_kernel(mem_ref, gain_ref, w_ref, kg_ref, k_ref, v_ref):
    m = mem_ref[...]
    h = (m * _rms(m) * gain_ref[...]).astype(BF16)
    D = m.shape[1]
    dh = D // XATTN_HEADS
    k = jnp.dot(h, w_ref[:, :D], preferred_element_type=F32)
    v_ref[...] = jnp.dot(h, w_ref[:, D:], preferred_element_type=F32).astype(v_ref.dtype)
    for hh in range(XATTN_HEADS):
        kh = k[:, hh * dh:(hh + 1) * dh]
        k_ref[:, hh * dh:(hh + 1) * dh] = (kh * _rms(kh) * kg_ref[...]).astype(k_ref.dtype)


def _mem_kv(mem2, gain, xkv, kg, batch, mem_len):
    BM, D = mem2.shape
    const = lambda b: (0, 0)
    out = jax.ShapeDtypeStruct((BM, D), BF16)
    return pl.pallas_call(
        _mem_kv_kernel,
        grid=(batch,),
        in_specs=[
            pl.BlockSpec((mem_len, D), lambda b: (b, 0)),
            pl.BlockSpec((1, D), const),
            pl.BlockSpec((D, 2 * D), const),
            pl.BlockSpec((1, D // XATTN_HEADS), const),
        ],
        out_specs=[pl.BlockSpec((mem_len, D), lambda b: (b, 0))] * 2,
        out_shape=[out, out],
        compiler_params=pltpu.CompilerParams(
            dimension_semantics=("parallel",), vmem_limit_bytes=VMEM_LIMIT),
        name="mem_kv",
    )(mem2, gain, xkv, kg)


def _mix_xattn_kernel(x_ref, oret_ref, odiff_ref, wout_ref, nx_ref, xq_ref, qg_ref,
                      kmem_ref, vmem_ref, xo_ref, out_ref):
    D = x_ref.shape[1]
    dh = D // XATTN_HEADS
    x1 = (x_ref[...]
          + jnp.dot(oret_ref[...], wout_ref[:RET_WIDTH, :], preferred_element_type=F32)
          + jnp.dot(odiff_ref[...], wout_ref[RET_WIDTH:, :], preferred_element_type=F32))
    h = (x1 * _rms(x1) * nx_ref[...]).astype(BF16)
    q = jnp.dot(h, xq_ref[...], preferred_element_type=F32)
    nt = (((1,), (1,)), ((), ()))
    outs = []
    for hh in range(XATTN_HEADS):
        sl = slice(hh * dh, (hh + 1) * dh)
        qh = q[:, sl]
        qn = (qh * _rms(qh) * qg_ref[...] * (dh ** -0.5)).astype(BF16)
        s = lax.dot_general(qn, kmem_ref[:, sl], nt, preferred_element_type=F32)
        e = jnp.exp(s - jnp.max(s, axis=-1, keepdims=True))
        p = e * (1.0 / jnp.sum(e, axis=-1, keepdims=True))
        outs.append(jnp.dot(p.astype(BF16), vmem_ref[:, sl],
                            preferred_element_type=F32).astype(BF16))
    o = jnp.concatenate(outs, axis=1)
    out_ref[...] = x1 + jnp.dot(o, xo_ref[...], preferred_element_type=F32)


def _mix_xattn(x2, o_ret, o_diff, w_out, nx, xq, qg, kmem, vmem, xo, seq, mem_len, tm):
    T, D = x2.shape
    nblk = seq // tm
    tok = lambda i: (i, 0)
    const = lambda i: (0, 0)
    memb = lambda i: (i // nblk, 0)
    return pl.pallas_call(
        _mix_xattn_kernel,
        grid=(T // tm,),
        in_specs=[
            pl.BlockSpec((tm, D), tok),
            pl.BlockSpec((tm, RET_WIDTH), tok),
            pl.BlockSpec((tm, DIFF_WIDTH), tok),
            pl.BlockSpec((D, D), const),
            pl.BlockSpec((1, D), const),
            pl.BlockSpec((D, D), const),
            pl.BlockSpec((1, D // XATTN_HEADS), const),
            pl.BlockSpec((mem_len, D), memb),
            pl.BlockSpec((mem_len, D), memb),
            pl.BlockSpec((D, D), const),
        ],
        out_specs=pl.BlockSpec((tm, D), tok),
        out_shape=jax.ShapeDtypeStruct((T, D), F32),
        compiler_params=pltpu.CompilerParams(
            dimension_semantics=("parallel",), vmem_limit_bytes=VMEM_LIMIT),
        name="mix_xattn",
    )(x2, o_ret, o_diff, w_out, nx, xq, qg, kmem, vmem, xo)


def _ffn_kernel(x_ref, gain_ref, wg_ref, wu_ref, wd_ref, out_ref, *, hidden_chunk):
    x = x_ref[...]
    h = (x * _rms(x) * gain_ref[...]).astype(BF16)
    hidden = wg_ref.shape[1]
    acc = x
    for c in range(hidden // hidden_chunk):
        sl = slice(c * hidden_chunk, (c + 1) * hidden_chunk)
        g = jnp.dot(h, wg_ref[:, sl], preferred_element_type=F32)
        u = jnp.dot(h, wu_ref[:, sl], preferred_element_type=F32)
        a = (g * jax.nn.sigmoid(g) * u).astype(BF16)
        acc = acc + jnp.dot(a, wd_ref[sl, :], preferred_element_type=F32)
    out_ref[...] = acc


def _ffn(x2, gain, wg, wu, wd, tm, hidden_chunk):
    T, D = x2.shape
    hidden = wg.shape[1]
    tok = lambda i: (i, 0)
    const = lambda i: (0, 0)
    kern = functools.partial(_ffn_kernel, hidden_chunk=hidden_chunk)
    return pl.pallas_call(
        kern,
        grid=(T // tm,),
        in_specs=[
            pl.BlockSpec((tm, D), tok),
            pl.BlockSpec((1, D), const),
            pl.BlockSpec((D, hidden), const, pipeline_mode=pl.Buffered(1)),
            pl.BlockSpec((D, hidden), const, pipeline_mode=pl.Buffered(1)),
            pl.BlockSpec((hidden, D), const, pipeline_mode=pl.Buffered(1)),
        ],
        out_specs=pl.BlockSpec((tm, D), tok),
        out_shape=jax.ShapeDtypeStruct((T, D), F32),
        compiler_params=pltpu.CompilerParams(
            dimension_semantics=("parallel",), vmem_limit_bytes=VMEM_LIMIT),
        name="ffn",
    )(x2, gain, wg, wu, wd)


def _rotary_tables(seq):
    half = RET_DK // 2
    inv = 1.0 / (RET_THETA_BASE ** np.linspace(0.0, 1.0, half))
    ang = np.arange(seq, dtype=np.float64)[:, None] * inv[None, :]
    cos, sin = np.cos(ang), np.sin(ang)
    cos_t = np.tile(cos, (1, LANES // half))
    sin_t = np.tile(np.concatenate([-sin, sin], axis=1), (1, LANES // RET_DK))
    return cos_t.astype(np.float32), sin_t.astype(np.float32)


def _decay_tables(chunk):
    H = RET_HEADS
    log_g = np.log1p(-(2.0 ** (-5.0 - np.arange(H, dtype=np.float64))))
    idx = np.arange(chunk, dtype=np.float64)
    rel = idx[:, None] - idx[None, :]
    dmask = np.where(rel >= 0, np.exp(log_g[:, None, None] * np.maximum(rel, 0.0)), 0.0)
    zeta = np.exp(log_g[:, None] * (chunk - 1.0 - idx))
    xi = np.exp(log_g[:, None] * (idx + 1.0))
    g_chunk = np.exp(log_g * chunk)
    widen = lambda t: np.repeat(t.T, RET_DK, axis=1)
    f32 = lambda t: t.astype(np.float32)
    dmask = dmask.reshape(H // 2, 2 * chunk, chunk)
    return (f32(dmask), f32(widen(zeta)), f32(widen(xi)),
            f32(np.repeat(g_chunk, RET_DK)[None, :]))


def kernel(x, mem, norm_mix, w_in, diff_q_gain, diff_k_gain, diff_lambda, diff_subln,
           group_scale, w_out, norm_x, norm_mem, xq, xkv, xq_gain, xk_gain, xo,
           norm_ffn, w_gate, w_up, w_down):
    B, S, D = x.shape
    M = mem.shape[1]
    depth = norm_mix.shape[0]
    x2 = x.reshape(B * S, D)
    mem2 = mem.reshape(B * M, D)

    ret_chunk = 256
    cos_t, sin_t = _rotary_tables(S)
    dmask, zeta, xi, gchunk = _decay_tables(ret_chunk)
    row = lambda a: a.reshape(1, -1)
    twice = lambda a: jnp.tile(a.reshape(1, -1), (1, 2))

    for l in range(depth):
        lam_init = 0.8 - 0.6 * math.exp(-0.3 * l)
        rq, rk, rv, rg, dq, dk, dv = _in_proj(
            x2, row(norm_mix[l]), w_in[l], cos_t, sin_t,
            twice(diff_q_gain[l]), twice(diff_k_gain[l]), S, tm=1024)
        o_ret = _retention(rq, rk, rv, rg, dmask, zeta, xi, gchunk,
                           row(group_scale[l, :RET_WIDTH]), B, S, ts=1024, chunk=ret_chunk)
        later_weights = [w_out[l], xq[l], xkv[l], xo[l], w_gate[l], w_up[l], w_down[l]]
        o_diff, (w_out_b, xq_b, xkv_b, xo_b, w_gate_b, w_up_b, w_down_b) = _diff_attn(
            dq, dk, dv, diff_lambda[l], row(diff_subln[l]), row(group_scale[l, RET_WIDTH:]),
            later_weights, B, S, tq=1024, tk=512, rb=512, heads=2, lam_init=lam_init)
        kmem, vmem = _mem_kv(mem2, row(norm_mem[l]), xkv_b, row(xk_gain[l]), B, M)
        x2 = _mix_xattn(x2, o_ret, o_diff, w_out_b, row(norm_x[l]), xq_b, row(xq_gain[l]),
                        kmem, vmem, xo_b, S, M, tm=1024)
        x2 = _ffn(x2, row(norm_ffn[l]), w_gate_b, w_up_b, w_down_b, tm=512, hidden_chunk=256)
    return x2.reshape(B, S, D)
```

```python
import functools
import math

import jax
import jax.numpy as jnp
import numpy as np
from jax import lax
from jax.experimental import pallas as pl
from jax.experimental.pallas import tpu as pltpu

F32 = jnp.float32
BF16 = jnp.bfloat16

EPS = 1e-6
LANES = 128
VMEM_LIMIT = 56 * 1024 * 1024

RET_HEADS = 8
RET_DK = 64
RET_DV = 64
RET_WIDTH = 512
RET_THETA_BASE = 10000.0
DIFF_HEADS = 4
DIFF_DQK = 64
DIFF_DV = 128
DIFF_WIDTH = 512
XATTN_HEADS = 4
SEG = 512
N_SEG = 7
LOG2E = math.log2(math.e)
F32_SUBLANES = 8
BF16_SUBLANES = 16
VT_ROWS = DIFF_DV + BF16_SUBLANES


def _rms(x, axis=-1):
    return lax.rsqrt(jnp.mean(x * x, axis=axis, keepdims=True) + EPS)


def _low_half_mask(shape):
    lane = lax.broadcasted_iota(jnp.int32, shape, len(shape) - 1)
    return (lane % LANES) < (LANES // 2)


def _in_proj_kernel(x_ref, gain_ref, w_ref, cos_ref, sin_ref, qg_ref, kg_ref,
                    rq_ref, rk_ref, rv_ref, rg_ref, dq_ref, dk_ref, dvt_ref, wb_ref):
    @pl.when(pl.program_id(0) == 0)
    def _():
        for seg in range(N_SEG):
            cols = slice(seg * SEG, (seg + 1) * SEG)
            wb_ref[:, cols] = w_ref[:, cols].astype(BF16)

    x = x_ref[...]
    h = (x * _rms(x) * gain_ref[...]).astype(BF16)

    def proj(seg):
        return jnp.dot(h, wb_ref[:, seg * SEG:(seg + 1) * SEG], preferred_element_type=F32)

    cos = cos_ref[...]
    sin = sin_ref[...]
    low = _low_half_mask(cos.shape)
    half = RET_DK // 2
    first_half = (lax.broadcasted_iota(jnp.int32, cos.shape, 1) % RET_DK) < half

    def rotary_store(y, out_ref, scale):
        for j in range(SEG // LANES):
            ys = y[:, j * LANES:(j + 1) * LANES]
            partner = jnp.where(first_half, pltpu.roll(ys, LANES - half, 1),
                                pltpu.roll(ys, half, 1))
            r = ys * cos + partner * sin
            if scale != 1.0:
                r = r * scale
            out_ref[:, j * LANES:(j + 1) * LANES] = r.astype(out_ref.dtype)

    def qknorm_store(y, g_ref, out_ref, scale):
        g = g_ref[...]
        for j in range(SEG // LANES):
            ys = y[:, j * LANES:(j + 1) * LANES]
            sq = ys * ys
            s_lo = jnp.sum(jnp.where(low, sq, 0.0), axis=-1, keepdims=True)
            s_hi = jnp.sum(jnp.where(low, 0.0, sq), axis=-1, keepdims=True)
            inv = jnp.where(low, lax.rsqrt(s_lo * (1.0 / DIFF_DQK) + EPS),
                            lax.rsqrt(s_hi * (1.0 / DIFF_DQK) + EPS))
            r = ys * inv * g
            if scale != 1.0:
                r = r * scale
            out_ref[:, j * LANES:(j + 1) * LANES] = r.astype(out_ref.dtype)

    rotary_store(proj(0), rq_ref, 1.0)
    rotary_store(proj(1), rk_ref, RET_DK ** -0.5)
    rv_ref[...] = proj(2).astype(BF16)
    g = proj(3)
    rg_ref[...] = (g * jax.nn.sigmoid(g)).astype(BF16)
    qknorm_store(proj(4), qg_ref, dq_ref, DIFF_DQK ** -0.5 * LOG2E)
    qknorm_store(proj(5), kg_ref, dk_ref, 1.0)
    dv = proj(6)
    ones = jnp.ones((VT_ROWS - DIFF_DV, dv.shape[0]), BF16)
    for hh in range(DIFF_HEADS):
        dvt_ref[hh * VT_ROWS:hh * VT_ROWS + DIFF_DV, :] = (
            dv[:, hh * DIFF_DV:(hh + 1) * DIFF_DV].T.astype(BF16))
        dvt_ref[hh * VT_ROWS + DIFF_DV:(hh + 1) * VT_ROWS, :] = ones


def _in_proj(x2, gain, w_in, cos_t, sin_t, qg, kg, seq, tm):
    T, D = x2.shape
    n_pos_blocks = seq // tm
    tok = lambda i: (i, 0)
    const = lambda i: (0, 0)
    pos = lambda i: (i % n_pos_blocks, 0)
    out = jax.ShapeDtypeStruct((T, SEG), BF16)
    return pl.pallas_call(
        _in_proj_kernel,
        grid=(T // tm,),
        in_specs=[
            pl.BlockSpec((tm, D), tok),
            pl.BlockSpec((1, D), const),
            pl.BlockSpec((D, N_SEG * SEG), const, pipeline_mode=pl.Buffered(1)),
            pl.BlockSpec((tm, LANES), pos),
            pl.BlockSpec((tm, LANES), pos),
            pl.BlockSpec((1, LANES), const),
            pl.BlockSpec((1, LANES), const),
        ],
        out_specs=[pl.BlockSpec((tm, SEG), tok)] * (N_SEG - 1)
        + [pl.BlockSpec((DIFF_HEADS * VT_ROWS, tm), lambda i: (0, i))],
        out_shape=[out] * (N_SEG - 1)
        + [jax.ShapeDtypeStruct((DIFF_HEADS * VT_ROWS, T), BF16)],
        scratch_shapes=[pltpu.VMEM((D, N_SEG * SEG), BF16)],
        compiler_params=pltpu.CompilerParams(
            dimension_semantics=("arbitrary",), vmem_limit_bytes=VMEM_LIMIT),
        name="in_proj",
    )(x2, gain, w_in, cos_t, sin_t, qg, kg)


def _retention_kernel(q_ref, k_ref, v_ref, g_ref, dmask_ref, zeta_ref, xi_ref, gchunk_ref,
                      gs_ref, o_ref, state_ref, *, chunk, n_chunks):
    @pl.when(pl.program_id(1) == 0)
    def _():
        state_ref[...] = jnp.zeros_like(state_ref)

    n_slabs = RET_WIDTH // LANES
    low = _low_half_mask((chunk, LANES))
    row = lax.broadcasted_iota(jnp.int32, (LANES, LANES), 0)
    col = lax.broadcasted_iota(jnp.int32, (LANES, LANES), 1)
    blockdiag = (row < RET_DK) == (col < RET_DK)
    nt = (((1,), (1,)), ((), ()))
    tn = (((0,), (0,)), ((), ()))

    for c in range(n_chunks):
        rows = slice(c * chunk, (c + 1) * chunk)
        for s in range(n_slabs):
            sl = slice(s * LANES, (s + 1) * LANES)
            zeta = zeta_ref[:, sl]
            xi = xi_ref[:, sl]
            gch = gchunk_ref[:, sl]
            q = q_ref[rows, sl]
            k = k_ref[rows, sl]
            v = v_ref[rows, sl]
            zero = jnp.zeros_like(q)
            q2 = jnp.concatenate([jnp.where(low, q, zero), jnp.where(low, zero, q)], axis=0)
            s2 = lax.dot_general(q2, k, nt, preferred_element_type=F32) * dmask_ref[s]
            o2 = jnp.dot(s2.astype(BF16), v, preferred_element_type=F32)
            r_prev = state_ref[s]
            o_inter = jnp.dot(q, r_prev.astype(BF16), preferred_element_type=F32) * xi
            o = jnp.where(low, o2[:chunk], o2[chunk:]) + o_inter
            kz = (k.astype(F32) * zeta).astype(BF16)
            kv = lax.dot_general(kz, v, tn, preferred_element_type=F32)
            state_ref[s] = r_prev * gch + jnp.where(blockdiag, kv, 0.0)
            sq = o * o
            m_lo = jnp.sum(jnp.where(low, sq, 0.0), axis=-1, keepdims=True)
            m_hi = jnp.sum(jnp.where(low, 0.0, sq), axis=-1, keepdims=True)
            inv = jnp.where(low, lax.rsqrt(m_lo * (1.0 / RET_DV) + EPS),
                            lax.rsqrt(m_hi * (1.0 / RET_DV) + EPS))
            out = o * inv * g_ref[rows, sl].astype(F32) * gs_ref[:, sl]
            o_ref[rows, sl] = out.astype(o_ref.dtype)


def _retention(rq, rk, rv, rg, dmask, zeta, xi, gchunk, gs_ret, batch, seq, ts, chunk):
    T = rq.shape[0]
    nblk = seq // ts
    tok = lambda b, i: (b * nblk + i, 0)
    c2 = lambda b, i: (0, 0)
    c3 = lambda b, i: (0, 0, 0)
    kern = functools.partial(_retention_kernel, chunk=chunk, n_chunks=ts // chunk)
    return pl.pallas_call(
        kern,
        grid=(batch, nblk),
        in_specs=[pl.BlockSpec((ts, RET_WIDTH), tok)] * 4 + [
            pl.BlockSpec((RET_HEADS // 2, 2 * chunk, chunk), c3),
            pl.BlockSpec((chunk, RET_WIDTH), c2),
            pl.BlockSpec((chunk, RET_WIDTH), c2),
            pl.BlockSpec((1, RET_WIDTH), c2),
            pl.BlockSpec((1, RET_WIDTH), c2),
        ],
        out_specs=pl.BlockSpec((ts, RET_WIDTH), tok),
        out_shape=jax.ShapeDtypeStruct((T, RET_WIDTH), BF16),
        scratch_shapes=[pltpu.VMEM((RET_WIDTH // LANES, LANES, LANES), F32)],
        compiler_params=pltpu.CompilerParams(
            dimension_semantics=("parallel", "arbitrary"), vmem_limit_bytes=VMEM_LIMIT),
        name="retention",
    )(rq, rk, rv, rg, dmask, zeta, xi, gchunk, gs_ret)


def _diff_attn_kernel(q_ref, k_ref, vt_ref, lam_ref, subln_ref, gs_ref, *refs,
                      n_cast, heads, tq, tk, rb, lam_init):
    w_refs = refs[:n_cast]
    o_ref = refs[n_cast]
    wb_refs = refs[n_cast + 1:2 * n_cast + 1]
    scratch = refs[2 * n_cast + 1:]
    for hh in range(heads):
        lanes = pl.ds(hh * LANES, LANES)
        _diff_attn_head(
            q_ref.at[:, lanes], k_ref.at[:, lanes],
            vt_ref.at[pl.ds(hh * VT_ROWS, VT_ROWS), :], lam_ref, subln_ref,
            gs_ref.at[:, lanes], o_ref.at[:, lanes],
            w_refs if hh == 0 else (), wb_refs if hh == 0 else (),
            *[ref.at[hh] for ref in scratch], tq=tq, tk=tk, rb=rb, lam_init=lam_init)


def _diff_attn_head(q_ref, k_ref, vt_ref, lam_ref, subln_ref, gs_ref, o_ref, w_refs, wb_refs,
                    qs_ref, s_ref, mc_ref, m_ref, acc_ref, *, tq, tk, rb, lam_init):
    i = pl.program_id(2)
    q = q_ref[...]
    low = _low_half_mask(q.shape)
    zero = jnp.zeros_like(q)
    qs_ref[:tq, :] = jnp.where(low, q, zero)
    qs_ref[tq:, :] = jnp.where(low, zero, q)
    nt = (((1,), (1,)), ((), ()))

    n_blocks = 2 * tq // rb
    n_diag = tq // tk
    n_full = i * n_diag

    def diag_kind(d, b):
        q_first = (b * rb) % tq
        if d * tk > q_first + rb - 1:
            return "skip"
        return "full" if (d + 1) * tk - 1 <= q_first else "mask"

    def scores(j, b, key_offset=None):
        start = pl.multiple_of(j * tk, tk)
        cols = slice(b * rb, (b + 1) * rb)
        s = lax.dot_general(k_ref[pl.ds(start, tk), :], qs_ref[cols, :], nt,
                            preferred_element_type=F32)
        if key_offset is not None:
            kpos = lax.broadcasted_iota(jnp.int32, s.shape, 0) + key_offset
            qpos = lax.broadcasted_iota(jnp.int32, s.shape, 1) + (b * rb) % tq
            s = jnp.where(kpos <= qpos, s, -jnp.inf)
        s_ref[:, cols] = s
        mc_ref[:, cols] = jnp.broadcast_to(jnp.max(s, axis=0, keepdims=True),
                                           (F32_SUBLANES, rb))

    def consume(j, b):
        start = pl.multiple_of(j * tk, tk)
        cols = slice(b * rb, (b + 1) * rb)
        m_prev = m_ref[:, cols]
        m_new = jnp.maximum(m_prev, mc_ref[:, cols])
        alpha = jnp.exp2(m_prev - m_new)
        p = jnp.exp2(s_ref[:, cols] - m_new[0:1, :])
        pv = jnp.dot(vt_ref[:, pl.ds(start, tk)], p.astype(BF16),
                     preferred_element_type=F32)
        acc_ref[:, cols] = alpha[0:1, :] * acc_ref[:, cols] + pv
        m_ref[:, cols] = m_new

    ahead = n_blocks // 2
    order = sorted(range(n_blocks), key=lambda b: -((b * rb) % tq))
    assert all(diag_kind(0, b) == "full" for b in order[:ahead])
    for b in order[:ahead]:
        scores(0, b)
    m_ref[...] = jnp.full_like(m_ref, -jnp.inf)
    acc_ref[...] = jnp.zeros_like(acc_ref)

    def full_chunks(j0, count):
        for u in range(count):
            j = j0 + u
            for pos, b in enumerate(order):
                if pos + ahead < n_blocks:
                    scores(j, order[pos + ahead])
                else:
                    scores(j + 1, order[pos + ahead - n_blocks])
                consume(j, b)

    def body(t, carry):
        full_chunks(t * (2 * n_diag), 2 * n_diag)
        return carry

    lax.fori_loop(0, i // 2, body, 0)

    @pl.when(i % 2 == 1)
    def _():
        full_chunks((i - 1) * n_diag, n_diag)

    lv = lam_ref[...]
    lam = (jnp.exp(jnp.sum(lv[0:1] * lv[1:2], axis=-1, keepdims=True))
           - jnp.exp(jnp.sum(lv[2:3] * lv[3:4], axis=-1, keepdims=True)) + lam_init)

    out_gain = subln_ref[...] * (1.0 - lam_init) * gs_ref[...]

    def finalize(d):
        a0 = acc_ref[:, d * tk:(d + 1) * tk]
        a1 = acc_ref[:, tq + d * tk:tq + (d + 1) * tk]
        o0 = a0[:DIFF_DV, :] * (1.0 / a0[DIFF_DV:DIFF_DV + 1, :])
        o1 = a1[:DIFF_DV, :] * (1.0 / a1[DIFF_DV:DIFF_DV + 1, :])
        o = (o0 - lam * o1).T
        o_ref[d * tk:(d + 1) * tk, :] = (o * _rms(o) * out_gain).astype(o_ref.dtype)

    for d in range(n_diag):
        for pos, b in enumerate(order):
            nd, npos = ((d, pos + ahead) if pos + ahead < n_blocks
                        else (d + 1, pos + ahead - n_blocks))
            if nd < n_diag and diag_kind(nd, order[npos]) != "skip":
                masked = diag_kind(nd, order[npos]) == "mask"
                scores(n_full + nd, order[npos], nd * tk if masked else None)
            if diag_kind(d, b) != "skip":
                consume(n_full + d, b)
        finalize(d)
        if d == 0:
            for w_ref, wb_ref in zip(w_refs, wb_refs):
                wb_ref[...] = w_ref[...].astype(wb_ref.dtype)


def _diff_attn(dq, dk, dvt, lam_p, subln, gs_diff, cast_weights, batch, seq, tq, tk, rb, heads,
               lam_init):
    T = dq.shape[0]
    nq = seq // tq
    n_groups = DIFF_HEADS // heads
    n_steps = batch * n_groups * nq
    assert seq % tq == 0 and tq % tk == 0 and tq % rb == 0 and DIFF_HEADS % heads == 0

    def cast_spec(w):
        rows = w.shape[0]
        n_blk = max(n for n in range(1, n_steps + 1)
                    if n_steps % n == 0 and rows % (n * BF16_SUBLANES) == 0)
        hold = n_steps // n_blk
        return pl.BlockSpec(
            (rows // n_blk, w.shape[1]),
            lambda b, g, i: (((b * n_groups + g) * nq + i) // hold, 0))

    w_specs = [cast_spec(w) for w in cast_weights]
    kern = functools.partial(_diff_attn_kernel, n_cast=len(cast_weights), heads=heads, tq=tq,
                             tk=tk, rb=rb, lam_init=lam_init)
    per_head = lambda shape, dtype: pltpu.VMEM((heads,) + shape, dtype)
    outs = pl.pallas_call(
        kern,
        grid=(batch, n_groups, nq),
        in_specs=[
            pl.BlockSpec((tq, heads * LANES), lambda b, g, i: (b * nq + i, g)),
            pl.BlockSpec((seq, heads * LANES), lambda b, g, i: (b, g)),
            pl.BlockSpec((heads * VT_ROWS, seq), lambda b, g, i: (g, b)),
            pl.BlockSpec((4, DIFF_DQK), lambda b, g, i: (0, 0)),
            pl.BlockSpec((1, LANES), lambda b, g, i: (0, 0)),
            pl.BlockSpec((1, heads * LANES), lambda b, g, i: (0, g)),
        ] + w_specs,
        out_specs=[pl.BlockSpec((tq, heads * LANES), lambda b, g, i: (b * nq + i, g))]
        + w_specs,
        out_shape=[jax.ShapeDtypeStruct((T, DIFF_WIDTH), BF16)]
        + [jax.ShapeDtypeStruct(w.shape, BF16) for w in cast_weights],
        scratch_shapes=[per_head((2 * tq, LANES), BF16), per_head((tk, 2 * tq), F32),
                        per_head((F32_SUBLANES, 2 * tq), F32),
                        per_head((F32_SUBLANES, 2 * tq), F32),
                        per_head((VT_ROWS, 2 * tq), F32)],
        compiler_params=pltpu.CompilerParams(
            dimension_semantics=("parallel", "parallel", "arbitrary"),
            vmem_limit_bytes=VMEM_LIMIT),
        name="diff_attn",
    )(dq, dk, dvt, lam_p, subln, gs_diff, *cast_weights)
    return outs[0], outs[1:]


def _mem_kv_kernel(mem_ref, gain_ref, w_ref, kg_ref, k_ref, v_ref):
    m = mem_ref[...]
    h = (m * _rms(m) * gain_ref[...]).astype(BF16)
    D = m.shape[1]
    dh = D // XATTN_HEADS
    k = jnp.dot(h, w_ref[:, :D], preferred_element_type=F32)
    v_ref[...] = jnp.dot(h, w_ref[:, D:], preferred_element_type=F32).astype(v_ref.dtype)
    for hh in range(XATTN_HEADS):
        kh = k[:, hh * dh:(hh + 1) * dh]
        k_ref[:, hh * dh:(hh + 1) * dh] = (kh * _rms(kh) * kg_ref[...]).astype(k_ref.dtype)


def _mem_kv(mem2, gain, xkv, kg, batch, mem_len):
    BM, D = mem2.shape
    const = lambda b: (0, 0)
    out = jax.ShapeDtypeStruct((BM, D), BF16)
    return pl.pallas_call(
        _mem_kv_kernel,
        grid=(batch,),
        in_specs=[
            pl.BlockSpec((mem_len, D), lambda b: (b, 0)),
            pl.BlockSpec((1, D), const),
            pl.BlockSpec((D, 2 * D), const),
            pl.BlockSpec((1, D // XATTN_HEADS), const),
        ],
        out_specs=[pl.BlockSpec((mem_len, D), lambda b: (b, 0))] * 2,
        out_shape=[out, out],
        compiler_params=pltpu.CompilerParams(
            dimension_semantics=("parallel",), vmem_limit_bytes=VMEM_LIMIT),
        name="mem_kv",
    )(mem2, gain, xkv, kg)


def _mix_xattn_kernel(x_ref, oret_ref, odiff_ref, wout_ref, nx_ref, xq_ref, qg_ref,
                      kmem_ref, vmem_ref, xo_ref, out_ref):
    D = x_ref.shape[1]
    dh = D // XATTN_HEADS
    x1 = (x_ref[...]
          + jnp.dot(oret_ref[...], wout_ref[:RET_WIDTH, :], preferred_element_type=F32)
          + jnp.dot(odiff_ref[...], wout_ref[RET_WIDTH:, :], preferred_element_type=F32))
    h = (x1 * _rms(x1) * nx_ref[...]).astype(BF16)
    q = jnp.dot(h, xq_ref[...], preferred_element_type=F32)
    nt = (((1,), (1,)), ((), ()))
    outs = []
    for hh in range(XATTN_HEADS):
        sl = slice(hh * dh, (hh + 1) * dh)
        qh = q[:, sl]
        qn = (qh * _rms(qh) * qg_ref[...] * (dh ** -0.5)).astype(BF16)
        s = lax.dot_general(qn, kmem_ref[:, sl], nt, preferred_element_type=F32)
        e = jnp.exp(s - jnp.max(s, axis=-1, keepdims=True))
        p = e * (1.0 / jnp.sum(e, axis=-1, keepdims=True))
        outs.append(jnp.dot(p.astype(BF16), vmem_ref[:, sl],
                            preferred_element_type=F32).astype(BF16))
    o = jnp.concatenate(outs, axis=1)
    out_ref[...] = x1 + jnp.dot(o, xo_ref[...], preferred_element_type=F32)


def _mix_xattn(x2, o_ret, o_diff, w_out, nx, xq, qg, kmem, vmem, xo, seq, mem_len, tm):
    T, D = x2.shape
    nblk = seq // tm
    tok = lambda i: (i, 0)
    const = lambda i: (0, 0)
    memb = lambda i: (i // nblk, 0)
    return pl.pallas_call(
        _mix_xattn_kernel,
        grid=(T // tm,),
        in_specs=[
            pl.BlockSpec((tm, D), tok),
            pl.BlockSpec((tm, RET_WIDTH), tok),
            pl.BlockSpec((tm, DIFF_WIDTH), tok),
            pl.BlockSpec((D, D), const),
            pl.BlockSpec((1, D), const),
            pl.BlockSpec((D, D), const),
            pl.BlockSpec((1, D // XATTN_HEADS), const),
            pl.BlockSpec((mem_len, D), memb),
            pl.BlockSpec((mem_len, D), memb),
            pl.BlockSpec((D, D), const),
        ],
        out_specs=pl.BlockSpec((tm, D), tok),
        out_shape=jax.ShapeDtypeStruct((T, D), F32),
        compiler_params=pltpu.CompilerParams(
            dimension_semantics=("parallel",), vmem_limit_bytes=VMEM_LIMIT),
        name="mix_xattn",
    )(x2, o_ret, o_diff, w_out, nx, xq, qg, kmem, vmem, xo)


def _ffn_kernel(x_ref, gain_ref, wg_ref, wu_ref, wd_ref, out_ref, *, hidden_chunk):
    x = x_ref[...]
    h = (x * _rms(x) * gain_ref[...]).astype(BF16)
    hidden = wg_ref.shape[1]
    acc = x
    for c in range(hidden // hidden_chunk):
        sl = slice(c * hidden_chunk, (c + 1) * hidden_chunk)
        g = jnp.dot(h, wg_ref[:, sl], preferred_element_type=F32)
        u = jnp.dot(h, wu_ref[:, sl], preferred_element_type=F32)
        a = (g * jax.nn.sigmoid(g) * u).astype(BF16)
        acc = acc + jnp.dot(a, wd_ref[sl, :], preferred_element_type=F32)
    out_ref[...] = acc


def _ffn(x2, gain, wg, wu, wd, tm, hidden_chunk):
    T, D = x2.shape
    hidden = wg.shape[1]
    tok = lambda i: (i, 0)
    const = lambda i: (0, 0)
    kern = functools.partial(_ffn_kernel, hidden_chunk=hidden_chunk)
    return pl.pallas_call(
        kern,
        grid=(T // tm,),
        in_specs=[
            pl.BlockSpec((tm, D), tok),
            pl.BlockSpec((1, D), const),
            pl.BlockSpec((D, hidden), const, pipeline_mode=pl.Buffered(1)),
            pl.BlockSpec((D, hidden), const, pipeline_mode=pl.Buffered(1)),
            pl.BlockSpec((hidden, D), const, pipeline_mode=pl.Buffered(1)),
        ],
        out_specs=pl.BlockSpec((tm, D), tok),
        out_shape=jax.ShapeDtypeStruct((T, D), F32),
        compiler_params=pltpu.CompilerParams(
            dimension_semantics=("parallel",), vmem_limit_bytes=VMEM_LIMIT),
        name="ffn",
    )(x2, gain, wg, wu, wd)


def _rotary_tables(seq):
    half = RET_DK // 2
    inv = 1.0 / (RET_THETA_BASE ** np.linspace(0.0, 1.0, half))
    ang = np.arange(seq, dtype=np.float64)[:, None] * inv[None, :]
    cos, sin = np.cos(ang), np.sin(ang)
    cos_t = np.tile(cos, (1, LANES // half))
    sin_t = np.tile(np.concatenate([-sin, sin], axis=1), (1, LANES // RET_DK))
    return cos_t.astype(np.float32), sin_t.astype(np.float32)


def _decay_tables(chunk):
    H = RET_HEADS
    log_g = np.log1p(-(2.0 ** (-5.0 - np.arange(H, dtype=np.float64))))
    idx = np.arange(chunk, dtype=np.float64)
    rel = idx[:, None] - idx[None, :]
    dmask = np.where(rel >= 0, np.exp(log_g[:, None, None] * np.maximum(rel, 0.0)), 0.0)
    zeta = np.exp(log_g[:, None] * (chunk - 1.0 - idx))
    xi = np.exp(log_g[:, None] * (idx + 1.0))
    g_chunk = np.exp(log_g * chunk)
    widen = lambda t: np.repeat(t.T, RET_DK, axis=1)
    f32 = lambda t: t.astype(np.float32)
    dmask = dmask.reshape(H // 2, 2 * chunk, chunk)
    return (f32(dmask), f32(widen(zeta)), f32(widen(xi)),
            f32(np.repeat(g_chunk, RET_DK)[None, :]))


def kernel(x, mem, norm_mix, w_in, diff_q_gain, diff_k_gain, diff_lambda, diff_subln,
           group_scale, w_out, norm_x, norm_mem, xq, xkv, xq_gain, xk_gain, xo,
           norm_ffn, w_gate, w_up, w_down):
    B, S, D = x.shape
    M = mem.shape[1]
    depth = norm_mix.shape[0]
    x2 = x.reshape(B * S, D)
    mem2 = mem.reshape(B * M, D)

    ret_chunk = 256
    cos_t, sin_t = _rotary_tables(S)
    dmask, zeta, xi, gchunk = _decay_tables(ret_chunk)
    row = lambda a: a.reshape(1, -1)
    twice = lambda a: jnp.tile(a.reshape(1, -1), (1, 2))

    for l in range(depth):
        lam_init = 0.8 - 0.6 * math.exp(-0.3 * l)
        rq, rk, rv, rg, dq, dk, dv = _in_proj(
            x2, row(norm_mix[l]), w_in[l], cos_t, sin_t,
            twice(diff_q_gain[l]), twice(diff_k_gain[l]), S, tm=512)
        o_ret = _retention(rq, rk, rv, rg, dmask, zeta, xi, gchunk,
                           row(group_scale[l, :RET_WIDTH]), B, S, ts=1024, chunk=ret_chunk)
        later_weights = [w_out[l], xq[l], xkv[l], xo[l], w_gate[l], w_up[l], w_down[l]]
        o_diff, (w_out_b, xq_b, xkv_b, xo_b, w_gate_b, w_up_b, w_down_b) = _diff_attn(
            dq, dk, dv, diff_lambda[l], row(diff_subln[l]), row(group_scale[l, RET_WIDTH:]),
            later_weights, B, S, tq=1024, tk=512, rb=512, heads=2, lam_init=lam_init)
        kmem, vmem = _mem_kv(mem2, row(norm_mem[l]), xkv_b, row(xk_gain[l]), B, M)
        x2 = _mix_xattn(x2, o_ret, o_diff, w_out_b, row(norm_x[l]), xq_b, row(xq_gain[l]),
                        kmem, vmem, xo_b, S, M, tm=512)
        x2 = _ffn(x2, row(norm_ffn[l]), w_gate_b, w_up_b, w_down_b, tm=512, hidden_chunk=256)
    return x2.reshape(B, S, D)
```

```python
import functools
import math

import jax
import jax.numpy as jnp
import numpy as np
from jax import lax
from jax.experimental import pallas as pl
from jax.experimental.pallas import tpu as pltpu

F32 = jnp.float32
BF16 = jnp.bfloat16

EPS = 1e-6
LANES = 128
VMEM_LIMIT = 56 * 1024 * 1024

RET_HEADS = 8
RET_DK = 64
RET_DV = 64
RET_WIDTH = 512
RET_THETA_BASE = 10000.0
DIFF_HEADS = 4
DIFF_DQK = 64
DIFF_DV = 128
DIFF_WIDTH = 512
XATTN_HEADS = 4
SEG = 512
N_SEG = 7
LOG2E = math.log2(math.e)
F32_SUBLANES = 8
BF16_SUBLANES = 16
VT_ROWS = DIFF_DV + BF16_SUBLANES


def _rms(x, axis=-1):
    return lax.rsqrt(jnp.mean(x * x, axis=axis, keepdims=True) + EPS)


def _low_half_mask(shape):
    lane = lax.broadcasted_iota(jnp.int32, shape, len(shape) - 1)
    return (lane % LANES) < (LANES // 2)


def _in_proj_kernel(x_ref, gain_ref, w_ref, cos_ref, sin_ref, qg_ref, kg_ref,
                    rq_ref, rk_ref, rv_ref, rg_ref, dq_ref, dk_ref, dvt_ref, wb_ref):
    @pl.when(pl.program_id(0) == 0)
    def _():
        for seg in range(N_SEG):
            cols = slice(seg * SEG, (seg + 1) * SEG)
            wb_ref[:, cols] = w_ref[:, cols].astype(BF16)

    x = x_ref[...]
    h = (x * _rms(x) * gain_ref[...]).astype(BF16)

    def proj(seg):
        return jnp.dot(h, wb_ref[:, seg * SEG:(seg + 1) * SEG], preferred_element_type=F32)

    cos = cos_ref[...]
    sin = sin_ref[...]
    low = _low_half_mask(cos.shape)
    half = RET_DK // 2
    first_half = (lax.broadcasted_iota(jnp.int32, cos.shape, 1) % RET_DK) < half

    def rotary_store(y, out_ref, scale):
        for j in range(SEG // LANES):
            ys = y[:, j * LANES:(j + 1) * LANES]
            partner = jnp.where(first_half, pltpu.roll(ys, LANES - half, 1),
                                pltpu.roll(ys, half, 1))
            r = ys * cos + partner * sin
            if scale != 1.0:
                r = r * scale
            out_ref[:, j * LANES:(j + 1) * LANES] = r.astype(out_ref.dtype)

    def qknorm_store(y, g_ref, out_ref, scale):
        g = g_ref[...]
        for j in range(SEG // LANES):
            ys = y[:, j * LANES:(j + 1) * LANES]
            sq = ys * ys
            s_lo = jnp.sum(jnp.where(low, sq, 0.0), axis=-1, keepdims=True)
            s_hi = jnp.sum(jnp.where(low, 0.0, sq), axis=-1, keepdims=True)
            inv = jnp.where(low, lax.rsqrt(s_lo * (1.0 / DIFF_DQK) + EPS),
                            lax.rsqrt(s_hi * (1.0 / DIFF_DQK) + EPS))
            r = ys * inv * g
            if scale != 1.0:
                r = r * scale
            out_ref[:, j * LANES:(j + 1) * LANES] = r.astype(out_ref.dtype)

    rotary_store(proj(0), rq_ref, 1.0)
    rotary_store(proj(1), rk_ref, RET_DK ** -0.5)
    rv_ref[...] = proj(2).astype(BF16)
    g = proj(3)
    rg_ref[...] = (g * jax.nn.sigmoid(g)).astype(BF16)
    qknorm_store(proj(4), qg_ref, dq_ref, DIFF_DQK ** -0.5 * LOG2E)
    qknorm_store(proj(5), kg_ref, dk_ref, 1.0)
    dv = proj(6)
    ones = jnp.ones((VT_ROWS - DIFF_DV, dv.shape[0]), BF16)
    for hh in range(DIFF_HEADS):
        dvt_ref[hh * VT_ROWS:hh * VT_ROWS + DIFF_DV, :] = (
            dv[:, hh * DIFF_DV:(hh + 1) * DIFF_DV].T.astype(BF16))
        dvt_ref[hh * VT_ROWS + DIFF_DV:(hh + 1) * VT_ROWS, :] = ones


def _in_proj(x2, gain, w_in, cos_t, sin_t, qg, kg, seq, tm):
    T, D = x2.shape
    n_pos_blocks = seq // tm
    tok = lambda i: (i, 0)
    const = lambda i: (0, 0)
    pos = lambda i: (i % n_pos_blocks, 0)
    out = jax.ShapeDtypeStruct((T, SEG), BF16)
    return pl.pallas_call(
        _in_proj_kernel,
        grid=(T // tm,),
        in_specs=[
            pl.BlockSpec((tm, D), tok),
            pl.BlockSpec((1, D), const),
            pl.BlockSpec((D, N_SEG * SEG), const, pipeline_mode=pl.Buffered(1)),
            pl.BlockSpec((tm, LANES), pos),
            pl.BlockSpec((tm, LANES), pos),
            pl.BlockSpec((1, LANES), const),
            pl.BlockSpec((1, LANES), const),
        ],
        out_specs=[pl.BlockSpec((tm, SEG), tok)] * (N_SEG - 1)
        + [pl.BlockSpec((DIFF_HEADS * VT_ROWS, tm), lambda i: (0, i))],
        out_shape=[out] * (N_SEG - 1)
        + [jax.ShapeDtypeStruct((DIFF_HEADS * VT_ROWS, T), BF16)],
        scratch_shapes=[pltpu.VMEM((D, N_SEG * SEG), BF16)],
        compiler_params=pltpu.CompilerParams(
            dimension_semantics=("arbitrary",), vmem_limit_bytes=VMEM_LIMIT),
        name="in_proj",
    )(x2, gain, w_in, cos_t, sin_t, qg, kg)


def _retention_kernel(q_ref, k_ref, v_ref, g_ref, dmask_ref, zeta_ref, xi_ref, gchunk_ref,
                      gs_ref, o_ref, state_ref, *, chunk, n_chunks):
    @pl.when(pl.program_id(1) == 0)
    def _():
        state_ref[...] = jnp.zeros_like(state_ref)

    n_slabs = RET_WIDTH // LANES
    low = _low_half_mask((chunk, LANES))
    row = lax.broadcasted_iota(jnp.int32, (LANES, LANES), 0)
    col = lax.broadcasted_iota(jnp.int32, (LANES, LANES), 1)
    blockdiag = (row < RET_DK) == (col < RET_DK)
    nt = (((1,), (1,)), ((), ()))
    tn = (((0,), (0,)), ((), ()))

    for c in range(n_chunks):
        rows = slice(c * chunk, (c + 1) * chunk)
        for s in range(n_slabs):
            sl = slice(s * LANES, (s + 1) * LANES)
            zeta = zeta_ref[:, sl]
            xi = xi_ref[:, sl]
            gch = gchunk_ref[:, sl]
            q = q_ref[rows, sl]
            k = k_ref[rows, sl]
            v = v_ref[rows, sl]
            zero = jnp.zeros_like(q)
            q2 = jnp.concatenate([jnp.where(low, q, zero), jnp.where(low, zero, q)], axis=0)
            s2 = lax.dot_general(q2, k, nt, preferred_element_type=F32) * dmask_ref[s]
            o2 = jnp.dot(s2.astype(BF16), v, preferred_element_type=F32)
            r_prev = state_ref[s]
            o_inter = jnp.dot(q, r_prev.astype(BF16), preferred_element_type=F32) * xi
            o = jnp.where(low, o2[:chunk], o2[chunk:]) + o_inter
            kz = (k.astype(F32) * zeta).astype(BF16)
            kv = lax.dot_general(kz, v, tn, preferred_element_type=F32)
            state_ref[s] = r_prev * gch + jnp.where(blockdiag, kv, 0.0)
            sq = o * o
            m_lo = jnp.sum(jnp.where(low, sq, 0.0), axis=-1, keepdims=True)
            m_hi = jnp.sum(jnp.where(low, 0.0, sq), axis=-1, keepdims=True)
            inv = jnp.where(low, lax.rsqrt(m_lo * (1.0 / RET_DV) + EPS),
                            lax.rsqrt(m_hi * (1.0 / RET_DV) + EPS))
            out = o * inv * g_ref[rows, sl].astype(F32) * gs_ref[:, sl]
            o_ref[rows, sl] = out.astype(o_ref.dtype)


def _retention(rq, rk, rv, rg, dmask, zeta, xi, gchunk, gs_ret, batch, seq, ts, chunk):
    T = rq.shape[0]
    nblk = seq // ts
    tok = lambda b, i: (b * nblk + i, 0)
    c2 = lambda b, i: (0, 0)
    c3 = lambda b, i: (0, 0, 0)
    kern = functools.partial(_retention_kernel, chunk=chunk, n_chunks=ts // chunk)
    return pl.pallas_call(
        kern,
        grid=(batch, nblk),
        in_specs=[pl.BlockSpec((ts, RET_WIDTH), tok)] * 4 + [
            pl.BlockSpec((RET_HEADS // 2, 2 * chunk, chunk), c3),
            pl.BlockSpec((chunk, RET_WIDTH), c2),
            pl.BlockSpec((chunk, RET_WIDTH), c2),
            pl.BlockSpec((1, RET_WIDTH), c2),
            pl.BlockSpec((1, RET_WIDTH), c2),
        ],
        out_specs=pl.BlockSpec((ts, RET_WIDTH), tok),
        out_shape=jax.ShapeDtypeStruct((T, RET_WIDTH), BF16),
        scratch_shapes=[pltpu.VMEM((RET_WIDTH // LANES, LANES, LANES), F32)],
        compiler_params=pltpu.CompilerParams(
            dimension_semantics=("parallel", "arbitrary"), vmem_limit_bytes=VMEM_LIMIT),
        name="retention",
    )(rq, rk, rv, rg, dmask, zeta, xi, gchunk, gs_ret)


def _diff_attn_kernel(q_ref, k_ref, vt_ref, lam_ref, subln_ref, gs_ref, *refs,
                      n_cast, heads, tq, tk, rb, lam_init):
    w_refs = refs[:n_cast]
    o_ref = refs[n_cast]
    wb_refs = refs[n_cast + 1:2 * n_cast + 1]
    qs_all, s_all, mc_all, m_all, acc_all = refs[2 * n_cast + 1:]
    i = pl.program_id(2)
    nt = (((1,), (1,)), ((), ()))
    n_blocks = 2 * tq // rb
    n_diag = tq // tk
    n_full = i * n_diag
    head_ids = range(heads)

    def lanes(hh):
        return pl.ds(hh * LANES, LANES)

    def diag_kind(d, b):
        q_first = (b * rb) % tq
        if d * tk > q_first + rb - 1:
            return "skip"
        return "full" if (d + 1) * tk - 1 <= q_first else "mask"

    def scores(hh, j, b, key_offset=None):
        start = pl.multiple_of(j * tk, tk)
        cols = slice(b * rb, (b + 1) * rb)
        s = lax.dot_general(k_ref[pl.ds(start, tk), lanes(hh)], qs_all[hh, cols, :], nt,
                            preferred_element_type=F32)
        if key_offset is not None:
            kpos = lax.broadcasted_iota(jnp.int32, s.shape, 0) + key_offset
            qpos = lax.broadcasted_iota(jnp.int32, s.shape, 1) + (b * rb) % tq
            s = jnp.where(kpos <= qpos, s, -jnp.inf)
        s_all[hh, :, cols] = s
        mc_all[hh, :, cols] = jnp.broadcast_to(jnp.max(s, axis=0, keepdims=True),
                                               (F32_SUBLANES, rb))

    def consume(hh, j, b):
        start = pl.multiple_of(j * tk, tk)
        cols = slice(b * rb, (b + 1) * rb)
        m_prev = m_all[hh, :, cols]
        m_new = jnp.maximum(m_prev, mc_all[hh, :, cols])
        alpha = jnp.exp2(m_prev - m_new)
        p = jnp.exp2(s_all[hh, :, cols] - m_new[0:1, :])
        pv = jnp.dot(vt_ref[hh * VT_ROWS:(hh + 1) * VT_ROWS, pl.ds(start, tk)], p.astype(BF16),
                     preferred_element_type=F32)
        acc_all[hh, :, cols] = alpha[0:1, :] * acc_all[hh, :, cols] + pv
        m_all[hh, :, cols] = m_new

    for hh in head_ids:
        q = q_ref[:, lanes(hh)]
        low = _low_half_mask(q.shape)
        zero = jnp.zeros_like(q)
        qs_all[hh, :tq, :] = jnp.where(low, q, zero)
        qs_all[hh, tq:, :] = jnp.where(low, zero, q)

    ahead = n_blocks // 2
    order = sorted(range(n_blocks), key=lambda b: -((b * rb) % tq))
    assert all(diag_kind(0, b) == "full" for b in order[:ahead])
    for b in order[:ahead]:
        for hh in head_ids:
            scores(hh, 0, b)
    m_all[...] = jnp.full_like(m_all, -jnp.inf)
    acc_all[...] = jnp.zeros_like(acc_all)

    def full_chunks(j0, count):
        for u in range(count):
            j = j0 + u
            for pos, b in enumerate(order):
                for hh in head_ids:
                    if pos + ahead < n_blocks:
                        scores(hh, j, order[pos + ahead])
                    else:
                        scores(hh, j + 1, order[pos + ahead - n_blocks])
                    consume(hh, j, b)

    def body(t, carry):
        full_chunks(t * (2 * n_diag), 2 * n_diag)
        return carry

    lax.fori_loop(0, i // 2, body, 0)

    @pl.when(i % 2 == 1)
    def _():
        full_chunks((i - 1) * n_diag, n_diag)

    lv = lam_ref[...]
    lam = (jnp.exp(jnp.sum(lv[0:1] * lv[1:2], axis=-1, keepdims=True))
           - jnp.exp(jnp.sum(lv[2:3] * lv[3:4], axis=-1, keepdims=True)) + lam_init)

    def finalize(hh, d):
        a0 = acc_all[hh, :, d * tk:(d + 1) * tk]
        a1 = acc_all[hh, :, tq + d * tk:tq + (d + 1) * tk]
        o0 = a0[:DIFF_DV, :] * (1.0 / a0[DIFF_DV:DIFF_DV + 1, :])
        o1 = a1[:DIFF_DV, :] * (1.0 / a1[DIFF_DV:DIFF_DV + 1, :])
        o = (o0 - lam * o1).T
        out_gain = subln_ref[...] * (1.0 - lam_init) * gs_ref[:, lanes(hh)]
        o_ref[d * tk:(d + 1) * tk, lanes(hh)] = (o * _rms(o) * out_gain).astype(o_ref.dtype)

    for d in range(n_diag):
        for pos, b in enumerate(order):
            nd, npos = ((d, pos + ahead) if pos + ahead < n_blocks
                        else (d + 1, pos + ahead - n_blocks))
            for hh in head_ids:
                if nd < n_diag and diag_kind(nd, order[npos]) != "skip":
                    masked = diag_kind(nd, order[npos]) == "mask"
                    scores(hh, n_full + nd, order[npos], nd * tk if masked else None)
                if diag_kind(d, b) != "skip":
                    consume(hh, n_full + d, b)
        for hh in head_ids:
            finalize(hh, d)
        if d == 0:
            for w_ref, wb_ref in zip(w_refs, wb_refs):
                wb_ref[...] = w_ref[...].astype(wb_ref.dtype)


def _diff_attn(dq, dk, dvt, lam_p, subln, gs_diff, cast_weights, batch, seq, tq, tk, rb, heads,
               lam_init):
    T = dq.shape[0]
    nq = seq // tq
    n_groups = DIFF_HEADS // heads
    n_steps = batch * n_groups * nq
    assert seq % tq == 0 and tq % tk == 0 and tq % rb == 0 and DIFF_HEADS % heads == 0

    def cast_spec(w):
        rows = w.shape[0]
        n_blk = max(n for n in range(1, n_steps + 1)
                    if n_steps % n == 0 and rows % (n * BF16_SUBLANES) == 0)
        hold = n_steps // n_blk
        return pl.BlockSpec(
            (rows // n_blk, w.shape[1]),
            lambda b, g, i: (((b * n_groups + g) * nq + i) // hold, 0))

    w_specs = [cast_spec(w) for w in cast_weights]
    kern = functools.partial(_diff_attn_kernel, n_cast=len(cast_weights), heads=heads, tq=tq,
                             tk=tk, rb=rb, lam_init=lam_init)
    per_head = lambda shape, dtype: pltpu.VMEM((heads,) + shape, dtype)
    outs = pl.pallas_call(
        kern,
        grid=(batch, n_groups, nq),
        in_specs=[
            pl.BlockSpec((tq, heads * LANES), lambda b, g, i: (b * nq + i, g)),
            pl.BlockSpec((seq, heads * LANES), lambda b, g, i: (b, g)),
            pl.BlockSpec((heads * VT_ROWS, seq), lambda b, g, i: (g, b)),
            pl.BlockSpec((4, DIFF_DQK), lambda b, g, i: (0, 0)),
            pl.BlockSpec((1, LANES), lambda b, g, i: (0, 0)),
            pl.BlockSpec((1, heads * LANES), lambda b, g, i: (0, g)),
        ] + w_specs,
        out_specs=[pl.BlockSpec((tq, heads * LANES), lambda b, g, i: (b * nq + i, g))]
        + w_specs,
        out_shape=[jax.ShapeDtypeStruct((T, DIFF_WIDTH), BF16)]
        + [jax.ShapeDtypeStruct(w.shape, BF16) for w in cast_weights],
        scratch_shapes=[per_head((2 * tq, LANES), BF16), per_head((tk, 2 * tq), F32),
                        per_head((F32_SUBLANES, 2 * tq), F32),
                        per_head((F32_SUBLANES, 2 * tq), F32),
                        per_head((VT_ROWS, 2 * tq), F32)],
        compiler_params=pltpu.CompilerParams(
            dimension_semantics=("parallel", "parallel", "arbitrary"),
            vmem_limit_bytes=VMEM_LIMIT),
        name="diff_attn",
    )(dq, dk, dvt, lam_p, subln, gs_diff, *cast_weights)
    return outs[0], outs[1:]


def _mix_xattn_kernel(x_ref, oret_ref, odiff_ref, wout_ref, nx_ref, xq_ref, qg_ref,
                      mem_ref, nm_ref, xkv_ref, kg_ref, xo_ref, out_ref, kmem_ref, vmem_ref,
                      *, tiles_per_batch):
    D = x_ref.shape[1]
    dh = D // XATTN_HEADS

    @pl.when(pl.program_id(0) % tiles_per_batch == 0)
    def _():
        m = mem_ref[...]
        hm = (m * _rms(m) * nm_ref[...]).astype(BF16)
        k = jnp.dot(hm, xkv_ref[:, :D], preferred_element_type=F32)
        vmem_ref[...] = jnp.dot(hm, xkv_ref[:, D:], preferred_element_type=F32).astype(BF16)
        for hh in range(XATTN_HEADS):
            kh = k[:, hh * dh:(hh + 1) * dh]
            kmem_ref[:, hh * dh:(hh + 1) * dh] = (kh * _rms(kh) * kg_ref[...]).astype(BF16)

    x1 = (x_ref[...]
          + jnp.dot(oret_ref[...], wout_ref[:RET_WIDTH, :], preferred_element_type=F32)
          + jnp.dot(odiff_ref[...], wout_ref[RET_WIDTH:, :], preferred_element_type=F32))
    h = (x1 * _rms(x1) * nx_ref[...]).astype(BF16)
    q = jnp.dot(h, xq_ref[...], preferred_element_type=F32)
    nt = (((1,), (1,)), ((), ()))
    outs = []
    for hh in range(XATTN_HEADS):
        sl = slice(hh * dh, (hh + 1) * dh)
        qh = q[:, sl]
        qn = (qh * _rms(qh) * qg_ref[...] * (dh ** -0.5)).astype(BF16)
        s = lax.dot_general(qn, kmem_ref[:, sl], nt, preferred_element_type=F32)
        e = jnp.exp(s - jnp.max(s, axis=-1, keepdims=True))
        p = e * (1.0 / jnp.sum(e, axis=-1, keepdims=True))
        outs.append(jnp.dot(p.astype(BF16), vmem_ref[:, sl],
                            preferred_element_type=F32).astype(BF16))
    o = jnp.concatenate(outs, axis=1)
    out_ref[...] = x1 + jnp.dot(o, xo_ref[...], preferred_element_type=F32)


def _mix_xattn(x2, o_ret, o_diff, w_out, nx, xq, qg, mem2, nm, xkv, kg, xo, seq, mem_len, tm):
    T, D = x2.shape
    nblk = seq // tm
    tok = lambda i: (i, 0)
    const = lambda i: (0, 0)
    memb = lambda i: (i // nblk, 0)
    return pl.pallas_call(
        functools.partial(_mix_xattn_kernel, tiles_per_batch=nblk),
        grid=(T // tm,),
        in_specs=[
            pl.BlockSpec((tm, D), tok),
            pl.BlockSpec((tm, RET_WIDTH), tok),
            pl.BlockSpec((tm, DIFF_WIDTH), tok),
            pl.BlockSpec((D, D), const),
            pl.BlockSpec((1, D), const),
            pl.BlockSpec((D, D), const),
            pl.BlockSpec((1, D // XATTN_HEADS), const),
            pl.BlockSpec((mem_len, D), memb),
            pl.BlockSpec((1, D), const),
            pl.BlockSpec((D, 2 * D), const),
            pl.BlockSpec((1, D // XATTN_HEADS), const),
            pl.BlockSpec((D, D), const),
        ],
        out_specs=pl.BlockSpec((tm, D), tok),
        out_shape=jax.ShapeDtypeStruct((T, D), F32),
        scratch_shapes=[pltpu.VMEM((mem_len, D), BF16), pltpu.VMEM((mem_len, D), BF16)],
        compiler_params=pltpu.CompilerParams(
            dimension_semantics=("arbitrary",), vmem_limit_bytes=VMEM_LIMIT),
        name="mix_xattn",
    )(x2, o_ret, o_diff, w_out, nx, xq, qg, mem2, nm, xkv, kg, xo)


def _ffn_kernel(x_ref, gain_ref, wg_ref, wu_ref, wd_ref, out_ref, *, hidden_chunk):
    x = x_ref[...]
    h = (x * _rms(x) * gain_ref[...]).astype(BF16)
    hidden = wg_ref.shape[1]
    acc = x
    for c in range(hidden // hidden_chunk):
        sl = slice(c * hidden_chunk, (c + 1) * hidden_chunk)
        g = jnp.dot(h, wg_ref[:, sl], preferred_element_type=F32)
        u = jnp.dot(h, wu_ref[:, sl], preferred_element_type=F32)
        a = (g * jax.nn.sigmoid(g) * u).astype(BF16)
        acc = acc + jnp.dot(a, wd_ref[sl, :], preferred_element_type=F32)
    out_ref[...] = acc


def _ffn(x2, gain, wg, wu, wd, tm, hidden_chunk):
    T, D = x2.shape
    hidden = wg.shape[1]
    tok = lambda i: (i, 0)
    const = lambda i: (0, 0)
    kern = functools.partial(_ffn_kernel, hidden_chunk=hidden_chunk)
    return pl.pallas_call(
        kern,
        grid=(T // tm,),
        in_specs=[
            pl.BlockSpec((tm, D), tok),
            pl.BlockSpec((1, D), const),
            pl.BlockSpec((D, hidden), const, pipeline_mode=pl.Buffered(1)),
            pl.BlockSpec((D, hidden), const, pipeline_mode=pl.Buffered(1)),
            pl.BlockSpec((hidden, D), const, pipeline_mode=pl.Buffered(1)),
        ],
        out_specs=pl.BlockSpec((tm, D), tok),
        out_shape=jax.ShapeDtypeStruct((T, D), F32),
        compiler_params=pltpu.CompilerParams(
            dimension_semantics=("parallel",), vmem_limit_bytes=VMEM_LIMIT),
        name="ffn",
    )(x2, gain, wg, wu, wd)


def _rotary_tables(seq):
    half = RET_DK // 2
    inv = 1.0 / (RET_THETA_BASE ** np.linspace(0.0, 1.0, half))
    ang = np.arange(seq, dtype=np.float64)[:, None] * inv[None, :]
    cos, sin = np.cos(ang), np.sin(ang)
    cos_t = np.tile(cos, (1, LANES // half))
    sin_t = np.tile(np.concatenate([-sin, sin], axis=1), (1, LANES // RET_DK))
    return cos_t.astype(np.float32), sin_t.astype(np.float32)


def _decay_tables(chunk):
    H = RET_HEADS
    log_g = np.log1p(-(2.0 ** (-5.0 - np.arange(H, dtype=np.float64))))
    idx = np.arange(chunk, dtype=np.float64)
    rel = idx[:, None] - idx[None, :]
    dmask = np.where(rel >= 0, np.exp(log_g[:, None, None] * np.maximum(rel, 0.0)), 0.0)
    zeta = np.exp(log_g[:, None] * (chunk - 1.0 - idx))
    xi = np.exp(log_g[:, None] * (idx + 1.0))
    g_chunk = np.exp(log_g * chunk)
    widen = lambda t: np.repeat(t.T, RET_DK, axis=1)
    f32 = lambda t: t.astype(np.float32)
    dmask = dmask.reshape(H // 2, 2 * chunk, chunk)
    return (f32(dmask), f32(widen(zeta)), f32(widen(xi)),
            f32(np.repeat(g_chunk, RET_DK)[None, :]))


def kernel(x, mem, norm_mix, w_in, diff_q_gain, diff_k_gain, diff_lambda, diff_subln,
           group_scale, w_out, norm_x, norm_mem, xq, xkv, xq_gain, xk_gain, xo,
           norm_ffn, w_gate, w_up, w_down):
    B, S, D = x.shape
    M = mem.shape[1]
    depth = norm_mix.shape[0]
    x2 = x.reshape(B * S, D)
    mem2 = mem.reshape(B * M, D)

    ret_chunk = 256
    cos_t, sin_t = _rotary_tables(S)
    dmask, zeta, xi, gchunk = _decay_tables(ret_chunk)
    row = lambda a: a.reshape(1, -1)
    twice = lambda a: jnp.tile(a.reshape(1, -1), (1, 2))

    for l in range(depth):
        lam_init = 0.8 - 0.6 * math.exp(-0.3 * l)
        rq, rk, rv, rg, dq, dk, dv = _in_proj(
            x2, row(norm_mix[l]), w_in[l], cos_t, sin_t,
            twice(diff_q_gain[l]), twice(diff_k_gain[l]), S, tm=1024)
        o_ret = _retention(rq, rk, rv, rg, dmask, zeta, xi, gchunk,
                           row(group_scale[l, :RET_WIDTH]), B, S, ts=1024, chunk=ret_chunk)
        later_weights = [w_out[l], xq[l], xkv[l], xo[l], w_gate[l], w_up[l], w_down[l]]
        o_diff, (w_out_b, xq_b, xkv_b, xo_b, w_gate_b, w_up_b, w_down_b) = _diff_attn(
            dq, dk, dv, diff_lambda[l], row(diff_subln[l]), row(group_scale[l, RET_WIDTH:]),
            later_weights, B, S, tq=1024, tk=512, rb=512, heads=2, lam_init=lam_init)
        x2 = _mix_xattn(x2, o_ret, o_diff, w_out_b, row(norm_x[l]), xq_b, row(xq_gain[l]),
                        mem2, row(norm_mem[l]), xkv_b, row(xk_gain[l]), xo_b, S, M, tm=1024)
        x2 = _ffn(x2, row(norm_ffn[l]), w_gate_b, w_up_b, w_down_b, tm=512, hidden_chunk=256)
    return x2.reshape(B, S, D)
```

```python
import functools
import math

import jax
import jax.numpy as jnp
import numpy as np
from jax import lax
from jax.experimental import pallas as pl
from jax.experimental.pallas import tpu as pltpu

F32 = jnp.float32
BF16 = jnp.bfloat16

EPS = 1e-6
LANES = 128
VMEM_LIMIT = 56 * 1024 * 1024

RET_HEADS = 8
RET_DK = 64
RET_DV = 64
RET_WIDTH = 512
RET_THETA_BASE = 10000.0
DIFF_HEADS = 4
DIFF_DQK = 64
DIFF_DV = 128
DIFF_WIDTH = 512
XATTN_HEADS = 4
SEG = 512
N_SEG = 7
LOG2E = math.log2(math.e)
F32_SUBLANES = 8
BF16_SUBLANES = 16
VT_ROWS = DIFF_DV + BF16_SUBLANES


def _rms(x, axis=-1):
    return lax.rsqrt(jnp.mean(x * x, axis=axis, keepdims=True) + EPS)


def _low_half_mask(shape):
    lane = lax.broadcasted_iota(jnp.int32, shape, len(shape) - 1)
    return (lane % LANES) < (LANES // 2)


def _in_proj_kernel(x_ref, gain_ref, w_ref, cos_ref, sin_ref, qg_ref, kg_ref,
                    rq_ref, rk_ref, rv_ref, rg_ref, dq_ref, dk_ref, dvt_ref, wb_ref):
    @pl.when(pl.program_id(0) == 0)
    def _():
        for seg in range(N_SEG):
            cols = slice(seg * SEG, (seg + 1) * SEG)
            wb_ref[:, cols] = w_ref[:, cols].astype(BF16)

    x = x_ref[...]
    h = (x * _rms(x) * gain_ref[...]).astype(BF16)

    def proj(seg):
        return jnp.dot(h, wb_ref[:, seg * SEG:(seg + 1) * SEG], preferred_element_type=F32)

    cos = cos_ref[...]
    sin = sin_ref[...]
    low = _low_half_mask(cos.shape)
    half = RET_DK // 2
    first_half = (lax.broadcasted_iota(jnp.int32, cos.shape, 1) % RET_DK) < half

    def rotary_store(y, out_ref, scale):
        for j in range(SEG // LANES):
            ys = y[:, j * LANES:(j + 1) * LANES]
            partner = jnp.where(first_half, pltpu.roll(ys, LANES - half, 1),
                                pltpu.roll(ys, half, 1))
            r = ys * cos + partner * sin
            if scale != 1.0:
                r = r * scale
            out_ref[:, j * LANES:(j + 1) * LANES] = r.astype(out_ref.dtype)

    def qknorm_store(y, g_ref, out_ref, scale):
        g = jnp.concatenate([g_ref[...]] * (LANES // DIFF_DQK), axis=1)
        for j in range(SEG // LANES):
            ys = y[:, j * LANES:(j + 1) * LANES]
            sq = ys * ys
            s_lo = jnp.sum(jnp.where(low, sq, 0.0), axis=-1, keepdims=True)
            s_hi = jnp.sum(jnp.where(low, 0.0, sq), axis=-1, keepdims=True)
            inv = jnp.where(low, lax.rsqrt(s_lo * (1.0 / DIFF_DQK) + EPS),
                            lax.rsqrt(s_hi * (1.0 / DIFF_DQK) + EPS))
            r = ys * inv * g
            if scale != 1.0:
                r = r * scale
            out_ref[:, j * LANES:(j + 1) * LANES] = r.astype(out_ref.dtype)

    rotary_store(proj(0), rq_ref, 1.0)
    rotary_store(proj(1), rk_ref, RET_DK ** -0.5)
    rv_ref[...] = proj(2).astype(BF16)
    g = proj(3)
    rg_ref[...] = (g * jax.nn.sigmoid(g)).astype(BF16)
    qknorm_store(proj(4), qg_ref, dq_ref, DIFF_DQK ** -0.5 * LOG2E)
    qknorm_store(proj(5), kg_ref, dk_ref, 1.0)
    dv = proj(6)
    ones = jnp.ones((VT_ROWS - DIFF_DV, dv.shape[0]), BF16)
    for hh in range(DIFF_HEADS):
        dvt_ref[hh * VT_ROWS:hh * VT_ROWS + DIFF_DV, :] = (
            dv[:, hh * DIFF_DV:(hh + 1) * DIFF_DV].T.astype(BF16))
        dvt_ref[hh * VT_ROWS + DIFF_DV:(hh + 1) * VT_ROWS, :] = ones


def _in_proj(x2, gain, w_in, cos_t, sin_t, qg, kg, seq, tm):
    T, D = x2.shape
    n_pos_blocks = seq // tm
    tok = lambda i: (i, 0)
    const = lambda i: (0, 0)
    pos = lambda i: (i % n_pos_blocks, 0)
    out = jax.ShapeDtypeStruct((T, SEG), BF16)
    return pl.pallas_call(
        _in_proj_kernel,
        grid=(T // tm,),
        in_specs=[
            pl.BlockSpec((tm, D), tok),
            pl.BlockSpec((1, D), const),
            pl.BlockSpec((D, N_SEG * SEG), const, pipeline_mode=pl.Buffered(1)),
            pl.BlockSpec((tm, LANES), pos),
            pl.BlockSpec((tm, LANES), pos),
            pl.BlockSpec((1, DIFF_DQK), const),
            pl.BlockSpec((1, DIFF_DQK), const),
        ],
        out_specs=[pl.BlockSpec((tm, SEG), tok)] * (N_SEG - 1)
        + [pl.BlockSpec((DIFF_HEADS * VT_ROWS, tm), lambda i: (0, i))],
        out_shape=[out] * (N_SEG - 1)
        + [jax.ShapeDtypeStruct((DIFF_HEADS * VT_ROWS, T), BF16)],
        scratch_shapes=[pltpu.VMEM((D, N_SEG * SEG), BF16)],
        compiler_params=pltpu.CompilerParams(
            dimension_semantics=("arbitrary",), vmem_limit_bytes=VMEM_LIMIT),
        name="in_proj",
    )(x2, gain, w_in, cos_t, sin_t, qg, kg)


def _retention_kernel(q_ref, k_ref, v_ref, g_ref, dmask_ref, zeta_ref, xi_ref, gchunk_ref,
                      gs_ref, o_ref, state_ref, *, chunk, n_chunks):
    @pl.when(pl.program_id(1) == 0)
    def _():
        state_ref[...] = jnp.zeros_like(state_ref)

    n_slabs = RET_WIDTH // LANES
    low = _low_half_mask((chunk, LANES))
    row = lax.broadcasted_iota(jnp.int32, (LANES, LANES), 0)
    col = lax.broadcasted_iota(jnp.int32, (LANES, LANES), 1)
    blockdiag = (row < RET_DK) == (col < RET_DK)
    nt = (((1,), (1,)), ((), ()))
    tn = (((0,), (0,)), ((), ()))

    for c in range(n_chunks):
        rows = slice(c * chunk, (c + 1) * chunk)
        for s in range(n_slabs):
            sl = slice(s * LANES, (s + 1) * LANES)
            zeta = zeta_ref[:, sl]
            xi = xi_ref[:, sl]
            gch = gchunk_ref[:, sl]
            q = q_ref[rows, sl]
            k = k_ref[rows, sl]
            v = v_ref[rows, sl]
            zero = jnp.zeros_like(q)
            q2 = jnp.concatenate([jnp.where(low, q, zero), jnp.where(low, zero, q)], axis=0)
            s2 = lax.dot_general(q2, k, nt, preferred_element_type=F32) * dmask_ref[s]
            o2 = jnp.dot(s2.astype(BF16), v, preferred_element_type=F32)
            r_prev = state_ref[s]
            o_inter = jnp.dot(q, r_prev.astype(BF16), preferred_element_type=F32) * xi
            o = jnp.where(low, o2[:chunk], o2[chunk:]) + o_inter
            kz = (k.astype(F32) * zeta).astype(BF16)
            kv = lax.dot_general(kz, v, tn, preferred_element_type=F32)
            state_ref[s] = r_prev * gch + jnp.where(blockdiag, kv, 0.0)
            sq = o * o
            m_lo = jnp.sum(jnp.where(low, sq, 0.0), axis=-1, keepdims=True)
            m_hi = jnp.sum(jnp.where(low, 0.0, sq), axis=-1, keepdims=True)
            inv = jnp.where(low, lax.rsqrt(m_lo * (1.0 / RET_DV) + EPS),
                            lax.rsqrt(m_hi * (1.0 / RET_DV) + EPS))
            out = o * inv * g_ref[rows, sl].astype(F32) * gs_ref[:, sl]
            o_ref[rows, sl] = out.astype(o_ref.dtype)


def _retention(rq, rk, rv, rg, dmask, zeta, xi, gchunk, gs_ret, batch, seq, ts, chunk):
    T = rq.shape[0]
    nblk = seq // ts
    tok = lambda b, i: (b * nblk + i, 0)
    c2 = lambda b, i: (0, 0)
    c3 = lambda b, i: (0, 0, 0)
    kern = functools.partial(_retention_kernel, chunk=chunk, n_chunks=ts // chunk)
    return pl.pallas_call(
        kern,
        grid=(batch, nblk),
        in_specs=[pl.BlockSpec((ts, RET_WIDTH), tok)] * 4 + [
            pl.BlockSpec((RET_HEADS // 2, 2 * chunk, chunk), c3),
            pl.BlockSpec((chunk, RET_WIDTH), c2),
            pl.BlockSpec((chunk, RET_WIDTH), c2),
            pl.BlockSpec((1, RET_WIDTH), c2),
            pl.BlockSpec((1, RET_WIDTH), c2),
        ],
        out_specs=pl.BlockSpec((ts, RET_WIDTH), tok),
        out_shape=jax.ShapeDtypeStruct((T, RET_WIDTH), BF16),
        scratch_shapes=[pltpu.VMEM((RET_WIDTH // LANES, LANES, LANES), F32)],
        compiler_params=pltpu.CompilerParams(
            dimension_semantics=("parallel", "arbitrary"), vmem_limit_bytes=VMEM_LIMIT),
        name="retention",
    )(rq, rk, rv, rg, dmask, zeta, xi, gchunk, gs_ret)


def _diff_attn_kernel(q_ref, k_ref, vt_ref, lam_ref, subln_ref, gs_ref, *refs,
                      n_cast, heads, tq, tk, rb, lam_init):
    w_refs = refs[:n_cast]
    o_ref = refs[n_cast]
    wb_refs = refs[n_cast + 1:2 * n_cast + 1]
    qs_all, s_all, mc_all, m_all, acc_all = refs[2 * n_cast + 1:]
    i = pl.program_id(2)
    nt = (((1,), (1,)), ((), ()))
    n_blocks = 2 * tq // rb
    n_diag = tq // tk
    n_full = i * n_diag
    head_ids = range(heads)

    def lanes(hh):
        return pl.ds(hh * LANES, LANES)

    def diag_kind(d, b):
        q_first = (b * rb) % tq
        if d * tk > q_first + rb - 1:
            return "skip"
        return "full" if (d + 1) * tk - 1 <= q_first else "mask"

    def scores(hh, j, b, key_offset=None):
        start = pl.multiple_of(j * tk, tk)
        cols = slice(b * rb, (b + 1) * rb)
        s = lax.dot_general(k_ref[pl.ds(start, tk), lanes(hh)], qs_all[hh, cols, :], nt,
                            preferred_element_type=F32)
        if key_offset is not None:
            kpos = lax.broadcasted_iota(jnp.int32, s.shape, 0) + key_offset
            qpos = lax.broadcasted_iota(jnp.int32, s.shape, 1) + (b * rb) % tq
            s = jnp.where(kpos <= qpos, s, -jnp.inf)
        s_all[hh, :, cols] = s
        mc_all[hh, :, cols] = jnp.broadcast_to(jnp.max(s, axis=0, keepdims=True),
                                               (F32_SUBLANES, rb))

    def consume(hh, j, b):
        start = pl.multiple_of(j * tk, tk)
        cols = slice(b * rb, (b + 1) * rb)
        m_prev = m_all[hh, :, cols]
        m_new = jnp.maximum(m_prev, mc_all[hh, :, cols])
        alpha = jnp.exp2(m_prev - m_new)
        p = jnp.exp2(s_all[hh, :, cols] - m_new[0:1, :])
        pv = jnp.dot(vt_ref[hh * VT_ROWS:(hh + 1) * VT_ROWS, pl.ds(start, tk)], p.astype(BF16),
                     preferred_element_type=F32)
        acc_all[hh, :, cols] = alpha[0:1, :] * acc_all[hh, :, cols] + pv
        m_all[hh, :, cols] = m_new

    for hh in head_ids:
        q = q_ref[:, lanes(hh)]
        low = _low_half_mask(q.shape)
        zero = jnp.zeros_like(q)
        qs_all[hh, :tq, :] = jnp.where(low, q, zero)
        qs_all[hh, tq:, :] = jnp.where(low, zero, q)

    ahead = n_blocks // 2
    order = sorted(range(n_blocks), key=lambda b: -((b * rb) % tq))
    assert all(diag_kind(0, b) == "full" for b in order[:ahead])
    for b in order[:ahead]:
        for hh in head_ids:
            scores(hh, 0, b)
    m_all[...] = jnp.full_like(m_all, -jnp.inf)
    acc_all[...] = jnp.zeros_like(acc_all)

    def full_chunks(j0, count):
        for u in range(count):
            j = j0 + u
            for pos, b in enumerate(order):
                for hh in head_ids:
                    if pos + ahead < n_blocks:
                        scores(hh, j, order[pos + ahead])
                    else:
                        scores(hh, j + 1, order[pos + ahead - n_blocks])
                    consume(hh, j, b)

    def body(t, carry):
        full_chunks(t * (2 * n_diag), 2 * n_diag)
        return carry

    lax.fori_loop(0, i // 2, body, 0)

    @pl.when(i % 2 == 1)
    def _():
        full_chunks((i - 1) * n_diag, n_diag)

    lv = lam_ref[...]
    lam = (jnp.exp(jnp.sum(lv[0:1] * lv[1:2], axis=-1, keepdims=True))
           - jnp.exp(jnp.sum(lv[2:3] * lv[3:4], axis=-1, keepdims=True)) + lam_init)

    def finalize(hh, d):
        a0 = acc_all[hh, :, d * tk:(d + 1) * tk]
        a1 = acc_all[hh, :, tq + d * tk:tq + (d + 1) * tk]
        o0 = a0[:DIFF_DV, :] * (1.0 / a0[DIFF_DV:DIFF_DV + 1, :])
        o1 = a1[:DIFF_DV, :] * (1.0 / a1[DIFF_DV:DIFF_DV + 1, :])
        o = (o0 - lam * o1).T
        out_gain = subln_ref[...] * (1.0 - lam_init) * gs_ref[:, lanes(hh)]
        o_ref[d * tk:(d + 1) * tk, lanes(hh)] = (o * _rms(o) * out_gain).astype(o_ref.dtype)

    for d in range(n_diag):
        for pos, b in enumerate(order):
            nd, npos = ((d, pos + ahead) if pos + ahead < n_blocks
                        else (d + 1, pos + ahead - n_blocks))
            for hh in head_ids:
                if nd < n_diag and diag_kind(nd, order[npos]) != "skip":
                    masked = diag_kind(nd, order[npos]) == "mask"
                    scores(hh, n_full + nd, order[npos], nd * tk if masked else None)
                if diag_kind(d, b) != "skip":
                    consume(hh, n_full + d, b)
        for hh in head_ids:
            finalize(hh, d)
        if d == 0:
            for w_ref, wb_ref in zip(w_refs, wb_refs):
                wb_ref[...] = w_ref[...].astype(wb_ref.dtype)


def _diff_attn(dq, dk, dvt, lam_p, subln, gs_diff, cast_weights, batch, seq, tq, tk, rb, heads,
               lam_init):
    T = dq.shape[0]
    nq = seq // tq
    n_groups = DIFF_HEADS // heads
    n_steps = batch * n_groups * nq
    assert seq % tq == 0 and tq % tk == 0 and tq % rb == 0 and DIFF_HEADS % heads == 0
    assert RET_WIDTH % (heads * LANES) == 0

    def cast_spec(w):
        rows = w.shape[0]
        n_blk = max(n for n in range(1, n_steps + 1)
                    if n_steps % n == 0 and rows % (n * BF16_SUBLANES) == 0)
        hold = n_steps // n_blk
        return pl.BlockSpec(
            (rows // n_blk, w.shape[1]),
            lambda b, g, i: (((b * n_groups + g) * nq + i) // hold, 0))

    w_specs = [cast_spec(w) for w in cast_weights]
    kern = functools.partial(_diff_attn_kernel, n_cast=len(cast_weights), heads=heads, tq=tq,
                             tk=tk, rb=rb, lam_init=lam_init)
    per_head = lambda shape, dtype: pltpu.VMEM((heads,) + shape, dtype)
    outs = pl.pallas_call(
        kern,
        grid=(batch, n_groups, nq),
        in_specs=[
            pl.BlockSpec((tq, heads * LANES), lambda b, g, i: (b * nq + i, g)),
            pl.BlockSpec((seq, heads * LANES), lambda b, g, i: (b, g)),
            pl.BlockSpec((heads * VT_ROWS, seq), lambda b, g, i: (g, b)),
            pl.BlockSpec((4, DIFF_DQK), lambda b, g, i: (0, 0)),
            pl.BlockSpec((1, LANES), lambda b, g, i: (0, 0)),
            pl.BlockSpec((1, heads * LANES),
                         lambda b, g, i: (0, RET_WIDTH // (heads * LANES) + g)),
        ] + w_specs,
        out_specs=[pl.BlockSpec((tq, heads * LANES), lambda b, g, i: (b * nq + i, g))]
        + w_specs,
        out_shape=[jax.ShapeDtypeStruct((T, DIFF_WIDTH), BF16)]
        + [jax.ShapeDtypeStruct(w.shape, BF16) for w in cast_weights],
        scratch_shapes=[per_head((2 * tq, LANES), BF16), per_head((tk, 2 * tq), F32),
                        per_head((F32_SUBLANES, 2 * tq), F32),
                        per_head((F32_SUBLANES, 2 * tq), F32),
                        per_head((VT_ROWS, 2 * tq), F32)],
        compiler_params=pltpu.CompilerParams(
            dimension_semantics=("parallel", "parallel", "arbitrary"),
            vmem_limit_bytes=VMEM_LIMIT),
        name="diff_attn",
    )(dq, dk, dvt, lam_p, subln, gs_diff, *cast_weights)
    return outs[0], outs[1:]


def _mix_xattn_kernel(x_ref, oret_ref, odiff_ref, wout_ref, nx_ref, xq_ref, qg_ref,
                      mem_ref, nm_ref, xkv_ref, kg_ref, xo_ref, out_ref, kmem_ref, vmem_ref,
                      *, tiles_per_batch):
    D = x_ref.shape[1]
    dh = D // XATTN_HEADS

    @pl.when(pl.program_id(0) % tiles_per_batch == 0)
    def _():
        m = mem_ref[...]
        hm = (m * _rms(m) * nm_ref[...]).astype(BF16)
        k = jnp.dot(hm, xkv_ref[:, :D], preferred_element_type=F32)
        vmem_ref[...] = jnp.dot(hm, xkv_ref[:, D:], preferred_element_type=F32).astype(BF16)
        for hh in range(XATTN_HEADS):
            kh = k[:, hh * dh:(hh + 1) * dh]
            kmem_ref[:, hh * dh:(hh + 1) * dh] = (kh * _rms(kh) * kg_ref[...]).astype(BF16)

    x1 = (x_ref[...]
          + jnp.dot(oret_ref[...], wout_ref[:RET_WIDTH, :], preferred_element_type=F32)
          + jnp.dot(odiff_ref[...], wout_ref[RET_WIDTH:, :], preferred_element_type=F32))
    h = (x1 * _rms(x1) * nx_ref[...]).astype(BF16)
    q = jnp.dot(h, xq_ref[...], preferred_element_type=F32)
    nt = (((1,), (1,)), ((), ()))
    outs = []
    for hh in range(XATTN_HEADS):
        sl = slice(hh * dh, (hh + 1) * dh)
        qh = q[:, sl]
        qn = (qh * _rms(qh) * qg_ref[...] * (dh ** -0.5)).astype(BF16)
        s = lax.dot_general(qn, kmem_ref[:, sl], nt, preferred_element_type=F32)
        e = jnp.exp(s - jnp.max(s, axis=-1, keepdims=True))
        p = e * (1.0 / jnp.sum(e, axis=-1, keepdims=True))
        outs.append(jnp.dot(p.astype(BF16), vmem_ref[:, sl],
                            preferred_element_type=F32).astype(BF16))
    o = jnp.concatenate(outs, axis=1)
    out_ref[...] = x1 + jnp.dot(o, xo_ref[...], preferred_element_type=F32)


def _mix_xattn(x2, o_ret, o_diff, w_out, nx, xq, qg, mem2, nm, xkv, kg, xo, seq, mem_len, tm):
    T, D = x2.shape
    nblk = seq // tm
    tok = lambda i: (i, 0)
    const = lambda i: (0, 0)
    memb = lambda i: (i // nblk, 0)
    return pl.pallas_call(
        functools.partial(_mix_xattn_kernel, tiles_per_batch=nblk),
        grid=(T // tm,),
        in_specs=[
            pl.BlockSpec((tm, D), tok),
            pl.BlockSpec((tm, RET_WIDTH), tok),
            pl.BlockSpec((tm, DIFF_WIDTH), tok),
            pl.BlockSpec((D, D), const),
            pl.BlockSpec((1, D), const),
            pl.BlockSpec((D, D), const),
            pl.BlockSpec((1, D // XATTN_HEADS), const),
            pl.BlockSpec((mem_len, D), memb),
            pl.BlockSpec((1, D), const),
            pl.BlockSpec((D, 2 * D), const),
            pl.BlockSpec((1, D // XATTN_HEADS), const),
            pl.BlockSpec((D, D), const),
        ],
        out_specs=pl.BlockSpec((tm, D), tok),
        out_shape=jax.ShapeDtypeStruct((T, D), F32),
        scratch_shapes=[pltpu.VMEM((mem_len, D), BF16), pltpu.VMEM((mem_len, D), BF16)],
        compiler_params=pltpu.CompilerParams(
            dimension_semantics=("arbitrary",), vmem_limit_bytes=VMEM_LIMIT),
        name="mix_xattn",
    )(x2, o_ret, o_diff, w_out, nx, xq, qg, mem2, nm, xkv, kg, xo)


def _ffn_kernel(x_ref, gain_ref, wg_ref, wu_ref, wd_ref, out_ref, *, hidden_chunk):
    x = x_ref[...]
    h = (x * _rms(x) * gain_ref[...]).astype(BF16)
    hidden = wg_ref.shape[1]
    acc = x
    for c in range(hidden // hidden_chunk):
        sl = slice(c * hidden_chunk, (c + 1) * hidden_chunk)
        g = jnp.dot(h, wg_ref[:, sl], preferred_element_type=F32)
        u = jnp.dot(h, wu_ref[:, sl], preferred_element_type=F32)
        a = (g * jax.nn.sigmoid(g) * u).astype(BF16)
        acc = acc + jnp.dot(a, wd_ref[sl, :], preferred_element_type=F32)
    out_ref[...] = acc


def _ffn(x2, gain, wg, wu, wd, tm, hidden_chunk):
    T, D = x2.shape
    hidden = wg.shape[1]
    tok = lambda i: (i, 0)
    const = lambda i: (0, 0)
    kern = functools.partial(_ffn_kernel, hidden_chunk=hidden_chunk)
    return pl.pallas_call(
        kern,
        grid=(T // tm,),
        in_specs=[
            pl.BlockSpec((tm, D), tok),
            pl.BlockSpec((1, D), const),
            pl.BlockSpec((D, hidden), const, pipeline_mode=pl.Buffered(1)),
            pl.BlockSpec((D, hidden), const, pipeline_mode=pl.Buffered(1)),
            pl.BlockSpec((hidden, D), const, pipeline_mode=pl.Buffered(1)),
        ],
        out_specs=pl.BlockSpec((tm, D), tok),
        out_shape=jax.ShapeDtypeStruct((T, D), F32),
        compiler_params=pltpu.CompilerParams(
            dimension_semantics=("parallel",), vmem_limit_bytes=VMEM_LIMIT),
        name="ffn",
    )(x2, gain, wg, wu, wd)


def _rotary_tables(seq):
    half = RET_DK // 2
    inv = 1.0 / (RET_THETA_BASE ** np.linspace(0.0, 1.0, half))
    ang = np.arange(seq, dtype=np.float64)[:, None] * inv[None, :]
    cos, sin = np.cos(ang), np.sin(ang)
    cos_t = np.tile(cos, (1, LANES // half))
    sin_t = np.tile(np.concatenate([-sin, sin], axis=1), (1, LANES // RET_DK))
    return cos_t.astype(np.float32), sin_t.astype(np.float32)


def _decay_tables(chunk):
    H = RET_HEADS
    log_g = np.log1p(-(2.0 ** (-5.0 - np.arange(H, dtype=np.float64))))
    idx = np.arange(chunk, dtype=np.float64)
    rel = idx[:, None] - idx[None, :]
    dmask = np.where(rel >= 0, np.exp(log_g[:, None, None] * np.maximum(rel, 0.0)), 0.0)
    zeta = np.exp(log_g[:, None] * (chunk - 1.0 - idx))
    xi = np.exp(log_g[:, None] * (idx + 1.0))
    g_chunk = np.exp(log_g * chunk)
    widen = lambda t: np.repeat(t.T, RET_DK, axis=1)
    f32 = lambda t: t.astype(np.float32)
    dmask = dmask.reshape(H // 2, 2 * chunk, chunk)
    return (f32(dmask), f32(widen(zeta)), f32(widen(xi)),
            f32(np.repeat(g_chunk, RET_DK)[None, :]))


def kernel(x, mem, norm_mix, w_in, diff_q_gain, diff_k_gain, diff_lambda, diff_subln,
           group_scale, w_out, norm_x, norm_mem, xq, xkv, xq_gain, xk_gain, xo,
           norm_ffn, w_gate, w_up, w_down):
    B, S, D = x.shape
    M = mem.shape[1]
    depth = norm_mix.shape[0]
    x2 = x.reshape(B * S, D)
    mem2 = mem.reshape(B * M, D)

    ret_chunk = 256
    cos_t, sin_t = _rotary_tables(S)
    dmask, zeta, xi, gchunk = _decay_tables(ret_chunk)
    row = lambda a: a.reshape(1, -1)

    for l in range(depth):
        lam_init = 0.8 - 0.6 * math.exp(-0.3 * l)
        rq, rk, rv, rg, dq, dk, dv = _in_proj(
            x2, row(norm_mix[l]), w_in[l], cos_t, sin_t,
            row(diff_q_gain[l]), row(diff_k_gain[l]), S, tm=1024)
        o_ret = _retention(rq, rk, rv, rg, dmask, zeta, xi, gchunk,
                           row(group_scale[l]), B, S, ts=2048, chunk=ret_chunk)
        later_weights = [w_out[l], xq[l], xkv[l], xo[l], w_gate[l], w_up[l], w_down[l]]
        o_diff, (w_out_b, xq_b, xkv_b, xo_b, w_gate_b, w_up_b, w_down_b) = _diff_attn(
            dq, dk, dv, diff_lambda[l], row(diff_subln[l]), row(group_scale[l]),
            later_weights, B, S, tq=1024, tk=512, rb=512, heads=2, lam_init=lam_init)
        x2 = _mix_xattn(x2, o_ret, o_diff, w_out_b, row(norm_x[l]), xq_b, row(xq_gain[l]),
                        mem2, row(norm_mem[l]), xkv_b, row(xk_gain[l]), xo_b, S, M, tm=1024)
        x2 = _ffn(x2, row(norm_ffn[l]), w_gate_b, w_up_b, w_down_b, tm=512, hidden_chunk=256)
    return x2.reshape(B, S, D)
```

```python
import functools
import math

import jax
import jax.numpy as jnp
import numpy as np
from jax import lax
from jax.experimental import pallas as pl
from jax.experimental.pallas import tpu as pltpu

F32 = jnp.float32
BF16 = jnp.bfloat16

EPS = 1e-6
LANES = 128
VMEM_LIMIT = 56 * 1024 * 1024

RET_HEADS = 8
RET_DK = 64
RET_DV = 64
RET_WIDTH = 512
RET_THETA_BASE = 10000.0
DIFF_HEADS = 4
DIFF_DQK = 64
DIFF_DV = 128
DIFF_WIDTH = 512
XATTN_HEADS = 4
SEG = 512
N_SEG = 7
LOG2E = math.log2(math.e)
F32_SUBLANES = 8
BF16_SUBLANES = 16
VT_ROWS = DIFF_DV + BF16_SUBLANES


def _rms(x, axis=-1):
    return lax.rsqrt(jnp.mean(x * x, axis=axis, keepdims=True) + EPS)


def _low_half_mask(shape):
    lane = lax.broadcasted_iota(jnp.int32, shape, len(shape) - 1)
    return (lane % LANES) < (LANES // 2)


def _in_proj_kernel(x_ref, gain_ref, w_ref, cos_ref, sin_ref, qg_ref, kg_ref,
                    rq_ref, rk_ref, rv_ref, rg_ref, dq_ref, dk_ref, dvt_ref, wb_ref):
    @pl.when(pl.program_id(0) == 0)
    def _():
        for seg in range(N_SEG):
            cols = slice(seg * SEG, (seg + 1) * SEG)
            wb_ref[:, cols] = w_ref[:, cols].astype(BF16)

    x = x_ref[...]
    h = (x * _rms(x) * gain_ref[...]).astype(BF16)

    def proj(seg):
        return jnp.dot(h, wb_ref[:, seg * SEG:(seg + 1) * SEG], preferred_element_type=F32)

    cos = cos_ref[...]
    sin = sin_ref[...]
    low = _low_half_mask(cos.shape)
    half = RET_DK // 2
    first_half = (lax.broadcasted_iota(jnp.int32, cos.shape, 1) % RET_DK) < half

    def rotary_store(y, out_ref, scale):
        for j in range(SEG // LANES):
            ys = y[:, j * LANES:(j + 1) * LANES]
            partner = jnp.where(first_half, pltpu.roll(ys, LANES - half, 1),
                                pltpu.roll(ys, half, 1))
            r = ys * cos + partner * sin
            if scale != 1.0:
                r = r * scale
            out_ref[:, j * LANES:(j + 1) * LANES] = r.astype(out_ref.dtype)

    def qknorm_store(y, g_ref, out_ref, scale):
        g = jnp.concatenate([g_ref[...]] * (LANES // DIFF_DQK), axis=1)
        for j in range(SEG // LANES):
            ys = y[:, j * LANES:(j + 1) * LANES]
            sq = ys * ys
            s_lo = jnp.sum(jnp.where(low, sq, 0.0), axis=-1, keepdims=True)
            s_hi = jnp.sum(jnp.where(low, 0.0, sq), axis=-1, keepdims=True)
            inv = jnp.where(low, lax.rsqrt(s_lo * (1.0 / DIFF_DQK) + EPS),
                            lax.rsqrt(s_hi * (1.0 / DIFF_DQK) + EPS))
            r = ys * inv * g
            if scale != 1.0:
                r = r * scale
            out_ref[:, j * LANES:(j + 1) * LANES] = r.astype(out_ref.dtype)

    rotary_store(proj(0), rq_ref, 1.0)
    rotary_store(proj(1), rk_ref, RET_DK ** -0.5)
    rv_ref[...] = proj(2).astype(BF16)
    g = proj(3)
    rg_ref[...] = (g * jax.nn.sigmoid(g)).astype(BF16)
    qknorm_store(proj(4), qg_ref, dq_ref, DIFF_DQK ** -0.5 * LOG2E)
    qknorm_store(proj(5), kg_ref, dk_ref, 1.0)
    dv = proj(6)
    ones = jnp.ones((VT_ROWS - DIFF_DV, dv.shape[0]), BF16)
    for hh in range(DIFF_HEADS):
        dvt_ref[hh * VT_ROWS:hh * VT_ROWS + DIFF_DV, :] = (
            dv[:, hh * DIFF_DV:(hh + 1) * DIFF_DV].T.astype(BF16))
        dvt_ref[hh * VT_ROWS + DIFF_DV:(hh + 1) * VT_ROWS, :] = ones


def _in_proj(x2, gain, w_in, cos_t, sin_t, qg, kg, seq, tm):
    T, D = x2.shape
    n_pos_blocks = seq // tm
    tok = lambda i: (i, 0)
    const = lambda i: (0, 0)
    pos = lambda i: (i % n_pos_blocks, 0)
    out = jax.ShapeDtypeStruct((T, SEG), BF16)
    return pl.pallas_call(
        _in_proj_kernel,
        grid=(T // tm,),
        in_specs=[
            pl.BlockSpec((tm, D), tok),
            pl.BlockSpec((1, D), const),
            pl.BlockSpec((D, N_SEG * SEG), const, pipeline_mode=pl.Buffered(1)),
            pl.BlockSpec((tm, LANES), pos),
            pl.BlockSpec((tm, LANES), pos),
            pl.BlockSpec((1, DIFF_DQK), const),
            pl.BlockSpec((1, DIFF_DQK), const),
        ],
        out_specs=[pl.BlockSpec((tm, SEG), tok)] * (N_SEG - 1)
        + [pl.BlockSpec((DIFF_HEADS * VT_ROWS, tm), lambda i: (0, i))],
        out_shape=[out] * (N_SEG - 1)
        + [jax.ShapeDtypeStruct((DIFF_HEADS * VT_ROWS, T), BF16)],
        scratch_shapes=[pltpu.VMEM((D, N_SEG * SEG), BF16)],
        compiler_params=pltpu.CompilerParams(
            dimension_semantics=("arbitrary",), vmem_limit_bytes=VMEM_LIMIT),
        name="in_proj",
    )(x2, gain, w_in, cos_t, sin_t, qg, kg)


def _retention_kernel(q_ref, k_ref, v_ref, g_ref, dmask_ref, zeta_ref, xi_ref, gchunk_ref,
                      gs_ref, o_ref, state_ref, *, chunk, n_chunks):
    @pl.when(pl.program_id(1) == 0)
    def _():
        state_ref[...] = jnp.zeros_like(state_ref)

    n_slabs = RET_WIDTH // LANES
    low = _low_half_mask((chunk, LANES))
    row = lax.broadcasted_iota(jnp.int32, (LANES, LANES), 0)
    col = lax.broadcasted_iota(jnp.int32, (LANES, LANES), 1)
    blockdiag = (row < RET_DK) == (col < RET_DK)
    nt = (((1,), (1,)), ((), ()))
    tn = (((0,), (0,)), ((), ()))

    for c in range(n_chunks):
        rows = slice(c * chunk, (c + 1) * chunk)
        for s in range(n_slabs):
            sl = slice(s * LANES, (s + 1) * LANES)
            zeta = zeta_ref[:, sl]
            xi = xi_ref[:, sl]
            gch = gchunk_ref[:, sl]
            q = q_ref[rows, sl]
            k = k_ref[rows, sl]
            v = v_ref[rows, sl]
            zero = jnp.zeros_like(q)
            q2 = jnp.concatenate([jnp.where(low, q, zero), jnp.where(low, zero, q)], axis=0)
            s2 = lax.dot_general(q2, k, nt, preferred_element_type=F32) * dmask_ref[s]
            o2 = jnp.dot(s2.astype(BF16), v, preferred_element_type=F32)
            r_prev = state_ref[s]
            o_inter = jnp.dot(q, r_prev.astype(BF16), preferred_element_type=F32) * xi
            o = jnp.where(low, o2[:chunk], o2[chunk:]) + o_inter
            kz = (k.astype(F32) * zeta).astype(BF16)
            kv = lax.dot_general(kz, v, tn, preferred_element_type=F32)
            state_ref[s] = r_prev * gch + jnp.where(blockdiag, kv, 0.0)
            sq = o * o
            m_lo = jnp.sum(jnp.where(low, sq, 0.0), axis=-1, keepdims=True)
            m_hi = jnp.sum(jnp.where(low, 0.0, sq), axis=-1, keepdims=True)
            inv = jnp.where(low, lax.rsqrt(m_lo * (1.0 / RET_DV) + EPS),
                            lax.rsqrt(m_hi * (1.0 / RET_DV) + EPS))
            out = o * inv * g_ref[rows, sl].astype(F32) * gs_ref[:, sl]
            o_ref[rows, sl] = out.astype(o_ref.dtype)


def _retention(rq, rk, rv, rg, dmask, zeta, xi, gchunk, gs_ret, batch, seq, ts, chunk):
    T = rq.shape[0]
    nblk = seq // ts
    tok = lambda b, i: (b * nblk + i, 0)
    c2 = lambda b, i: (0, 0)
    c3 = lambda b, i: (0, 0, 0)
    kern = functools.partial(_retention_kernel, chunk=chunk, n_chunks=ts // chunk)
    return pl.pallas_call(
        kern,
        grid=(batch, nblk),
        in_specs=[pl.BlockSpec((ts, RET_WIDTH), tok)] * 4 + [
            pl.BlockSpec((RET_HEADS // 2, 2 * chunk, chunk), c3),
            pl.BlockSpec((chunk, RET_WIDTH), c2),
            pl.BlockSpec((chunk, RET_WIDTH), c2),
            pl.BlockSpec((1, RET_WIDTH), c2),
            pl.BlockSpec((1, RET_WIDTH), c2),
        ],
        out_specs=pl.BlockSpec((ts, RET_WIDTH), tok),
        out_shape=jax.ShapeDtypeStruct((T, RET_WIDTH), BF16),
        scratch_shapes=[pltpu.VMEM((RET_WIDTH // LANES, LANES, LANES), F32)],
        compiler_params=pltpu.CompilerParams(
            dimension_semantics=("parallel", "arbitrary"), vmem_limit_bytes=VMEM_LIMIT),
        name="retention",
    )(rq, rk, rv, rg, dmask, zeta, xi, gchunk, gs_ret)


def _diff_attn_kernel(q_ref, k_ref, vt_ref, lam_ref, subln_ref, gs_ref, *refs,
                      n_cast, heads, tq, tk, rb, lam_init):
    w_refs = refs[:n_cast]
    o_ref = refs[n_cast]
    wb_refs = refs[n_cast + 1:2 * n_cast + 1]
    qs_all, s_all, mc_all, m_all, acc_all = refs[2 * n_cast + 1:]
    i = pl.program_id(2)
    nt = (((1,), (1,)), ((), ()))
    n_blocks = 2 * tq // rb
    n_diag = tq // tk
    n_full = i * n_diag
    head_ids = range(heads)

    def lanes(hh):
        return pl.ds(hh * LANES, LANES)

    def diag_kind(d, b):
        q_first = (b * rb) % tq
        if d * tk > q_first + rb - 1:
            return "skip"
        return "full" if (d + 1) * tk - 1 <= q_first else "mask"

    def scores(hh, j, b, key_offset=None):
        start = pl.multiple_of(j * tk, tk)
        cols = slice(b * rb, (b + 1) * rb)
        s = lax.dot_general(k_ref[pl.ds(start, tk), lanes(hh)], qs_all[hh, cols, :], nt,
                            preferred_element_type=F32)
        if key_offset is not None:
            kpos = lax.broadcasted_iota(jnp.int32, s.shape, 0) + key_offset
            qpos = lax.broadcasted_iota(jnp.int32, s.shape, 1) + (b * rb) % tq
            s = jnp.where(kpos <= qpos, s, -jnp.inf)
        s_all[hh, :, cols] = s
        mc_all[hh, :, cols] = jnp.broadcast_to(jnp.max(s, axis=0, keepdims=True),
                                               (F32_SUBLANES, rb))

    def consume(hh, j, b):
        start = pl.multiple_of(j * tk, tk)
        cols = slice(b * rb, (b + 1) * rb)
        m_prev = m_all[hh, :, cols]
        m_new = jnp.maximum(m_prev, mc_all[hh, :, cols])
        alpha = jnp.exp2(m_prev - m_new)
        p = jnp.exp2(s_all[hh, :, cols] - m_new[0:1, :])
        pv = jnp.dot(vt_ref[hh * VT_ROWS:(hh + 1) * VT_ROWS, pl.ds(start, tk)], p.astype(BF16),
                     preferred_element_type=F32)
        acc_all[hh, :, cols] = alpha[0:1, :] * acc_all[hh, :, cols] + pv
        m_all[hh, :, cols] = m_new

    for hh in head_ids:
        q = q_ref[:, lanes(hh)]
        low = _low_half_mask(q.shape)
        zero = jnp.zeros_like(q)
        qs_all[hh, :tq, :] = jnp.where(low, q, zero)
        qs_all[hh, tq:, :] = jnp.where(low, zero, q)

    ahead = n_blocks // 2
    order = sorted(range(n_blocks), key=lambda b: -((b * rb) % tq))
    assert all(diag_kind(0, b) == "full" for b in order[:ahead])
    for b in order[:ahead]:
        for hh in head_ids:
            scores(hh, 0, b)
    m_all[...] = jnp.full_like(m_all, -jnp.inf)
    acc_all[...] = jnp.zeros_like(acc_all)

    def full_chunks(j0, count):
        for u in range(count):
            j = j0 + u
            for pos, b in enumerate(order):
                for hh in head_ids:
                    if pos + ahead < n_blocks:
                        scores(hh, j, order[pos + ahead])
                    else:
                        scores(hh, j + 1, order[pos + ahead - n_blocks])
                    consume(hh, j, b)

    def body(t, carry):
        full_chunks(t * (2 * n_diag), 2 * n_diag)
        return carry

    lax.fori_loop(0, i // 2, body, 0)

    @pl.when(i % 2 == 1)
    def _():
        full_chunks((i - 1) * n_diag, n_diag)

    lv = lam_ref[...]
    lam = (jnp.exp(jnp.sum(lv[0:1] * lv[1:2], axis=-1, keepdims=True))
           - jnp.exp(jnp.sum(lv[2:3] * lv[3:4], axis=-1, keepdims=True)) + lam_init)

    def finalize(hh, d):
        a0 = acc_all[hh, :, d * tk:(d + 1) * tk]
        a1 = acc_all[hh, :, tq + d * tk:tq + (d + 1) * tk]
        o0 = a0[:DIFF_DV, :] * (1.0 / a0[DIFF_DV:DIFF_DV + 1, :])
        o1 = a1[:DIFF_DV, :] * (1.0 / a1[DIFF_DV:DIFF_DV + 1, :])
        o = (o0 - lam * o1).T
        out_gain = subln_ref[...] * (1.0 - lam_init) * gs_ref[:, lanes(hh)]
        o_ref[d * tk:(d + 1) * tk, lanes(hh)] = (o * _rms(o) * out_gain).astype(o_ref.dtype)

    for d in range(n_diag):
        for pos, b in enumerate(order):
            nd, npos = ((d, pos + ahead) if pos + ahead < n_blocks
                        else (d + 1, pos + ahead - n_blocks))
            for hh in head_ids:
                if nd < n_diag and diag_kind(nd, order[npos]) != "skip":
                    masked = diag_kind(nd, order[npos]) == "mask"
                    scores(hh, n_full + nd, order[npos], nd * tk if masked else None)
                if diag_kind(d, b) != "skip":
                    consume(hh, n_full + d, b)
        for hh in head_ids:
            finalize(hh, d)
        if d == 0:
            for w_ref, wb_ref in zip(w_refs, wb_refs):
                wb_ref[...] = w_ref[...].astype(wb_ref.dtype)


def _diff_attn(dq, dk, dvt, lam_p, subln, gs_diff, cast_weights, batch, seq, tq, tk, rb, heads,
               lam_init):
    T = dq.shape[0]
    nq = seq // tq
    n_groups = DIFF_HEADS // heads
    n_steps = batch * n_groups * nq
    assert seq % tq == 0 and tq % tk == 0 and tq % rb == 0 and DIFF_HEADS % heads == 0
    assert RET_WIDTH % (heads * LANES) == 0

    def cast_spec(w):
        rows = w.shape[0]
        n_blk = max(n for n in range(1, n_steps + 1)
                    if n_steps % n == 0 and rows % (n * BF16_SUBLANES) == 0)
        hold = n_steps // n_blk
        return pl.BlockSpec(
            (rows // n_blk, w.shape[1]),
            lambda b, g, i: (((b * n_groups + g) * nq + i) // hold, 0))

    w_specs = [cast_spec(w) for w in cast_weights]
    kern = functools.partial(_diff_attn_kernel, n_cast=len(cast_weights), heads=heads, tq=tq,
                             tk=tk, rb=rb, lam_init=lam_init)
    per_head = lambda shape, dtype: pltpu.VMEM((heads,) + shape, dtype)
    outs = pl.pallas_call(
        kern,
        grid=(batch, n_groups, nq),
        in_specs=[
            pl.BlockSpec((tq, heads * LANES), lambda b, g, i: (b * nq + i, g)),
            pl.BlockSpec((seq, heads * LANES), lambda b, g, i: (b, g)),
            pl.BlockSpec((heads * VT_ROWS, seq), lambda b, g, i: (g, b)),
            pl.BlockSpec((4, DIFF_DQK), lambda b, g, i: (0, 0)),
            pl.BlockSpec((1, LANES), lambda b, g, i: (0, 0)),
            pl.BlockSpec((1, heads * LANES),
                         lambda b, g, i: (0, RET_WIDTH // (heads * LANES) + g)),
        ] + w_specs,
        out_specs=[pl.BlockSpec((tq, heads * LANES), lambda b, g, i: (b * nq + i, g))]
        + w_specs,
        out_shape=[jax.ShapeDtypeStruct((T, DIFF_WIDTH), BF16)]
        + [jax.ShapeDtypeStruct(w.shape, BF16) for w in cast_weights],
        scratch_shapes=[per_head((2 * tq, LANES), BF16), per_head((tk, 2 * tq), F32),
                        per_head((F32_SUBLANES, 2 * tq), F32),
                        per_head((F32_SUBLANES, 2 * tq), F32),
                        per_head((VT_ROWS, 2 * tq), F32)],
        compiler_params=pltpu.CompilerParams(
            dimension_semantics=("parallel", "parallel", "arbitrary"),
            vmem_limit_bytes=VMEM_LIMIT),
        name="diff_attn",
    )(dq, dk, dvt, lam_p, subln, gs_diff, *cast_weights)
    return outs[0], outs[1:]


def _mix_xattn_kernel(x_ref, oret_ref, odiff_ref, wout_ref, nx_ref, xq_ref, qg_ref,
                      mem_ref, nm_ref, xkv_ref, kg_ref, xo_ref, out_ref, kmem_ref, vmem_ref,
                      *, tiles_per_batch):
    D = x_ref.shape[1]
    dh = D // XATTN_HEADS

    @pl.when(pl.program_id(0) % tiles_per_batch == 0)
    def _():
        m = mem_ref[...]
        hm = (m * _rms(m) * nm_ref[...]).astype(BF16)
        k = jnp.dot(hm, xkv_ref[:, :D], preferred_element_type=F32)
        vmem_ref[...] = jnp.dot(hm, xkv_ref[:, D:], preferred_element_type=F32).astype(BF16)
        for hh in range(XATTN_HEADS):
            kh = k[:, hh * dh:(hh + 1) * dh]
            kmem_ref[:, hh * dh:(hh + 1) * dh] = (kh * _rms(kh) * kg_ref[...]).astype(BF16)

    x1 = (x_ref[...]
          + jnp.dot(oret_ref[...], wout_ref[:RET_WIDTH, :], preferred_element_type=F32)
          + jnp.dot(odiff_ref[...], wout_ref[RET_WIDTH:, :], preferred_element_type=F32))
    h = (x1 * _rms(x1) * nx_ref[...]).astype(BF16)
    q = jnp.dot(h, xq_ref[...], preferred_element_type=F32)
    nt = (((1,), (1,)), ((), ()))
    outs = []
    for hh in range(XATTN_HEADS):
        sl = slice(hh * dh, (hh + 1) * dh)
        qh = q[:, sl]
        qn = (qh * _rms(qh) * qg_ref[...] * (dh ** -0.5)).astype(BF16)
        s = lax.dot_general(qn, kmem_ref[:, sl], nt, preferred_element_type=F32)
        e = jnp.exp(s - jnp.max(s, axis=-1, keepdims=True))
        p = e * (1.0 / jnp.sum(e, axis=-1, keepdims=True))
        outs.append(jnp.dot(p.astype(BF16), vmem_ref[:, sl],
                            preferred_element_type=F32).astype(BF16))
    o = jnp.concatenate(outs, axis=1)
    out_ref[...] = x1 + jnp.dot(o, xo_ref[...], preferred_element_type=F32)


def _mix_xattn(x2, o_ret, o_diff, w_out, nx, xq, qg, mem2, nm, xkv, kg, xo, seq, mem_len, tm):
    T, D = x2.shape
    nblk = seq // tm
    tok = lambda i: (i, 0)
    const = lambda i: (0, 0)
    memb = lambda i: (i // nblk, 0)
    return pl.pallas_call(
        functools.partial(_mix_xattn_kernel, tiles_per_batch=nblk),
        grid=(T // tm,),
        in_specs=[
            pl.BlockSpec((tm, D), tok),
            pl.BlockSpec((tm, RET_WIDTH), tok),
            pl.BlockSpec((tm, DIFF_WIDTH), tok),
            pl.BlockSpec((D, D), const),
            pl.BlockSpec((1, D), const),
            pl.BlockSpec((D, D), const),
            pl.BlockSpec((1, D // XATTN_HEADS), const),
            pl.BlockSpec((mem_len, D), memb),
            pl.BlockSpec((1, D), const),
            pl.BlockSpec((D, 2 * D), const),
            pl.BlockSpec((1, D // XATTN_HEADS), const),
            pl.BlockSpec((D, D), const),
        ],
        out_specs=pl.BlockSpec((tm, D), tok),
        out_shape=jax.ShapeDtypeStruct((T, D), F32),
        scratch_shapes=[pltpu.VMEM((mem_len, D), BF16), pltpu.VMEM((mem_len, D), BF16)],
        compiler_params=pltpu.CompilerParams(
            dimension_semantics=("arbitrary",), vmem_limit_bytes=VMEM_LIMIT),
        name="mix_xattn",
    )(x2, o_ret, o_diff, w_out, nx, xq, qg, mem2, nm, xkv, kg, xo)


def _ffn_kernel(x_ref, gain_ref, wg_ref, wu_ref, wd_ref, out_ref, *, hidden_chunk):
    x = x_ref[...]
    h = (x * _rms(x) * gain_ref[...]).astype(BF16)
    hidden = wg_ref.shape[1]
    acc = x
    for c in range(hidden // hidden_chunk):
        sl = slice(c * hidden_chunk, (c + 1) * hidden_chunk)
        g = jnp.dot(h, wg_ref[:, sl], preferred_element_type=F32)
        u = jnp.dot(h, wu_ref[:, sl], preferred_element_type=F32)
        a = (g * jax.nn.sigmoid(g) * u).astype(BF16)
        acc = acc + jnp.dot(a, wd_ref[sl, :], preferred_element_type=F32)
    out_ref[...] = acc


def _ffn(x2, gain, wg, wu, wd, tm, hidden_chunk):
    T, D = x2.shape
    hidden = wg.shape[1]
    tok = lambda i: (i, 0)
    const = lambda i: (0, 0)
    kern = functools.partial(_ffn_kernel, hidden_chunk=hidden_chunk)
    return pl.pallas_call(
        kern,
        grid=(T // tm,),
        in_specs=[
            pl.BlockSpec((tm, D), tok),
            pl.BlockSpec((1, D), const),
            pl.BlockSpec((D, hidden), const, pipeline_mode=pl.Buffered(1)),
            pl.BlockSpec((D, hidden), const, pipeline_mode=pl.Buffered(1)),
            pl.BlockSpec((hidden, D), const, pipeline_mode=pl.Buffered(1)),
        ],
        out_specs=pl.BlockSpec((tm, D), tok),
        out_shape=jax.ShapeDtypeStruct((T, D), F32),
        compiler_params=pltpu.CompilerParams(
            dimension_semantics=("parallel",), vmem_limit_bytes=VMEM_LIMIT),
        name="ffn",
    )(x2, gain, wg, wu, wd)


def _rotary_tables(seq):
    half = RET_DK // 2
    inv = 1.0 / (RET_THETA_BASE ** np.linspace(0.0, 1.0, half))
    ang = np.arange(seq, dtype=np.float64)[:, None] * inv[None, :]
    cos, sin = np.cos(ang), np.sin(ang)
    cos_t = np.tile(cos, (1, LANES // half))
    sin_t = np.tile(np.concatenate([-sin, sin], axis=1), (1, LANES // RET_DK))
    return cos_t.astype(np.float32), sin_t.astype(np.float32)


def _decay_tables(chunk):
    H = RET_HEADS
    log_g = np.log1p(-(2.0 ** (-5.0 - np.arange(H, dtype=np.float64))))
    idx = np.arange(chunk, dtype=np.float64)
    rel = idx[:, None] - idx[None, :]
    dmask = np.where(rel >= 0, np.exp(log_g[:, None, None] * np.maximum(rel, 0.0)), 0.0)
    zeta = np.exp(log_g[:, None] * (chunk - 1.0 - idx))
    xi = np.exp(log_g[:, None] * (idx + 1.0))
    g_chunk = np.exp(log_g * chunk)
    widen = lambda t: np.repeat(t.T, RET_DK, axis=1)
    f32 = lambda t: t.astype(np.float32)
    dmask = dmask.reshape(H // 2, 2 * chunk, chunk)
    return (f32(dmask), f32(widen(zeta)), f32(widen(xi)),
            f32(np.repeat(g_chunk, RET_DK)[None, :]))


def kernel(x, mem, norm_mix, w_in, diff_q_gain, diff_k_gain, diff_lambda, diff_subln,
           group_scale, w_out, norm_x, norm_mem, xq, xkv, xq_gain, xk_gain, xo,
           norm_ffn, w_gate, w_up, w_down):
    B, S, D = x.shape
    M = mem.shape[1]
    depth = norm_mix.shape[0]
    x2 = x.reshape(B * S, D)
    mem2 = mem.reshape(B * M, D)

    ret_chunk = 256
    cos_t, sin_t = _rotary_tables(S)
    dmask, zeta, xi, gchunk = _decay_tables(ret_chunk)
    row = lambda a: a.reshape(1, -1)

    for l in range(depth):
        lam_init = 0.8 - 0.6 * math.exp(-0.3 * l)
        rq, rk, rv, rg, dq, dk, dv = _in_proj(
            x2, row(norm_mix[l]), w_in[l], cos_t, sin_t,
            row(diff_q_gain[l]), row(diff_k_gain[l]), S, tm=1024)
        o_ret = _retention(rq, rk, rv, rg, dmask, zeta, xi, gchunk,
                           row(group_scale[l]), B, S, ts=2048, chunk=ret_chunk)
        later_weights = [w_out[l], xq[l], xkv[l], xo[l], w_gate[l], w_up[l], w_down[l]]
        o_diff, (w_out_b, xq_b, xkv_b, xo_b, w_gate_b, w_up_b, w_down_b) = _diff_attn(
            dq, dk, dv, diff_lambda[l], row(diff_subln[l]), row(group_scale[l]),
            later_weights, B, S, tq=1024, tk=512, rb=512, heads=2, lam_init=lam_init)
        x2 = _mix_xattn(x2, o_ret, o_diff, w_out_b, row(norm_x[l]), xq_b, row(xq_gain[l]),
                        mem2, row(norm_mem[l]), xkv_b, row(xk_gain[l]), xo_b, S, M, tm=1024)
        x2 = _ffn(x2, row(norm_ffn[l]), w_gate_b, w_up_b, w_down_b, tm=1024, hidden_chunk=256)
    return x2.reshape(B, S, D)
```

```python
import functools
import math

import jax
import jax.numpy as jnp
import numpy as np
from jax import lax
from jax.experimental import pallas as pl
from jax.experimental.pallas import tpu as pltpu

F32 = jnp.float32
BF16 = jnp.bfloat16

EPS = 1e-6
LANES = 128
VMEM_LIMIT = 56 * 1024 * 1024

RET_HEADS = 8
RET_DK = 64
RET_DV = 64
RET_WIDTH = 512
RET_THETA_BASE = 10000.0
DIFF_HEADS = 4
DIFF_DQK = 64
DIFF_DV = 128
DIFF_WIDTH = 512
XATTN_HEADS = 4
SEG = 512
N_SEG = 7
LOG2E = math.log2(math.e)
F32_SUBLANES = 8
BF16_SUBLANES = 16
VT_ROWS = DIFF_DV + BF16_SUBLANES


def _rms(x, axis=-1):
    return lax.rsqrt(jnp.mean(x * x, axis=axis, keepdims=True) + EPS)


def _low_half_mask(shape):
    lane = lax.broadcasted_iota(jnp.int32, shape, len(shape) - 1)
    return (lane % LANES) < (LANES // 2)


def _in_proj_kernel(x_ref, gain_ref, w_ref, cos_ref, sin_ref, qg_ref, kg_ref,
                    rq_ref, rk_ref, rv_ref, rg_ref, dq_ref, dk_ref, dvt_ref, wb_ref):
    @pl.when(pl.program_id(0) == 0)
    def _():
        for seg in range(N_SEG):
            cols = slice(seg * SEG, (seg + 1) * SEG)
            wb_ref[:, cols] = w_ref[:, cols].astype(BF16)

    x = x_ref[...]
    h = (x * _rms(x) * gain_ref[...]).astype(BF16)

    def proj(seg):
        return jnp.dot(h, wb_ref[:, seg * SEG:(seg + 1) * SEG], preferred_element_type=F32)

    cos = cos_ref[...]
    sin = sin_ref[...]
    low = _low_half_mask(cos.shape)
    half = RET_DK // 2
    first_half = (lax.broadcasted_iota(jnp.int32, cos.shape, 1) % RET_DK) < half

    def rotary_store(y, out_ref, scale):
        for j in range(SEG // LANES):
            ys = y[:, j * LANES:(j + 1) * LANES]
            partner = jnp.where(first_half, pltpu.roll(ys, LANES - half, 1),
                                pltpu.roll(ys, half, 1))
            r = ys * cos + partner * sin
            if scale != 1.0:
                r = r * scale
            out_ref[:, j * LANES:(j + 1) * LANES] = r.astype(out_ref.dtype)

    def qknorm_store(y, g_ref, out_ref, scale):
        g = jnp.concatenate([g_ref[...]] * (LANES // DIFF_DQK), axis=1)
        for j in range(SEG // LANES):
            ys = y[:, j * LANES:(j + 1) * LANES]
            sq = ys * ys
            s_lo = jnp.sum(jnp.where(low, sq, 0.0), axis=-1, keepdims=True)
            s_hi = jnp.sum(jnp.where(low, 0.0, sq), axis=-1, keepdims=True)
            inv = jnp.where(low, lax.rsqrt(s_lo * (1.0 / DIFF_DQK) + EPS),
                            lax.rsqrt(s_hi * (1.0 / DIFF_DQK) + EPS))
            r = ys * inv * g
            if scale != 1.0:
                r = r * scale
            out_ref[:, j * LANES:(j + 1) * LANES] = r.astype(out_ref.dtype)

    rotary_store(proj(0), rq_ref, 1.0)
    rotary_store(proj(1), rk_ref, RET_DK ** -0.5)
    rv_ref[...] = proj(2).astype(BF16)
    g = proj(3)
    rg_ref[...] = (g * jax.nn.sigmoid(g)).astype(BF16)
    qknorm_store(proj(4), qg_ref, dq_ref, DIFF_DQK ** -0.5 * LOG2E)
    qknorm_store(proj(5), kg_ref, dk_ref, 1.0)
    dv = proj(6)
    ones = jnp.ones((VT_ROWS - DIFF_DV, dv.shape[0]), BF16)
    for hh in range(DIFF_HEADS):
        dvt_ref[hh * VT_ROWS:hh * VT_ROWS + DIFF_DV, :] = (
            dv[:, hh * DIFF_DV:(hh + 1) * DIFF_DV].T.astype(BF16))
        dvt_ref[hh * VT_ROWS + DIFF_DV:(hh + 1) * VT_ROWS, :] = ones


def _in_proj(x2, gain, w_in, cos_t, sin_t, qg, kg, seq, tm):
    T, D = x2.shape
    n_pos_blocks = seq // tm
    tok = lambda i: (i, 0)
    const = lambda i: (0, 0)
    pos = lambda i: (i % n_pos_blocks, 0)
    out = jax.ShapeDtypeStruct((T, SEG), BF16)
    return pl.pallas_call(
        _in_proj_kernel,
        grid=(T // tm,),
        in_specs=[
            pl.BlockSpec((tm, D), tok),
            pl.BlockSpec((1, D), const),
            pl.BlockSpec((D, N_SEG * SEG), const, pipeline_mode=pl.Buffered(1)),
            pl.BlockSpec((tm, LANES), pos),
            pl.BlockSpec((tm, LANES), pos),
            pl.BlockSpec((1, DIFF_DQK), const),
            pl.BlockSpec((1, DIFF_DQK), const),
        ],
        out_specs=[pl.BlockSpec((tm, SEG), tok)] * (N_SEG - 1)
        + [pl.BlockSpec((DIFF_HEADS * VT_ROWS, tm), lambda i: (0, i))],
        out_shape=[out] * (N_SEG - 1)
        + [jax.ShapeDtypeStruct((DIFF_HEADS * VT_ROWS, T), BF16)],
        scratch_shapes=[pltpu.VMEM((D, N_SEG * SEG), BF16)],
        compiler_params=pltpu.CompilerParams(
            dimension_semantics=("arbitrary",), vmem_limit_bytes=VMEM_LIMIT),
        name="in_proj",
    )(x2, gain, w_in, cos_t, sin_t, qg, kg)


def _retention_kernel(q_ref, k_ref, v_ref, g_ref, dmask_ref, zeta_ref, xi_ref, gchunk_ref,
                      gs_ref, o_ref, state_ref, *, chunk, n_chunks):
    @pl.when(pl.program_id(1) == 0)
    def _():
        state_ref[...] = jnp.zeros_like(state_ref)

    n_slabs = RET_WIDTH // LANES
    low = _low_half_mask((chunk, LANES))
    row = lax.broadcasted_iota(jnp.int32, (LANES, LANES), 0)
    col = lax.broadcasted_iota(jnp.int32, (LANES, LANES), 1)
    blockdiag = (row < RET_DK) == (col < RET_DK)
    nt = (((1,), (1,)), ((), ()))
    tn = (((0,), (0,)), ((), ()))

    for c in range(n_chunks):
        rows = slice(c * chunk, (c + 1) * chunk)
        for s in range(n_slabs):
            sl = slice(s * LANES, (s + 1) * LANES)
            zeta = zeta_ref[:, sl]
            xi = xi_ref[:, sl]
            gch = gchunk_ref[:, sl]
            q = q_ref[rows, sl]
            k = k_ref[rows, sl]
            v = v_ref[rows, sl]
            zero = jnp.zeros_like(q)
            q2 = jnp.concatenate([jnp.where(low, q, zero), jnp.where(low, zero, q)], axis=0)
            s2 = lax.dot_general(q2, k, nt, preferred_element_type=F32) * dmask_ref[s]
            o2 = jnp.dot(s2.astype(BF16), v, preferred_element_type=F32)
            r_prev = state_ref[s]
            o_inter = jnp.dot(q, r_prev.astype(BF16), preferred_element_type=F32) * xi
            o = jnp.where(low, o2[:chunk], o2[chunk:]) + o_inter
            kz = (k.astype(F32) * zeta).astype(BF16)
            kv = lax.dot_general(kz, v, tn, preferred_element_type=F32)
            state_ref[s] = r_prev * gch + jnp.where(blockdiag, kv, 0.0)
            sq = o * o
            m_lo = jnp.sum(jnp.where(low, sq, 0.0), axis=-1, keepdims=True)
            m_hi = jnp.sum(jnp.where(low, 0.0, sq), axis=-1, keepdims=True)
            inv = jnp.where(low, lax.rsqrt(m_lo * (1.0 / RET_DV) + EPS),
                            lax.rsqrt(m_hi * (1.0 / RET_DV) + EPS))
            out = o * inv * g_ref[rows, sl].astype(F32) * gs_ref[:, sl]
            o_ref[rows, sl] = out.astype(o_ref.dtype)


def _retention(rq, rk, rv, rg, dmask, zeta, xi, gchunk, gs_ret, batch, seq, ts, chunk):
    T = rq.shape[0]
    nblk = seq // ts
    tok = lambda b, i: (b * nblk + i, 0)
    c2 = lambda b, i: (0, 0)
    c3 = lambda b, i: (0, 0, 0)
    kern = functools.partial(_retention_kernel, chunk=chunk, n_chunks=ts // chunk)
    return pl.pallas_call(
        kern,
        grid=(batch, nblk),
        in_specs=[pl.BlockSpec((ts, RET_WIDTH), tok)] * 4 + [
            pl.BlockSpec((RET_HEADS // 2, 2 * chunk, chunk), c3),
            pl.BlockSpec((chunk, RET_WIDTH), c2),
            pl.BlockSpec((chunk, RET_WIDTH), c2),
            pl.BlockSpec((1, RET_WIDTH), c2),
            pl.BlockSpec((1, RET_WIDTH), c2),
        ],
        out_specs=pl.BlockSpec((ts, RET_WIDTH), tok),
        out_shape=jax.ShapeDtypeStruct((T, RET_WIDTH), BF16),
        scratch_shapes=[pltpu.VMEM((RET_WIDTH // LANES, LANES, LANES), F32)],
        compiler_params=pltpu.CompilerParams(
            dimension_semantics=("parallel", "arbitrary"), vmem_limit_bytes=VMEM_LIMIT),
        name="retention",
    )(rq, rk, rv, rg, dmask, zeta, xi, gchunk, gs_ret)


def _diff_attn_kernel(q_ref, k_ref, vt_ref, lam_ref, subln_ref, gs_ref, *refs,
                      n_cast, heads, tq, tk, rb, lam_init):
    w_refs = refs[:n_cast]
    o_ref = refs[n_cast]
    wb_refs = refs[n_cast + 1:2 * n_cast + 1]
    qs_all, s_all, mc_all, m_all, acc_all = refs[2 * n_cast + 1:]
    i = pl.program_id(2)
    nt = (((1,), (1,)), ((), ()))
    n_blocks = 2 * tq // rb
    n_diag = tq // tk
    n_full = i * n_diag
    head_ids = range(heads)

    def lanes(hh):
        return pl.ds(hh * LANES, LANES)

    def diag_kind(d, b):
        q_first = (b * rb) % tq
        if d * tk > q_first + rb - 1:
            return "skip"
        return "full" if (d + 1) * tk - 1 <= q_first else "mask"

    def scores(hh, j, b, key_offset=None):
        start = pl.multiple_of(j * tk, tk)
        cols = slice(b * rb, (b + 1) * rb)
        s = lax.dot_general(k_ref[pl.ds(start, tk), lanes(hh)], qs_all[hh, cols, :], nt,
                            preferred_element_type=F32)
        if key_offset is not None:
            kpos = lax.broadcasted_iota(jnp.int32, s.shape, 0) + key_offset
            qpos = lax.broadcasted_iota(jnp.int32, s.shape, 1) + (b * rb) % tq
            s = jnp.where(kpos <= qpos, s, -jnp.inf)
        s_all[hh, :, cols] = s
        mc_all[hh, :, cols] = jnp.broadcast_to(jnp.max(s, axis=0, keepdims=True),
                                               (F32_SUBLANES, rb))

    def consume(hh, j, b):
        start = pl.multiple_of(j * tk, tk)
        cols = slice(b * rb, (b + 1) * rb)
        m_prev = m_all[hh, :, cols]
        m_new = jnp.maximum(m_prev, mc_all[hh, :, cols])
        alpha = jnp.exp2(m_prev - m_new)
        p = jnp.exp2(s_all[hh, :, cols] - m_new[0:1, :])
        pv = jnp.dot(vt_ref[hh * VT_ROWS:(hh + 1) * VT_ROWS, pl.ds(start, tk)], p.astype(BF16),
                     preferred_element_type=F32)
        acc_all[hh, :, cols] = alpha[0:1, :] * acc_all[hh, :, cols] + pv
        m_all[hh, :, cols] = m_new

    for hh in head_ids:
        q = q_ref[:, lanes(hh)]
        low = _low_half_mask(q.shape)
        zero = jnp.zeros_like(q)
        qs_all[hh, :tq, :] = jnp.where(low, q, zero)
        qs_all[hh, tq:, :] = jnp.where(low, zero, q)

    ahead = n_blocks // 2
    order = sorted(range(n_blocks), key=lambda b: -((b * rb) % tq))
    assert all(diag_kind(0, b) == "full" for b in order[:ahead])
    for b in order[:ahead]:
        for hh in head_ids:
            scores(hh, 0, b)
    m_all[...] = jnp.full_like(m_all, -jnp.inf)
    acc_all[...] = jnp.zeros_like(acc_all)

    def full_chunks(j0, count):
        for u in range(count):
            j = j0 + u
            for pos, b in enumerate(order):
                for hh in head_ids:
                    if pos + ahead < n_blocks:
                        scores(hh, j, order[pos + ahead])
                    else:
                        scores(hh, j + 1, order[pos + ahead - n_blocks])
                    consume(hh, j, b)

    def body(t, carry):
        full_chunks(t * (2 * n_diag), 2 * n_diag)
        return carry

    lax.fori_loop(0, i // 2, body, 0)

    @pl.when(i % 2 == 1)
    def _():
        full_chunks((i - 1) * n_diag, n_diag)

    lv = lam_ref[...]
    lam = (jnp.exp(jnp.sum(lv[0:1] * lv[1:2], axis=-1, keepdims=True))
           - jnp.exp(jnp.sum(lv[2:3] * lv[3:4], axis=-1, keepdims=True)) + lam_init)

    def finalize(hh, d):
        a0 = acc_all[hh, :, d * tk:(d + 1) * tk]
        a1 = acc_all[hh, :, tq + d * tk:tq + (d + 1) * tk]
        o0 = a0[:DIFF_DV, :] * (1.0 / a0[DIFF_DV:DIFF_DV + 1, :])
        o1 = a1[:DIFF_DV, :] * (1.0 / a1[DIFF_DV:DIFF_DV + 1, :])
        o = (o0 - lam * o1).T
        out_gain = subln_ref[...] * (1.0 - lam_init) * gs_ref[:, lanes(hh)]
        o_ref[d * tk:(d + 1) * tk, lanes(hh)] = (o * _rms(o) * out_gain).astype(o_ref.dtype)

    for d in range(n_diag):
        for pos, b in enumerate(order):
            nd, npos = ((d, pos + ahead) if pos + ahead < n_blocks
                        else (d + 1, pos + ahead - n_blocks))
            for hh in head_ids:
                if nd < n_diag and diag_kind(nd, order[npos]) != "skip":
                    masked = diag_kind(nd, order[npos]) == "mask"
                    scores(hh, n_full + nd, order[npos], nd * tk if masked else None)
                if diag_kind(d, b) != "skip":
                    consume(hh, n_full + d, b)
        for hh in head_ids:
            finalize(hh, d)
        if d == 0:
            for w_ref, wb_ref in zip(w_refs, wb_refs):
                wb_ref[...] = w_ref[...].astype(wb_ref.dtype)


def _diff_attn(dq, dk, dvt, lam_p, subln, gs_diff, cast_weights, batch, seq, tq, tk, rb, heads,
               lam_init):
    T = dq.shape[0]
    nq = seq // tq
    n_groups = DIFF_HEADS // heads
    n_steps = batch * n_groups * nq
    assert seq % tq == 0 and tq % tk == 0 and tq % rb == 0 and DIFF_HEADS % heads == 0
    assert RET_WIDTH % (heads * LANES) == 0

    def cast_spec(w):
        rows = w.shape[0]
        n_blk = max(n for n in range(1, n_steps + 1)
                    if n_steps % n == 0 and rows % (n * BF16_SUBLANES) == 0)
        hold = n_steps // n_blk
        return pl.BlockSpec(
            (rows // n_blk, w.shape[1]),
            lambda b, g, i: (((b * n_groups + g) * nq + i) // hold, 0))

    w_specs = [cast_spec(w) for w in cast_weights]
    kern = functools.partial(_diff_attn_kernel, n_cast=len(cast_weights), heads=heads, tq=tq,
                             tk=tk, rb=rb, lam_init=lam_init)
    per_head = lambda shape, dtype: pltpu.VMEM((heads,) + shape, dtype)
    outs = pl.pallas_call(
        kern,
        grid=(batch, n_groups, nq),
        in_specs=[
            pl.BlockSpec((tq, heads * LANES), lambda b, g, i: (b * nq + i, g)),
            pl.BlockSpec((seq, heads * LANES), lambda b, g, i: (b, g)),
            pl.BlockSpec((heads * VT_ROWS, seq), lambda b, g, i: (g, b)),
            pl.BlockSpec((4, DIFF_DQK), lambda b, g, i: (0, 0)),
            pl.BlockSpec((1, LANES), lambda b, g, i: (0, 0)),
            pl.BlockSpec((1, heads * LANES),
                         lambda b, g, i: (0, RET_WIDTH // (heads * LANES) + g)),
        ] + w_specs,
        out_specs=[pl.BlockSpec((tq, heads * LANES), lambda b, g, i: (b * nq + i, g))]
        + w_specs,
        out_shape=[jax.ShapeDtypeStruct((T, DIFF_WIDTH), BF16)]
        + [jax.ShapeDtypeStruct(w.shape, BF16) for w in cast_weights],
        scratch_shapes=[per_head((2 * tq, LANES), BF16), per_head((tk, 2 * tq), F32),
                        per_head((F32_SUBLANES, 2 * tq), F32),
                        per_head((F32_SUBLANES, 2 * tq), F32),
                        per_head((VT_ROWS, 2 * tq), F32)],
        compiler_params=pltpu.CompilerParams(
            dimension_semantics=("parallel", "parallel", "arbitrary"),
            vmem_limit_bytes=VMEM_LIMIT),
        name="diff_attn",
    )(dq, dk, dvt, lam_p, subln, gs_diff, *cast_weights)
    return outs[0], outs[1:]


def _mix_xattn_kernel(x_ref, oret_ref, odiff_ref, wout_ref, nx_ref, xq_ref, qg_ref,
                      mem_ref, nm_ref, xkv_ref, kg_ref, xo_ref, out_ref, kmem_ref, vmem_ref,
                      *, tiles_per_batch):
    D = x_ref.shape[1]
    dh = D // XATTN_HEADS

    @pl.when(pl.program_id(0) % tiles_per_batch == 0)
    def _():
        m = mem_ref[...]
        hm = (m * _rms(m) * nm_ref[...]).astype(BF16)
        k = jnp.dot(hm, xkv_ref[:, :D], preferred_element_type=F32)
        vmem_ref[...] = jnp.dot(hm, xkv_ref[:, D:], preferred_element_type=F32).astype(BF16)
        for hh in range(XATTN_HEADS):
            kh = k[:, hh * dh:(hh + 1) * dh]
            kmem_ref[:, hh * dh:(hh + 1) * dh] = (kh * _rms(kh) * kg_ref[...]).astype(BF16)

    x1 = (x_ref[...]
          + jnp.dot(oret_ref[...], wout_ref[:RET_WIDTH, :], preferred_element_type=F32)
          + jnp.dot(odiff_ref[...], wout_ref[RET_WIDTH:, :], preferred_element_type=F32))
    h = (x1 * _rms(x1) * nx_ref[...]).astype(BF16)
    q = jnp.dot(h, xq_ref[...], preferred_element_type=F32)
    nt = (((1,), (1,)), ((), ()))
    outs = []
    for hh in range(XATTN_HEADS):
        sl = slice(hh * dh, (hh + 1) * dh)
        qh = q[:, sl]
        qn = (qh * _rms(qh) * qg_ref[...] * (dh ** -0.5)).astype(BF16)
        s = lax.dot_general(qn, kmem_ref[:, sl], nt, preferred_element_type=F32)
        e = jnp.exp(s - jnp.max(s, axis=-1, keepdims=True))
        p = e * (1.0 / jnp.sum(e, axis=-1, keepdims=True))
        outs.append(jnp.dot(p.astype(BF16), vmem_ref[:, sl],
                            preferred_element_type=F32).astype(BF16))
    o = jnp.concatenate(outs, axis=1)
    out_ref[...] = x1 + jnp.dot(o, xo_ref[...], preferred_element_type=F32)


def _mix_xattn(x2, o_ret, o_diff, w_out, nx, xq, qg, mem2, nm, xkv, kg, xo, seq, mem_len, tm):
    T, D = x2.shape
    nblk = seq // tm
    tok = lambda i: (i, 0)
    const = lambda i: (0, 0)
    memb = lambda i: (i // nblk, 0)
    return pl.pallas_call(
        functools.partial(_mix_xattn_kernel, tiles_per_batch=nblk),
        grid=(T // tm,),
        in_specs=[
            pl.BlockSpec((tm, D), tok),
            pl.BlockSpec((tm, RET_WIDTH), tok),
            pl.BlockSpec((tm, DIFF_WIDTH), tok),
            pl.BlockSpec((D, D), const),
            pl.BlockSpec((1, D), const),
            pl.BlockSpec((D, D), const),
            pl.BlockSpec((1, D // XATTN_HEADS), const),
            pl.BlockSpec((mem_len, D), memb),
            pl.BlockSpec((1, D), const),
            pl.BlockSpec((D, 2 * D), const),
            pl.BlockSpec((1, D // XATTN_HEADS), const),
            pl.BlockSpec((D, D), const),
        ],
        out_specs=pl.BlockSpec((tm, D), tok),
        out_shape=jax.ShapeDtypeStruct((T, D), F32),
        scratch_shapes=[pltpu.VMEM((mem_len, D), BF16), pltpu.VMEM((mem_len, D), BF16)],
        compiler_params=pltpu.CompilerParams(
            dimension_semantics=("arbitrary",), vmem_limit_bytes=VMEM_LIMIT),
        name="mix_xattn",
    )(x2, o_ret, o_diff, w_out, nx, xq, qg, mem2, nm, xkv, kg, xo)


def _ffn_kernel(x_ref, gain_ref, wg_ref, wu_ref, wd_ref, out_ref, *, hidden_chunk):
    x = x_ref[...]
    h = (x * _rms(x) * gain_ref[...]).astype(BF16)
    hidden = wg_ref.shape[1]
    acc = x
    for c in range(hidden // hidden_chunk):
        sl = slice(c * hidden_chunk, (c + 1) * hidden_chunk)
        g = jnp.dot(h, wg_ref[:, sl], preferred_element_type=F32)
        u = jnp.dot(h, wu_ref[:, sl], preferred_element_type=F32)
        a = (g * jax.nn.sigmoid(g) * u).astype(BF16)
        acc = acc + jnp.dot(a, wd_ref[sl, :], preferred_element_type=F32)
    out_ref[...] = acc


def _ffn(x2, gain, wg, wu, wd, tm, hidden_chunk):
    T, D = x2.shape
    hidden = wg.shape[1]
    tok = lambda i: (i, 0)
    const = lambda i: (0, 0)
    kern = functools.partial(_ffn_kernel, hidden_chunk=hidden_chunk)
    return pl.pallas_call(
        kern,
        grid=(T // tm,),
        in_specs=[
            pl.BlockSpec((tm, D), tok),
            pl.BlockSpec((1, D), const),
            pl.BlockSpec((D, hidden), const, pipeline_mode=pl.Buffered(1)),
            pl.BlockSpec((D, hidden), const, pipeline_mode=pl.Buffered(1)),
            pl.BlockSpec((hidden, D), const, pipeline_mode=pl.Buffered(1)),
        ],
        out_specs=pl.BlockSpec((tm, D), tok),
        out_shape=jax.ShapeDtypeStruct((T, D), F32),
        compiler_params=pltpu.CompilerParams(
            dimension_semantics=("parallel",), vmem_limit_bytes=VMEM_LIMIT),
        name="ffn",
    )(x2, gain, wg, wu, wd)


def _rotary_tables(seq):
    half = RET_DK // 2
    inv = 1.0 / (RET_THETA_BASE ** np.linspace(0.0, 1.0, half))
    ang = np.arange(seq, dtype=np.float64)[:, None] * inv[None, :]
    cos, sin = np.cos(ang), np.sin(ang)
    cos_t = np.tile(cos, (1, LANES // half))
    sin_t = np.tile(np.concatenate([-sin, sin], axis=1), (1, LANES // RET_DK))
    return cos_t.astype(np.float32), sin_t.astype(np.float32)


def _decay_tables(chunk):
    H = RET_HEADS
    log_g = np.log1p(-(2.0 ** (-5.0 - np.arange(H, dtype=np.float64))))
    idx = np.arange(chunk, dtype=np.float64)
    rel = idx[:, None] - idx[None, :]
    dmask = np.where(rel >= 0, np.exp(log_g[:, None, None] * np.maximum(rel, 0.0)), 0.0)
    zeta = np.exp(log_g[:, None] * (chunk - 1.0 - idx))
    xi = np.exp(log_g[:, None] * (idx + 1.0))
    g_chunk = np.exp(log_g * chunk)
    widen = lambda t: np.repeat(t.T, RET_DK, axis=1)
    f32 = lambda t: t.astype(np.float32)
    dmask = dmask.reshape(H // 2, 2 * chunk, chunk)
    return (f32(dmask), f32(widen(zeta)), f32(widen(xi)),
            f32(np.repeat(g_chunk, RET_DK)[None, :]))


def kernel(x, mem, norm_mix, w_in, diff_q_gain, diff_k_gain, diff_lambda, diff_subln,
           group_scale, w_out, norm_x, norm_mem, xq, xkv, xq_gain, xk_gain, xo,
           norm_ffn, w_gate, w_up, w_down):
    B, S, D = x.shape
    M = mem.shape[1]
    depth = norm_mix.shape[0]
    x2 = x.reshape(B * S, D)
    mem2 = mem.reshape(B * M, D)

    ret_chunk = 256
    cos_t, sin_t = _rotary_tables(S)
    dmask, zeta, xi, gchunk = _decay_tables(ret_chunk)
    row = lambda a: a.reshape(1, -1)

    for l in range(depth):
        lam_init = 0.8 - 0.6 * math.exp(-0.3 * l)
        rq, rk, rv, rg, dq, dk, dv = _in_proj(
            x2, row(norm_mix[l]), w_in[l], cos_t, sin_t,
            row(diff_q_gain[l]), row(diff_k_gain[l]), S, tm=1024)
        o_ret = _retention(rq, rk, rv, rg, dmask, zeta, xi, gchunk,
                           row(group_scale[l]), B, S, ts=2048, chunk=ret_chunk)
        later_weights = [w_out[l], xq[l], xkv[l], xo[l], w_gate[l], w_up[l], w_down[l]]
        o_diff, (w_out_b, xq_b, xkv_b, xo_b, w_gate_b, w_up_b, w_down_b) = _diff_attn(
            dq, dk, dv, diff_lambda[l], row(diff_subln[l]), row(group_scale[l]),
            later_weights, B, S, tq=1024, tk=512, rb=256, heads=2, lam_init=lam_init)
        x2 = _mix_xattn(x2, o_ret, o_diff, w_out_b, row(norm_x[l]), xq_b, row(xq_gain[l]),
                        mem2, row(norm_mem[l]), xkv_b, row(xk_gain[l]), xo_b, S, M, tm=1024)
        x2 = _ffn(x2, row(norm_ffn[l]), w_gate_b, w_up_b, w_down_b, tm=1024, hidden_chunk=256)
    return x2.reshape(B, S, D)
```

```python
import functools
import math

import jax
import jax.numpy as jnp
import numpy as np
from jax import lax
from jax.experimental import pallas as pl
from jax.experimental.pallas import tpu as pltpu

F32 = jnp.float32
BF16 = jnp.bfloat16

EPS = 1e-6
LANES = 128
VMEM_LIMIT = 56 * 1024 * 1024

RET_HEADS = 8
RET_DK = 64
RET_DV = 64
RET_WIDTH = 512
RET_THETA_BASE = 10000.0
DIFF_HEADS = 4
DIFF_DQK = 64
DIFF_DV = 128
DIFF_WIDTH = 512
XATTN_HEADS = 4
SEG = 512
N_SEG = 7
LOG2E = math.log2(math.e)
F32_SUBLANES = 8
BF16_SUBLANES = 16
VT_ROWS = DIFF_DV + BF16_SUBLANES


def _rms(x, axis=-1):
    return lax.rsqrt(jnp.mean(x * x, axis=axis, keepdims=True) + EPS)


def _low_half_mask(shape):
    lane = lax.broadcasted_iota(jnp.int32, shape, len(shape) - 1)
    return (lane % LANES) < (LANES // 2)


def _in_proj_kernel(x_ref, gain_ref, w_ref, cos_ref, sin_ref, qg_ref, kg_ref,
                    rq_ref, rk_ref, rv_ref, rg_ref, dq_ref, dk_ref, dvt_ref, wb_ref):
    @pl.when(pl.program_id(0) == 0)
    def _():
        for seg in range(N_SEG):
            cols = slice(seg * SEG, (seg + 1) * SEG)
            wb_ref[:, cols] = w_ref[:, cols].astype(BF16)

    x = x_ref[...]
    h = (x * _rms(x) * gain_ref[...]).astype(BF16)

    def proj(seg):
        return jnp.dot(h, wb_ref[:, seg * SEG:(seg + 1) * SEG], preferred_element_type=F32)

    cos = cos_ref[...]
    sin = sin_ref[...]
    low = _low_half_mask(cos.shape)
    half = RET_DK // 2
    first_half = (lax.broadcasted_iota(jnp.int32, cos.shape, 1) % RET_DK) < half

    def rotary_store(y, out_ref, scale):
        for j in range(SEG // LANES):
            ys = y[:, j * LANES:(j + 1) * LANES]
            partner = jnp.where(first_half, pltpu.roll(ys, LANES - half, 1),
                                pltpu.roll(ys, half, 1))
            r = ys * cos + partner * sin
            if scale != 1.0:
                r = r * scale
            out_ref[:, j * LANES:(j + 1) * LANES] = r.astype(out_ref.dtype)

    def qknorm_store(y, g_ref, out_ref, scale):
        g = jnp.concatenate([g_ref[...]] * (LANES // DIFF_DQK), axis=1)
        for j in range(SEG // LANES):
            ys = y[:, j * LANES:(j + 1) * LANES]
            sq = ys * ys
            s_lo = jnp.sum(jnp.where(low, sq, 0.0), axis=-1, keepdims=True)
            s_hi = jnp.sum(jnp.where(low, 0.0, sq), axis=-1, keepdims=True)
            inv = jnp.where(low, lax.rsqrt(s_lo * (1.0 / DIFF_DQK) + EPS),
                            lax.rsqrt(s_hi * (1.0 / DIFF_DQK) + EPS))
            r = ys * inv * g
            if scale != 1.0:
                r = r * scale
            out_ref[:, j * LANES:(j + 1) * LANES] = r.astype(out_ref.dtype)

    rotary_store(proj(0), rq_ref, 1.0)
    rotary_store(proj(1), rk_ref, RET_DK ** -0.5)
    rv_ref[...] = proj(2).astype(BF16)
    g = proj(3)
    rg_ref[...] = (g * jax.nn.sigmoid(g)).astype(BF16)
    qknorm_store(proj(4), qg_ref, dq_ref, DIFF_DQK ** -0.5 * LOG2E)
    qknorm_store(proj(5), kg_ref, dk_ref, 1.0)
    dv = proj(6)
    ones = jnp.ones((VT_ROWS - DIFF_DV, dv.shape[0]), BF16)
    for hh in range(DIFF_HEADS):
        dvt_ref[hh * VT_ROWS:hh * VT_ROWS + DIFF_DV, :] = (
            dv[:, hh * DIFF_DV:(hh + 1) * DIFF_DV].T.astype(BF16))
        dvt_ref[hh * VT_ROWS + DIFF_DV:(hh + 1) * VT_ROWS, :] = ones


def _in_proj(x2, gain, w_in, cos_t, sin_t, qg, kg, seq, tm):
    T, D = x2.shape
    n_pos_blocks = seq // tm
    tok = lambda i: (i, 0)
    const = lambda i: (0, 0)
    pos = lambda i: (i % n_pos_blocks, 0)
    out = jax.ShapeDtypeStruct((T, SEG), BF16)
    return pl.pallas_call(
        _in_proj_kernel,
        grid=(T // tm,),
        in_specs=[
            pl.BlockSpec((tm, D), tok),
            pl.BlockSpec((1, D), const),
            pl.BlockSpec((D, N_SEG * SEG), const, pipeline_mode=pl.Buffered(1)),
            pl.BlockSpec((tm, LANES), pos),
            pl.BlockSpec((tm, LANES), pos),
            pl.BlockSpec((1, DIFF_DQK), const),
            pl.BlockSpec((1, DIFF_DQK), const),
        ],
        out_specs=[pl.BlockSpec((tm, SEG), tok)] * (N_SEG - 1)
        + [pl.BlockSpec((DIFF_HEADS * VT_ROWS, tm), lambda i: (0, i))],
        out_shape=[out] * (N_SEG - 1)
        + [jax.ShapeDtypeStruct((DIFF_HEADS * VT_ROWS, T), BF16)],
        scratch_shapes=[pltpu.VMEM((D, N_SEG * SEG), BF16)],
        compiler_params=pltpu.CompilerParams(
            dimension_semantics=("arbitrary",), vmem_limit_bytes=VMEM_LIMIT),
        name="in_proj",
    )(x2, gain, w_in, cos_t, sin_t, qg, kg)


def _retention_kernel(q_ref, k_ref, v_ref, g_ref, dmask_ref, zeta_ref, xi_ref, gchunk_ref,
                      gs_ref, o_ref, state_ref, *, chunk, n_chunks):
    @pl.when(pl.program_id(1) == 0)
    def _():
        state_ref[...] = jnp.zeros_like(state_ref)

    n_slabs = RET_WIDTH // LANES
    low = _low_half_mask((chunk, LANES))
    row = lax.broadcasted_iota(jnp.int32, (LANES, LANES), 0)
    col = lax.broadcasted_iota(jnp.int32, (LANES, LANES), 1)
    blockdiag = (row < RET_DK) == (col < RET_DK)
    nt = (((1,), (1,)), ((), ()))
    tn = (((0,), (0,)), ((), ()))

    for c in range(n_chunks):
        rows = slice(c * chunk, (c + 1) * chunk)
        for s in range(n_slabs):
            sl = slice(s * LANES, (s + 1) * LANES)
            zeta = zeta_ref[:, sl]
            xi = xi_ref[:, sl]
            gch = gchunk_ref[:, sl]
            q = q_ref[rows, sl]
            k = k_ref[rows, sl]
            v = v_ref[rows, sl]
            zero = jnp.zeros_like(q)
            q2 = jnp.concatenate([jnp.where(low, q, zero), jnp.where(low, zero, q)], axis=0)
            s2 = lax.dot_general(q2, k, nt, preferred_element_type=F32) * dmask_ref[s]
            o2 = jnp.dot(s2.astype(BF16), v, preferred_element_type=F32)
            r_prev = state_ref[s]
            o_inter = jnp.dot(q, r_prev.astype(BF16), preferred_element_type=F32) * xi
            o = jnp.where(low, o2[:chunk], o2[chunk:]) + o_inter
            kz = (k.astype(F32) * zeta).astype(BF16)
            kv = lax.dot_general(kz, v, tn, preferred_element_type=F32)
            state_ref[s] = r_prev * gch + jnp.where(blockdiag, kv, 0.0)
            sq = o * o
            m_lo = jnp.sum(jnp.where(low, sq, 0.0), axis=-1, keepdims=True)
            m_hi = jnp.sum(jnp.where(low, 0.0, sq), axis=-1, keepdims=True)
            inv = jnp.where(low, lax.rsqrt(m_lo * (1.0 / RET_DV) + EPS),
                            lax.rsqrt(m_hi * (1.0 / RET_DV) + EPS))
            out = o * inv * g_ref[rows, sl].astype(F32) * gs_ref[:, sl]
            o_ref[rows, sl] = out.astype(o_ref.dtype)


def _retention(rq, rk, rv, rg, dmask, zeta, xi, gchunk, gs_ret, batch, seq, ts, chunk):
    T = rq.shape[0]
    nblk = seq // ts
    tok = lambda b, i: (b * nblk + i, 0)
    c2 = lambda b, i: (0, 0)
    c3 = lambda b, i: (0, 0, 0)
    kern = functools.partial(_retention_kernel, chunk=chunk, n_chunks=ts // chunk)
    return pl.pallas_call(
        kern,
        grid=(batch, nblk),
        in_specs=[pl.BlockSpec((ts, RET_WIDTH), tok)] * 4 + [
            pl.BlockSpec((RET_HEADS // 2, 2 * chunk, chunk), c3),
            pl.BlockSpec((chunk, RET_WIDTH), c2),
            pl.BlockSpec((chunk, RET_WIDTH), c2),
            pl.BlockSpec((1, RET_WIDTH), c2),
            pl.BlockSpec((1, RET_WIDTH), c2),
        ],
        out_specs=pl.BlockSpec((ts, RET_WIDTH), tok),
        out_shape=jax.ShapeDtypeStruct((T, RET_WIDTH), BF16),
        scratch_shapes=[pltpu.VMEM((RET_WIDTH // LANES, LANES, LANES), F32)],
        compiler_params=pltpu.CompilerParams(
            dimension_semantics=("parallel", "arbitrary"), vmem_limit_bytes=VMEM_LIMIT),
        name="retention",
    )(rq, rk, rv, rg, dmask, zeta, xi, gchunk, gs_ret)


def _diff_attn_kernel(q_ref, k_ref, vt_ref, lam_ref, subln_ref, gs_ref, *refs,
                      n_cast, heads, tq, tk, rb, lam_init):
    w_refs = refs[:n_cast]
    o_ref = refs[n_cast]
    wb_refs = refs[n_cast + 1:2 * n_cast + 1]
    qs_all, s_all, mc_all, m_all, acc_all = refs[2 * n_cast + 1:]
    i = pl.program_id(2)
    nt = (((1,), (1,)), ((), ()))
    n_blocks = 2 * tq // rb
    n_diag = tq // tk
    n_full = i * n_diag
    head_ids = range(heads)

    def lanes(hh):
        return pl.ds(hh * LANES, LANES)

    def diag_kind(d, b):
        q_first = (b * rb) % tq
        if d * tk > q_first + rb - 1:
            return "skip"
        return "full" if (d + 1) * tk - 1 <= q_first else "mask"

    def scores(hh, j, b, key_offset=None):
        start = pl.multiple_of(j * tk, tk)
        cols = slice(b * rb, (b + 1) * rb)
        s = lax.dot_general(k_ref[pl.ds(start, tk), lanes(hh)], qs_all[hh, cols, :], nt,
                            preferred_element_type=F32)
        if key_offset is not None:
            kpos = lax.broadcasted_iota(jnp.int32, s.shape, 0) + key_offset
            qpos = lax.broadcasted_iota(jnp.int32, s.shape, 1) + (b * rb) % tq
            s = jnp.where(kpos <= qpos, s, -jnp.inf)
        s_all[hh, :, cols] = s
        mc_all[hh, :, cols] = jnp.broadcast_to(jnp.max(s, axis=0, keepdims=True),
                                               (F32_SUBLANES, rb))

    def consume(hh, j, b):
        start = pl.multiple_of(j * tk, tk)
        cols = slice(b * rb, (b + 1) * rb)
        m_prev = m_all[hh, :, cols]
        m_new = jnp.maximum(m_prev, mc_all[hh, :, cols])
        alpha = jnp.exp2(m_prev - m_new)
        p = jnp.exp2(s_all[hh, :, cols] - m_new[0:1, :])
        pv = jnp.dot(vt_ref[hh * VT_ROWS:(hh + 1) * VT_ROWS, pl.ds(start, tk)], p.astype(BF16),
                     preferred_element_type=F32)
        acc_all[hh, :, cols] = alpha[0:1, :] * acc_all[hh, :, cols] + pv
        m_all[hh, :, cols] = m_new

    for hh in head_ids:
        q = q_ref[:, lanes(hh)]
        low = _low_half_mask(q.shape)
        zero = jnp.zeros_like(q)
        qs_all[hh, :tq, :] = jnp.where(low, q, zero)
        qs_all[hh, tq:, :] = jnp.where(low, zero, q)

    ahead = n_blocks // 2
    order = sorted(range(n_blocks), key=lambda b: -((b * rb) % tq))
    assert all(diag_kind(0, b) == "full" for b in order[:ahead])
    for b in order[:ahead]:
        for hh in head_ids:
            scores(hh, 0, b)
    m_all[...] = jnp.full_like(m_all, -jnp.inf)
    acc_all[...] = jnp.zeros_like(acc_all)

    def full_chunks(j0, count):
        for u in range(count):
            j = j0 + u
            for pos, b in enumerate(order):
                for hh in head_ids:
                    if pos + ahead < n_blocks:
                        scores(hh, j, order[pos + ahead])
                    else:
                        scores(hh, j + 1, order[pos + ahead - n_blocks])
                    consume(hh, j, b)

    def body(t, carry):
        full_chunks(t * (2 * n_diag), 2 * n_diag)
        return carry

    lax.fori_loop(0, i // 2, body, 0)

    @pl.when(i % 2 == 1)
    def _():
        full_chunks((i - 1) * n_diag, n_diag)

    lv = lam_ref[...]
    lam = (jnp.exp(jnp.sum(lv[0:1] * lv[1:2], axis=-1, keepdims=True))
           - jnp.exp(jnp.sum(lv[2:3] * lv[3:4], axis=-1, keepdims=True)) + lam_init)

    def finalize(hh, d):
        a0 = acc_all[hh, :, d * tk:(d + 1) * tk]
        a1 = acc_all[hh, :, tq + d * tk:tq + (d + 1) * tk]
        o0 = a0[:DIFF_DV, :] * (1.0 / a0[DIFF_DV:DIFF_DV + 1, :])
        o1 = a1[:DIFF_DV, :] * (1.0 / a1[DIFF_DV:DIFF_DV + 1, :])
        o = (o0 - lam * o1).T
        out_gain = subln_ref[...] * (1.0 - lam_init) * gs_ref[:, lanes(hh)]
        o_ref[d * tk:(d + 1) * tk, lanes(hh)] = (o * _rms(o) * out_gain).astype(o_ref.dtype)

    for d in range(n_diag):
        for pos, b in enumerate(order):
            nd, npos = ((d, pos + ahead) if pos + ahead < n_blocks
                        else (d + 1, pos + ahead - n_blocks))
            for hh in head_ids:
                if nd < n_diag and diag_kind(nd, order[npos]) != "skip":
                    masked = diag_kind(nd, order[npos]) == "mask"
                    scores(hh, n_full + nd, order[npos], nd * tk if masked else None)
                if diag_kind(d, b) != "skip":
                    consume(hh, n_full + d, b)
        for hh in head_ids:
            finalize(hh, d)
        if d == 0:
            for w_ref, wb_ref in zip(w_refs, wb_refs):
                wb_ref[...] = w_ref[...].astype(wb_ref.dtype)


def _diff_attn(dq, dk, dvt, lam_p, subln, gs_diff, cast_weights, batch, seq, tq, tk, rb, heads,
               lam_init):
    T = dq.shape[0]
    nq = seq // tq
    n_groups = DIFF_HEADS // heads
    n_steps = batch * n_groups * nq
    assert seq % tq == 0 and tq % tk == 0 and tq % rb == 0 and DIFF_HEADS % heads == 0
    assert RET_WIDTH % (heads * LANES) == 0

    def cast_spec(w):
        rows = w.shape[0]
        n_blk = max(n for n in range(1, n_steps + 1)
                    if n_steps % n == 0 and rows % (n * BF16_SUBLANES) == 0)
        hold = n_steps // n_blk
        return pl.BlockSpec(
            (rows // n_blk, w.shape[1]),
            lambda b, g, i: (((b * n_groups + g) * nq + i) // hold, 0))

    w_specs = [cast_spec(w) for w in cast_weights]
    kern = functools.partial(_diff_attn_kernel, n_cast=len(cast_weights), heads=heads, tq=tq,
                             tk=tk, rb=rb, lam_init=lam_init)
    per_head = lambda shape, dtype: pltpu.VMEM((heads,) + shape, dtype)
    outs = pl.pallas_call(
        kern,
        grid=(batch, n_groups, nq),
        in_specs=[
            pl.BlockSpec((tq, heads * LANES), lambda b, g, i: (b * nq + i, g)),
            pl.BlockSpec((seq, heads * LANES), lambda b, g, i: (b, g)),
            pl.BlockSpec((heads * VT_ROWS, seq), lambda b, g, i: (g, b)),
            pl.BlockSpec((4, DIFF_DQK), lambda b, g, i: (0, 0)),
            pl.BlockSpec((1, LANES), lambda b, g, i: (0, 0)),
            pl.BlockSpec((1, heads * LANES),
                         lambda b, g, i: (0, RET_WIDTH // (heads * LANES) + g)),
        ] + w_specs,
        out_specs=[pl.BlockSpec((tq, heads * LANES), lambda b, g, i: (b * nq + i, g))]
        + w_specs,
        out_shape=[jax.ShapeDtypeStruct((T, DIFF_WIDTH), BF16)]
        + [jax.ShapeDtypeStruct(w.shape, BF16) for w in cast_weights],
        scratch_shapes=[per_head((2 * tq, LANES), BF16), per_head((tk, 2 * tq), F32),
                        per_head((F32_SUBLANES, 2 * tq), F32),
                        per_head((F32_SUBLANES, 2 * tq), F32),
                        per_head((VT_ROWS, 2 * tq), F32)],
        compiler_params=pltpu.CompilerParams(
            dimension_semantics=("parallel", "parallel", "arbitrary"),
            vmem_limit_bytes=VMEM_LIMIT),
        name="diff_attn",
    )(dq, dk, dvt, lam_p, subln, gs_diff, *cast_weights)
    return outs[0], outs[1:]


def _mix_xattn_kernel(x_ref, oret_ref, odiff_ref, wout_ref, nx_ref, xq_ref, qg_ref,
                      mem_ref, nm_ref, xkv_ref, kg_ref, xo_ref, out_ref, kmem_ref, vmem_ref,
                      *, tiles_per_batch):
    D = x_ref.shape[1]
    dh = D // XATTN_HEADS

    @pl.when(pl.program_id(0) % tiles_per_batch == 0)
    def _():
        m = mem_ref[...]
        hm = (m * _rms(m) * nm_ref[...]).astype(BF16)
        k = jnp.dot(hm, xkv_ref[:, :D], preferred_element_type=F32)
        vmem_ref[...] = jnp.dot(hm, xkv_ref[:, D:], preferred_element_type=F32).astype(BF16)
        for hh in range(XATTN_HEADS):
            kh = k[:, hh * dh:(hh + 1) * dh]
            kmem_ref[:, hh * dh:(hh + 1) * dh] = (kh * _rms(kh) * kg_ref[...]).astype(BF16)

    x1 = (x_ref[...]
          + jnp.dot(oret_ref[...], wout_ref[:RET_WIDTH, :], preferred_element_type=F32)
          + jnp.dot(odiff_ref[...], wout_ref[RET_WIDTH:, :], preferred_element_type=F32))
    h = (x1 * _rms(x1) * nx_ref[...]).astype(BF16)
    q = jnp.dot(h, xq_ref[...], preferred_element_type=F32)
    nt = (((1,), (1,)), ((), ()))
    outs = []
    for hh in range(XATTN_HEADS):
        sl = slice(hh * dh, (hh + 1) * dh)
        qh = q[:, sl]
        qn = (qh * _rms(qh) * qg_ref[...] * (dh ** -0.5)).astype(BF16)
        s = lax.dot_general(qn, kmem_ref[:, sl], nt, preferred_element_type=F32)
        e = jnp.exp(s - jnp.max(s, axis=-1, keepdims=True))
        p = e * (1.0 / jnp.sum(e, axis=-1, keepdims=True))
        outs.append(jnp.dot(p.astype(BF16), vmem_ref[:, sl],
                            preferred_element_type=F32).astype(BF16))
    o = jnp.concatenate(outs, axis=1)
    out_ref[...] = x1 + jnp.dot(o, xo_ref[...], preferred_element_type=F32)


def _mix_xattn(x2, o_ret, o_diff, w_out, nx, xq, qg, mem2, nm, xkv, kg, xo, seq, mem_len, tm):
    T, D = x2.shape
    nblk = seq // tm
    tok = lambda i: (i, 0)
    const = lambda i: (0, 0)
    memb = lambda i: (i // nblk, 0)
    return pl.pallas_call(
        functools.partial(_mix_xattn_kernel, tiles_per_batch=nblk),
        grid=(T // tm,),
        in_specs=[
            pl.BlockSpec((tm, D), tok),
            pl.BlockSpec((tm, RET_WIDTH), tok),
            pl.BlockSpec((tm, DIFF_WIDTH), tok),
            pl.BlockSpec((D, D), const),
            pl.BlockSpec((1, D), const),
            pl.BlockSpec((D, D), const),
            pl.BlockSpec((1, D // XATTN_HEADS), const),
            pl.BlockSpec((mem_len, D), memb),
            pl.BlockSpec((1, D), const),
            pl.BlockSpec((D, 2 * D), const),
            pl.BlockSpec((1, D // XATTN_HEADS), const),
            pl.BlockSpec((D, D), const),
        ],
        out_specs=pl.BlockSpec((tm, D), tok),
        out_shape=jax.ShapeDtypeStruct((T, D), F32),
        scratch_shapes=[pltpu.VMEM((mem_len, D), BF16), pltpu.VMEM((mem_len, D), BF16)],
        compiler_params=pltpu.CompilerParams(
            dimension_semantics=("arbitrary",), vmem_limit_bytes=VMEM_LIMIT),
        name="mix_xattn",
    )(x2, o_ret, o_diff, w_out, nx, xq, qg, mem2, nm, xkv, kg, xo)


def _ffn_kernel(x_ref, gain_ref, wg_ref, wu_ref, wd_ref, out_ref, *, hidden_chunk):
    x = x_ref[...]
    h = (x * _rms(x) * gain_ref[...]).astype(BF16)
    hidden = wg_ref.shape[1]
    acc = x
    for c in range(hidden // hidden_chunk):
        sl = slice(c * hidden_chunk, (c + 1) * hidden_chunk)
        g = jnp.dot(h, wg_ref[:, sl], preferred_element_type=F32)
        u = jnp.dot(h, wu_ref[:, sl], preferred_element_type=F32)
        a = (g * jax.nn.sigmoid(g) * u).astype(BF16)
        acc = acc + jnp.dot(a, wd_ref[sl, :], preferred_element_type=F32)
    out_ref[...] = acc


def _ffn(x2, gain, wg, wu, wd, tm, hidden_chunk):
    T, D = x2.shape
    hidden = wg.shape[1]
    tok = lambda i: (i, 0)
    const = lambda i: (0, 0)
    kern = functools.partial(_ffn_kernel, hidden_chunk=hidden_chunk)
    return pl.pallas_call(
        kern,
        grid=(T // tm,),
        in_specs=[
            pl.BlockSpec((tm, D), tok),
            pl.BlockSpec((1, D), const),
            pl.BlockSpec((D, hidden), const, pipeline_mode=pl.Buffered(1)),
            pl.BlockSpec((D, hidden), const, pipeline_mode=pl.Buffered(1)),
            pl.BlockSpec((hidden, D), const, pipeline_mode=pl.Buffered(1)),
        ],
        out_specs=pl.BlockSpec((tm, D), tok),
        out_shape=jax.ShapeDtypeStruct((T, D), F32),
        compiler_params=pltpu.CompilerParams(
            dimension_semantics=("parallel",), vmem_limit_bytes=VMEM_LIMIT),
        name="ffn",
    )(x2, gain, wg, wu, wd)


def _rotary_tables(seq):
    half = RET_DK // 2
    inv = 1.0 / (RET_THETA_BASE ** np.linspace(0.0, 1.0, half))
    ang = np.arange(seq, dtype=np.float64)[:, None] * inv[None, :]
    cos, sin = np.cos(ang), np.sin(ang)
    cos_t = np.tile(cos, (1, LANES // half))
    sin_t = np.tile(np.concatenate([-sin, sin], axis=1), (1, LANES // RET_DK))
    return cos_t.astype(np.float32), sin_t.astype(np.float32)


def _decay_tables(chunk):
    H = RET_HEADS
    log_g = np.log1p(-(2.0 ** (-5.0 - np.arange(H, dtype=np.float64))))
    idx = np.arange(chunk, dtype=np.float64)
    rel = idx[:, None] - idx[None, :]
    dmask = np.where(rel >= 0, np.exp(log_g[:, None, None] * np.maximum(rel, 0.0)), 0.0)
    zeta = np.exp(log_g[:, None] * (chunk - 1.0 - idx))
    xi = np.exp(log_g[:, None] * (idx + 1.0))
    g_chunk = np.exp(log_g * chunk)
    widen = lambda t: np.repeat(t.T, RET_DK, axis=1)
    f32 = lambda t: t.astype(np.float32)
    dmask = dmask.reshape(H // 2, 2 * chunk, chunk)
    return (f32(dmask), f32(widen(zeta)), f32(widen(xi)),
            f32(np.repeat(g_chunk, RET_DK)[None, :]))


def kernel(x, mem, norm_mix, w_in, diff_q_gain, diff_k_gain, diff_lambda, diff_subln,
           group_scale, w_out, norm_x, norm_mem, xq, xkv, xq_gain, xk_gain, xo,
           norm_ffn, w_gate, w_up, w_down):
    B, S, D = x.shape
    M = mem.shape[1]
    depth = norm_mix.shape[0]
    x2 = x.reshape(B * S, D)
    mem2 = mem.reshape(B * M, D)

    ret_chunk = 256
    cos_t, sin_t = _rotary_tables(S)
    dmask, zeta, xi, gchunk = _decay_tables(ret_chunk)
    row = lambda a: a.reshape(1, -1)

    for l in range(depth):
        lam_init = 0.8 - 0.6 * math.exp(-0.3 * l)
        rq, rk, rv, rg, dq, dk, dv = _in_proj(
            x2, row(norm_mix[l]), w_in[l], cos_t, sin_t,
            row(diff_q_gain[l]), row(diff_k_gain[l]), S, tm=1024)
        o_ret = _retention(rq, rk, rv, rg, dmask, zeta, xi, gchunk,
                           row(group_scale[l]), B, S, ts=1024, chunk=ret_chunk)
        later_weights = [w_out[l], xq[l], xkv[l], xo[l], w_gate[l], w_up[l], w_down[l]]
        o_diff, (w_out_b, xq_b, xkv_b, xo_b, w_gate_b, w_up_b, w_down_b) = _diff_attn(
            dq, dk, dv, diff_lambda[l], row(diff_subln[l]), row(group_scale[l]),
            later_weights, B, S, tq=1024, tk=512, rb=512, heads=2, lam_init=lam_init)
        x2 = _mix_xattn(x2, o_ret, o_diff, w_out_b, row(norm_x[l]), xq_b, row(xq_gain[l]),
                        mem2, row(norm_mem[l]), xkv_b, row(xk_gain[l]), xo_b, S, M, tm=1024)
        x2 = _ffn(x2, row(norm_ffn[l]), w_gate_b, w_up_b, w_down_b, tm=1024, hidden_chunk=256)
    return x2.reshape(B, S, D)
```
